```python
import jax, jax.numpy as jnp
from jax import lax
import numpy as np

D_MODEL = 2048
BATCH = 2
SEQ = 16384
DEPTH = 1

PLE_DIM = 256
GDN_HEADS = 8
GDN_DK = 128
GDN_DV = 128
GDN_CONV = 4
GDN_CHUNK = 64
MOBA_HEADS = 8
MOBA_DH = 128
MOBA_BLOCK = 256
MOBA_TOPK = 3
MOBA_Q_BLOCK = 64
D_FF = -(-8 * D_MODEL // (3 * 256)) * 256
RMS_EPS = 1e-6
GDN_QK_W = GDN_HEADS * GDN_DK
GDN_V_W = GDN_HEADS * GDN_DV
MOBA_W = MOBA_HEADS * MOBA_DH
MIX_W = GDN_V_W + MOBA_W
IN_SPLITS = (GDN_QK_W, GDN_QK_W, GDN_V_W, GDN_V_W, GDN_HEADS, GDN_HEADS, MOBA_W, MOBA_W, MOBA_W)
IN_W = sum(IN_SPLITS)
CONV_W = 2 * GDN_QK_W + GDN_V_W

kernel_name = "hybrid_gdn_moba_parallel_heads"


def rms_norm(x, w, eps=RMS_EPS):
    x32 = x.astype(jnp.float32)
    y = x32 * lax.rsqrt(jnp.mean(x32 * x32, axis=-1, keepdims=True) + eps)
    return (y * w.astype(jnp.float32)).astype(x.dtype)


def l2_norm(x, eps=1e-6):
    return x * lax.rsqrt(jnp.sum(x * x, axis=-1, keepdims=True) + eps)


def causal_short_conv(x, w):
    K = w.shape[1]
    T = x.shape[1]
    xp = jnp.pad(x, ((0, 0), (K - 1, 0), (0, 0)))
    out = xp[:, 0:T] * w[:, 0]
    for j in range(1, K):
        out = out + xp[:, j:j + T] * w[:, j]
    return out


def chunk_gated_delta_rule(q, k, v, g, beta):
    B, T, H, Dk = q.shape
    Dv = v.shape[-1]
    C = GDN_CHUNK
    NC = T // C
    to_chunks = lambda t: t.reshape(B, NC, C, H, t.shape[-1]).transpose(0, 3, 1, 2, 4)
    q, k, v = to_chunks(q), to_chunks(k), to_chunks(v)
    g = g.reshape(B, NC, C, H).transpose(0, 3, 1, 2)
    beta = beta.reshape(B, NC, C, H).transpose(0, 3, 1, 2)
    gc = jnp.cumsum(g, axis=-1)
    tril = jnp.tril(jnp.ones((C, C), dtype=bool))
    strict = jnp.tril(jnp.ones((C, C), dtype=bool), -1)
    decay = jnp.exp(jnp.where(tril, gc[..., :, None] - gc[..., None, :], -jnp.inf))
    k_beta = k * beta[..., None]
    v_beta = v * beta[..., None]
    A = jnp.where(strict, jnp.einsum('bhncd,bhnsd->bhncs', k_beta, k) * decay, 0.0)
    eye = jnp.eye(C, dtype=A.dtype)
    Tm = lax.linalg.triangular_solve(eye + A, jnp.broadcast_to(eye, A.shape),
                                     left_side=True, lower=True)
    u = jnp.einsum('bhncs,bhnse->bhnce', Tm, v_beta)
    w = jnp.einsum('bhncs,bhnsd->bhncd', Tm, k_beta * jnp.exp(gc)[..., None])
    qk = jnp.where(tril, jnp.einsum('bhncd,bhnsd->bhncs', q, k) * decay, 0.0)
    g_last = gc[..., -1]
    k_dec = k * jnp.exp(g_last[..., None] - gc)[..., None]
    q_dec = q * jnp.exp(gc)[..., None]
    xs = tuple(jnp.moveaxis(t, 2, 0) for t in (q_dec, k_dec, u, w, qk, g_last))

    def step(S, inp):
        q_c, k_c, u_c, w_c, qk_c, gl = inp
        v_new = u_c - jnp.einsum('bhcd,bhde->bhce', w_c, S)
        o_c = jnp.einsum('bhcd,bhde->bhce', q_c, S) + jnp.einsum('bhcs,bhse->bhce', qk_c, v_new)
        S = S * jnp.exp(gl)[..., None, None] + jnp.einsum('bhcd,bhce->bhde', k_c, v_new)
        return S, o_c

    S0 = jnp.zeros((B, H, Dk, Dv), jnp.float32)
    _, o = lax.scan(step, S0, xs)
    return o.transpose(1, 0, 3, 2, 4).reshape(B, T, H, Dv)


def gated_deltanet(q, k, v, z, b, a, conv_w, A_log, dt_bias, norm_w):
    B, T, _ = q.shape
    dtype = q.dtype
    qkv = jax.nn.silu(causal_short_conv(jnp.concatenate([q, k, v], axis=-1), conv_w))
    q, k, v = jnp.split(qkv, [GDN_QK_W, 2 * GDN_QK_W], axis=-1)
    q = l2_norm(q.reshape(B, T, GDN_HEADS, GDN_DK).astype(jnp.float32)) * (GDN_DK ** -0.5)
    k = l2_norm(k.reshape(B, T, GDN_HEADS, GDN_DK).astype(jnp.float32))
    v = v.reshape(B, T, GDN_HEADS, GDN_DV).astype(jnp.float32)
    beta = jax.nn.sigmoid(b.astype(jnp.float32))
    g = -jnp.exp(A_log.astype(jnp.float32)) * jax.nn.softplus(a.astype(jnp.float32) + dt_bias.astype(jnp.float32))
    o = chunk_gated_delta_rule(q, k, v, g, beta)
    zh = z.reshape(B, T, GDN_HEADS, GDN_DV).astype(jnp.float32)
    o = rms_norm(o, norm_w) * jax.nn.silu(zh)
    return o.reshape(B, T, GDN_V_W).astype(dtype)


def moba_attention(q, k, v, q_gain, k_gain):
    B, T, _ = q.shape
    H, D, BLK, QB = MOBA_HEADS, MOBA_DH, MOBA_BLOCK, MOBA_Q_BLOCK
    heads = lambda t: t.reshape(B, T, H, D)
    q = rms_norm(heads(q), q_gain)
    k = rms_norm(heads(k), k_gain)
    v = heads(v)
    Tp = -(-T // BLK) * BLK
    pad = ((0, 0), (0, Tp - T), (0, 0), (0, 0))
    q, k, v = [jnp.pad(t, pad).transpose(0, 2, 1, 3) for t in (q, k, v)]
    NB = Tp // BLK
    kb = k.reshape(B, H, NB, BLK, D)
    vb = v.reshape(B, H, NB, BLK, D)
    k_mean = jnp.mean(kb.astype(jnp.float32), axis=3)
    q_blk = jnp.arange(Tp) // BLK
    gate = jnp.einsum('bhtd,bhnd->bhtn', q.astype(jnp.float32), k_mean)
    past = jnp.arange(NB)[None, :] < q_blk[:, None]
    gate = jnp.where(past, gate, -jnp.inf)
    n_sel = min(MOBA_TOPK, NB)
    _, sel = lax.top_k(gate, n_sel)
    valid = sel < q_blk[:, None]
    NQ = Tp // QB
    q_s = q.reshape(B, H, NQ, QB, D).transpose(2, 0, 1, 3, 4)
    sel_s = sel.reshape(B, H, NQ, QB, n_sel).transpose(2, 0, 1, 3, 4)
    valid_s = valid.reshape(B, H, NQ, QB, n_sel).transpose(2, 0, 1, 3, 4)
    b_ix = jnp.arange(B)[:, None, None, None]
    h_ix = jnp.arange(H)[None, :, None, None]
    scale = D ** -0.5

    def query_block(args):
        qi, q_c, sel_c, valid_c = args
        q0 = qi * QB
        own = q0 // BLK
        k_own = lax.dynamic_index_in_dim(kb, own, axis=2, keepdims=False)
        v_own = lax.dynamic_index_in_dim(vb, own, axis=2, keepdims=False)
        k_sel = kb[b_ix, h_ix, sel_c]
        v_sel = vb[b_ix, h_ix, sel_c]
        s_sel = jnp.einsum('bhqd,bhqnkd->bhqnk', q_c, k_sel, preferred_element_type=jnp.float32) * scale
        s_sel = jnp.where(valid_c[..., None], s_sel, -jnp.inf).reshape(B, H, QB, n_sel * BLK)
        s_own = jnp.einsum('bhqd,bhkd->bhqk', q_c, k_own, preferred_element_type=jnp.float32) * scale
        causal = (own * BLK + jnp.arange(BLK))[None, :] <= (q0 + jnp.arange(QB))[:, None]
        s_own = jnp.where(causal, s_own, -jnp.inf)
        probs = jax.nn.softmax(jnp.concatenate([s_sel, s_own], axis=-1), axis=-1).astype(v.dtype)
        p_sel = probs[..., :n_sel * BLK].reshape(B, H, QB, n_sel, BLK)
        p_own = probs[..., n_sel * BLK:]
        return (jnp.einsum('bhqnk,bhqnkd->bhqd', p_sel, v_sel)
                + jnp.einsum('bhqk,bhkd->bhqd', p_own, v_own))

    o = lax.map(query_block, (jnp.arange(NQ), q_s, sel_s, valid_s))
    o = o.transpose(1, 0, 3, 2, 4).reshape(B, Tp, H * D)
    return o[:, :T]


def setup_inputs(seed: int = 0) -> dict:
    key = jax.random.key(seed)
    ks = jax.random.split(key, 20)
    f32 = jnp.float32
    nrm = lambda k, shape, s: jax.random.normal(k, shape, f32) * s
    gain = lambda k, shape: 1.0 + 0.02 * jax.random.normal(k, shape, f32)
    dt = jnp.exp(jax.random.uniform(ks[6], (DEPTH, GDN_HEADS), f32, np.log(1e-3), np.log(1e-1)))
    return {
        "x": jax.random.normal(ks[0], (BATCH, SEQ, D_MODEL), f32),
        "p": jax.random.normal(ks[1], (DEPTH, BATCH, SEQ, PLE_DIM), f32),
        "attn_norm": gain(ks[2], (DEPTH, D_MODEL)),
        "w_in": nrm(ks[3], (DEPTH, D_MODEL, IN_W), D_MODEL ** -0.5),
        "conv_w": nrm(ks[4], (DEPTH, CONV_W, GDN_CONV), GDN_CONV ** -0.5),
        "A_log": jnp.log(jax.random.uniform(ks[5], (DEPTH, GDN_HEADS), f32, 1.0, 16.0)),
        "dt_bias": dt + jnp.log(-jnp.expm1(-dt)),
        "gdn_norm": gain(ks[7], (DEPTH, GDN_DV)),
        "q_norm": gain(ks[8], (DEPTH, MOBA_DH)),
        "k_norm": gain(ks[9], (DEPTH, MOBA_DH)),
        "w_o": nrm(ks[10], (DEPTH, MIX_W, D_MODEL), MIX_W ** -0.5),
        "ffn_norm": gain(ks[11], (DEPTH, D_MODEL)),
        "w_gate": nrm(ks[12], (DEPTH, D_MODEL, D_FF), D_MODEL ** -0.5),
        "w_up": nrm(ks[13], (DEPTH, D_MODEL, D_FF), D_MODEL ** -0.5),
        "w_down": nrm(ks[14], (DEPTH, D_FF, D_MODEL), D_FF ** -0.5),
        "ple_norm": gain(ks[15], (DEPTH, D_MODEL)),
        "w_ple_gate": nrm(ks[16], (DEPTH, D_MODEL, D_MODEL), D_MODEL ** -0.5),
        "w_ple_proj": nrm(ks[17], (DEPTH, PLE_DIM, D_MODEL), PLE_DIM ** -0.5),
    }


def reference(x, p, attn_norm, w_in, conv_w, A_log, dt_bias, gdn_norm, q_norm, k_norm,
              w_o, ffn_norm, w_gate, w_up, w_down, ple_norm, w_ple_gate, w_ple_proj):
    h = x
    split_at = list(np.cumsum(IN_SPLITS)[:-1])
    for i in range(DEPTH):
        xn = rms_norm(h, attn_norm[i])
        proj = xn @ w_in[i]
        gq, gk, gv, gz, gb, ga, mq, mk, mv = jnp.split(proj, split_at, axis=-1)
        o_gdn = gated_deltanet(gq, gk, gv, gz, gb, ga, conv_w[i], A_log[i], dt_bias[i], gdn_norm[i])
        o_moba = moba_attention(mq, mk, mv, q_norm[i], k_norm[i])
        h = h + jnp.concatenate([o_gdn, o_moba], axis=-1) @ w_o[i]
        hn = rms_norm(h, ffn_norm[i])
        h = h + (jax.nn.silu(hn @ w_gate[i]) * (hn @ w_up[i])) @ w_down[i]
        gate = jax.nn.sigmoid(rms_norm(h, ple_norm[i]) @ w_ple_gate[i])
        h = h + gate * (p[i] @ w_ple_proj[i])
    return h
```

```python
import functools

import jax
import jax.numpy as jnp
from jax import lax
from jax.experimental import pallas as pl
from jax.experimental.pallas import tpu as pltpu

D_MODEL = 2048
PLE_DIM = 256
GDN_HEADS = 8
GDN_DK = 128
GDN_DV = 128
GDN_CONV = 4
GDN_CHUNK = 64
MOBA_HEADS = 8
MOBA_DH = 128
MOBA_BLOCK = 256
MOBA_TOPK = 3
RMS_EPS = 1e-6
GDN_W = GDN_HEADS * GDN_DK
MOBA_W = MOBA_HEADS * MOBA_DH
LANES = 128
MASK_NEG = -1e30

F32 = jnp.float32
BF16 = jnp.bfloat16
HI = lax.Precision.HIGHEST

VMEM_LIMIT = 56 * 1024 * 1024


def _params(sem):
    return pltpu.CompilerParams(dimension_semantics=sem, vmem_limit_bytes=VMEM_LIMIT)


def _sigmoid(x):
    return 1.0 / (1.0 + jnp.exp(-x))


def _dot_t(a, b, precision=None):
    return lax.dot_general(a, b, (((1,), (1,)), ((), ())), precision=precision,
                           preferred_element_type=F32)


def _rmsnorm_kernel(x_ref, w_ref, o_ref):
    x = x_ref[...]
    y = x * lax.rsqrt(jnp.mean(x * x, axis=-1, keepdims=True) + RMS_EPS)
    o_ref[...] = (y * w_ref[...]).astype(o_ref.dtype)


def _rmsnorm(x, w, tm):
    m, d = x.shape
    return pl.pallas_call(
        _rmsnorm_kernel,
        grid=(m // tm,),
        in_specs=[pl.BlockSpec((tm, d), lambda i: (i, 0)), pl.BlockSpec((1, d), lambda i: (0, 0))],
        out_specs=pl.BlockSpec((tm, d), lambda i: (i, 0)),
        out_shape=jax.ShapeDtypeStruct((m, d), BF16),
        compiler_params=_params(("parallel",)),
        name="rmsnorm_cast",
    )(x, w.reshape(1, d))


def _proj_kernel(x_ref, w_ref, o_ref):
    o_ref[...] = jnp.dot(x_ref[...], w_ref[...], preferred_element_type=F32).astype(o_ref.dtype)


def _proj(x, w, out_dtype, tm, tn, name):
    m, k = x.shape
    n = w.shape[1]
    return pl.pallas_call(
        _proj_kernel,
        grid=(m // tm, n // tn),
        in_specs=[pl.BlockSpec((tm, k), lambda i, j: (i, 0)), pl.BlockSpec((k, tn), lambda i, j: (0, j))],
        out_specs=pl.BlockSpec((tm, tn), lambda i, j: (i, j)),
        out_shape=jax.ShapeDtypeStruct((m, n), out_dtype),
        compiler_params=_params(("parallel", "arbitrary")),
        name=name,
    )(x, w)


def _head_rmsnorm(y, gain, scale):
    outs = []
    for h in range(y.shape[1] // LANES):
        yh = y[:, h * LANES:(h + 1) * LANES]
        r = lax.rsqrt(jnp.mean(yh * yh, axis=-1, keepdims=True) + RMS_EPS)
        outs.append(yh * r * gain[:, h * LANES:(h + 1) * LANES] * scale)
    return jnp.concatenate(outs, axis=1)


def _proj_qnorm_kernel(x_ref, w_ref, g_ref, o_ref, *, scale):
    y = jnp.dot(x_ref[...], w_ref[...], preferred_element_type=F32)
    o_ref[...] = _head_rmsnorm(y, g_ref[...], scale).astype(o_ref.dtype)


def _proj_knorm_kernel(x_ref, w_ref, g_ref, o_ref, km_ref):
    y = jnp.dot(x_ref[...], w_ref[...], preferred_element_type=F32)
    yn = _head_rmsnorm(y, g_ref[...], 1.0)
    o_ref[...] = yn.astype(o_ref.dtype)
    tm = yn.shape[0]
    for r in range(tm // MOBA_BLOCK):
        blk = yn[r * MOBA_BLOCK:(r + 1) * MOBA_BLOCK]
        km_ref[r] = jnp.mean(blk, axis=0, keepdims=True)


def _proj_moba_q(x, w, gain, tm, tn):
    m, k = x.shape
    n = w.shape[1]
    g = jnp.tile(gain.reshape(1, MOBA_DH), (1, n // MOBA_DH))
    return pl.pallas_call(
        functools.partial(_proj_qnorm_kernel, scale=MOBA_DH ** -0.5),
        grid=(m // tm, n // tn),
        in_specs=[pl.BlockSpec((tm, k), lambda i, j: (i, 0)), pl.BlockSpec((k, tn), lambda i, j: (0, j)),
                  pl.BlockSpec((1, tn), lambda i, j: (0, j))],
        out_specs=pl.BlockSpec((tm, tn), lambda i, j: (i, j)),
        out_shape=jax.ShapeDtypeStruct((m, n), BF16),
        compiler_params=_params(("parallel", "arbitrary")),
        name="proj_moba_q",
    )(x, w, g)


def _proj_moba_k(x, w, gain, tm, tn):
    m, k = x.shape
    n = w.shape[1]
    g = jnp.tile(gain.reshape(1, MOBA_DH), (1, n // MOBA_DH))
    nb = tm // MOBA_BLOCK
    return pl.pallas_call(
        _proj_knorm_kernel,
        grid=(m // tm, n // tn),
        in_specs=[pl.BlockSpec((tm, k), lambda i, j: (i, 0)), pl.BlockSpec((k, tn), lambda i, j: (0, j)),
                  pl.BlockSpec((1, tn), lambda i, j: (0, j))],
        out_specs=[pl.BlockSpec((tm, tn), lambda i, j: (i, j)),
                   pl.BlockSpec((nb, 1, tn), lambda i, j: (i, 0, j))],
        out_shape=[jax.ShapeDtypeStruct((m, n), BF16),
                   jax.ShapeDtypeStruct((m // MOBA_BLOCK, 1, n), F32)],
        compiler_params=_params(("parallel", "arbitrary")),
        name="proj_moba_k",
    )(x, w, g)


def _gdn_kernel(alog_ref, dtb_ref, xq_ref, xk_ref, xv_ref, z_ref, ba_ref, wq_ref, wk_ref, wv_ref,
                gn_ref, o_ref, s_ref, tq_ref, tk_ref, tv_ref, q_s, k_s, v_s, g_s, b_s, *, tb):
    h = pl.program_id(1)
    t = pl.program_id(2)
    C = GDN_CHUNK

    @pl.when(t == 0)
    def _():
        s_ref[...] = jnp.zeros_like(s_ref)
        tq_ref[...] = jnp.zeros_like(tq_ref)
        tk_ref[...] = jnp.zeros_like(tk_ref)
        tv_ref[...] = jnp.zeros_like(tv_ref)

    def conv_silu(x_ref, tail_ref, w_ref):
        x = x_ref[0]
        xp = jnp.concatenate([tail_ref[...], x], axis=0)
        w = w_ref[...]
        y = x * w[GDN_CONV - 1:GDN_CONV]
        for j in range(GDN_CONV - 1):
            shifted = pltpu.roll(xp, GDN_CONV - 1 - j, axis=0)[8:8 + tb]
            y = y + shifted * w[j:j + 1]
        tail_ref[...] = x[tb - 8:tb]
        return y * _sigmoid(y)

    q = conv_silu(xq_ref, tq_ref, wq_ref)
    k = conv_silu(xk_ref, tk_ref, wk_ref)
    v_s[...] = conv_silu(xv_ref, tv_ref, wv_ref)
    q_s[...] = q * lax.rsqrt(jnp.sum(q * q, axis=-1, keepdims=True) + 1e-6) * (GDN_DK ** -0.5)
    k_s[...] = k * lax.rsqrt(jnp.sum(k * k, axis=-1, keepdims=True) + 1e-6)

    ba = ba_ref[0]
    lane = lax.broadcasted_iota(jnp.int32, ba.shape, 1)
    bcol = jnp.sum(jnp.where(lane == h, ba, 0.0), axis=1, keepdims=True)
    acol = jnp.sum(jnp.where(lane == h + GDN_HEADS, ba, 0.0), axis=1, keepdims=True)
    ones_row = jnp.ones((1, LANES), F32)
    a_exp = jnp.exp(ones_row * alog_ref[h])
    sp_in = acol + dtb_ref[h]
    softplus = jnp.maximum(sp_in, 0.0) + jnp.log1p(jnp.exp(-jnp.abs(sp_in)))
    g_s[...] = -(softplus * a_exp)
    b_s[...] = _sigmoid(bcol) * ones_row

    ri = lax.broadcasted_iota(jnp.int32, (C, C), 0)
    ci = lax.broadcasted_iota(jnp.int32, (C, C), 1)
    tril = ri >= ci
    strict = ri > ci
    eye = ri == ci
    tril_f = tril.astype(F32)
    ones_cc = jnp.ones((C, C), F32)
    eye_f = eye.astype(F32)
    gn = gn_ref[...]

    def chunk(c, carry):
        sl = pl.ds(pl.multiple_of(c * C, C), C)
        qc, kc, vc = q_s[sl, :], k_s[sl, :], v_s[sl, :]
        beta = b_s[sl, :]
        gcb = jnp.dot(tril_f, g_s[sl, :], precision=HI, preferred_element_type=F32)
        gcc = gcb[:, :C]
        grow = jnp.dot(ones_cc, jnp.where(eye, gcc, 0.0), precision=HI, preferred_element_type=F32)
        decay = jnp.exp(jnp.where(tril, gcc - grow, -jnp.inf))
        k_beta = kc * beta
        v_beta = vc * beta
        a_mat = jnp.where(strict, _dot_t(k_beta, kc, HI) * decay, 0.0)
        t_inv = eye_f - jnp.where((ri >> 1 == ci >> 1), a_mat, 0.0)
        s = 2
        while s < C:
            sh = s.bit_length()
            off = (ri >> sh == ci >> sh) & ((ri & (2 * s - 1)) >= s) & ((ci & (2 * s - 1)) < s)
            a_off = jnp.where(off, a_mat, 0.0)
            t_inv = t_inv - jnp.dot(t_inv, jnp.dot(a_off, t_inv, precision=HI, preferred_element_type=F32),
                                    precision=HI, preferred_element_type=F32)
            s *= 2
        egc = jnp.exp(gcb)
        u = jnp.dot(t_inv, v_beta, precision=HI, preferred_element_type=F32)
        w = jnp.dot(t_inv, k_beta * egc, precision=HI, preferred_element_type=F32)
        qk = jnp.where(tril, _dot_t(qc, kc, HI) * decay, 0.0)
        gl = gcb[C - 1:C, :]
        egl = jnp.exp(gl)
        k_dec = kc * jnp.exp(gl - gcb)
        q_dec = qc * egc
        s_mat = s_ref[...]
        v_new = u - jnp.dot(w, s_mat, precision=HI, preferred_element_type=F32)
        o = (jnp.dot(q_dec, s_mat, precision=HI, preferred_element_type=F32)
             + jnp.dot(qk, v_new, precision=HI, preferred_element_type=F32))
        s_ref[...] = s_mat * egl + lax.dot_general(k_dec, v_new, (((0,), (0,)), ((), ())), precision=HI,
                                                   preferred_element_type=F32)
        on = o * lax.rsqrt(jnp.mean(o * o, axis=-1, keepdims=True) + RMS_EPS) * gn
        zc = z_ref[0, sl, :]
        o_ref[0, sl, :] = (on * (zc * _sigmoid(zc))).astype(o_ref.dtype)
        return carry

    lax.fori_loop(0, tb // C, chunk, 0)


def _gdn(proj, ba, conv_wt, a_log, dt_bias, gnorm, tb):
    b, t, _ = proj.shape
    nh = GDN_HEADS
    blk = lambda off: pl.BlockSpec((1, tb, LANES), lambda bi, hi, ti: (bi, ti, off + hi))
    wblk = lambda off: pl.BlockSpec((GDN_CONV, LANES), lambda bi, hi, ti: (0, off + hi))
    smem = pl.BlockSpec(memory_space=pltpu.SMEM)
    return pl.pallas_call(
        functools.partial(_gdn_kernel, tb=tb),
        grid=(b, nh, t // tb),
        in_specs=[smem, smem, blk(0), blk(nh), blk(2 * nh), blk(3 * nh),
                  pl.BlockSpec((1, tb, LANES), lambda bi, hi, ti: (bi, ti, 0)),
                  wblk(0), wblk(nh), wblk(2 * nh),
                  pl.BlockSpec((1, LANES), lambda bi, hi, ti: (0, 0))],
        out_specs=pl.BlockSpec((1, tb, LANES), lambda bi, hi, ti: (bi, ti, hi)),
        out_shape=jax.ShapeDtypeStruct((b, t, GDN_W), BF16),
        scratch_shapes=[pltpu.VMEM((GDN_DK, GDN_DV), F32)] + [pltpu.VMEM((8, LANES), F32)] * 3
        + [pltpu.VMEM((tb, LANES), F32)] * 5,
        compiler_params=_params(("parallel", "parallel", "arbitrary")),
        name="gdn",
    )(a_log, dt_bias, proj, proj, proj, proj, ba, conv_wt, conv_wt, conv_wt, gnorm.reshape(1, GDN_DV))


def _moba_kernel(q_ref, k_ref, v_ref, km_ref, o_ref, *, nb):
    i = pl.program_id(2)
    BLK = MOBA_BLOCK
    q = q_ref[0]
    km = km_ref[0].astype(BF16)
    km = jnp.concatenate([km, jnp.zeros((LANES - nb, LANES), BF16)], axis=0)
    gate = _dot_t(q, km)
    col = lax.broadcasted_iota(jnp.int32, (BLK, LANES), 1)
    past = col < i
    g = jnp.where(past, gate, -jnp.inf)
    sel = col == i
    for _ in range(MOBA_TOPK):
        m = jnp.max(g, axis=1, keepdims=True)
        idx = jnp.min(jnp.where(g == m, col, LANES), axis=1, keepdims=True)
        hit = col == idx
        sel = sel | (hit & past)
        g = jnp.where(hit, -jnp.inf, g)
    bias = jnp.where(sel, 0.0, MASK_NEG).astype(BF16)
    q_aug = jnp.concatenate([q, bias], axis=1)

    lane_k = lax.broadcasted_iota(jnp.int32, (BLK, LANES), 1)
    row_i = lax.broadcasted_iota(jnp.int32, (BLK, BLK), 0)
    col_i = lax.broadcasted_iota(jnp.int32, (BLK, BLK), 1)

    def scores(j):
        start = pl.multiple_of(j * BLK, BLK)
        kj = k_ref[0, pl.ds(start, BLK), :]
        vj = v_ref[0, pl.ds(start, BLK), :]
        onehot = jnp.where(lane_k == j, 1.0, 0.0).astype(BF16)
        return _dot_t(q_aug, jnp.concatenate([kj, onehot], axis=1)), vj

    s0, v0 = scores(i)
    s0 = jnp.where(row_i >= col_i, s0, MASK_NEG)
    m0 = jnp.max(s0, axis=1, keepdims=True)
    p0 = jnp.exp(s0 - m0)
    l0 = jnp.sum(p0, axis=1, keepdims=True)
    acc0 = jnp.dot(p0.astype(BF16), v0, preferred_element_type=F32)

    def body(j, carry):
        m, l, acc = carry
        s, vj = scores(j)
        m_new = jnp.maximum(m, jnp.max(s, axis=1, keepdims=True))
        alpha = jnp.exp(m - m_new)
        p = jnp.exp(s - m_new)
        l = alpha * l + jnp.sum(p, axis=1, keepdims=True)
        acc = alpha * acc + jnp.dot(p.astype(BF16), vj, preferred_element_type=F32)
        return m_new, l, acc

    m, l, acc = lax.fori_loop(0, i, body, (m0, l0, acc0))
    o_ref[0] = (acc / l).astype(o_ref.dtype)


def _moba(q, k, v, kmean):
    b, t, _ = q.shape
    nb = t // MOBA_BLOCK
    assert nb <= LANES
    return pl.pallas_call(
        functools.partial(_moba_kernel, nb=nb),
        grid=(b, MOBA_HEADS, nb),
        in_specs=[pl.BlockSpec((1, MOBA_BLOCK, LANES), lambda bi, hi, qi: (bi, qi, hi)),
                  pl.BlockSpec((1, t, LANES), lambda bi, hi, qi: (bi, 0, hi)),
                  pl.BlockSpec((1, t, LANES), lambda bi, hi, qi: (bi, 0, hi)),
                  pl.BlockSpec((1, nb, LANES), lambda bi, hi, qi: (bi, 0, hi))],
        out_specs=pl.BlockSpec((1, MOBA_BLOCK, LANES), lambda bi, hi, qi: (bi, qi, hi)),
        out_shape=jax.ShapeDtypeStruct((b, t, MOBA_W), BF16),
        compiler_params=_params(("parallel", "parallel", "arbitrary")),
        name="moba",
    )(q, k, v, kmean)


def _oproj_kernel(og_ref, om_ref, wa_ref, wb_ref, x_ref, nw_ref, h_ref, hn_ref):
    y = jnp.dot(og_ref[...], wa_ref[...], preferred_element_type=F32)
    y = y + jnp.dot(om_ref[...], wb_ref[...], preferred_element_type=F32)
    h = x_ref[...] + y
    h_ref[...] = h
    hn = h * lax.rsqrt(jnp.mean(h * h, axis=-1, keepdims=True) + RMS_EPS)
    hn_ref[...] = (hn * nw_ref[...]).astype(hn_ref.dtype)


def _oproj(og, om, wa, wb, x, norm_w, tm):
    m, d = x.shape
    ka, kb = og.shape[1], om.shape[1]
    row = lambda w: pl.BlockSpec((tm, w), lambda i: (i, 0))
    full = lambda r, c: pl.BlockSpec((r, c), lambda i: (0, 0))
    return pl.pallas_call(
        _oproj_kernel,
        grid=(m // tm,),
        in_specs=[row(ka), row(kb), full(ka, d), full(kb, d), row(d), full(1, d)],
        out_specs=[row(d), row(d)],
        out_shape=[jax.ShapeDtypeStruct((m, d), F32), jax.ShapeDtypeStruct((m, d), BF16)],
        compiler_params=_params(("parallel",)),
        name="oproj",
    )(og, om, wa, wb, x, norm_w.reshape(1, d))


def _ffn_kernel(hn_ref, wg_ref, wu_ref, wd_ref, h_ref, nw_ref, h2_ref, hn2_ref, acc_ref):
    f = pl.program_id(1)

    @pl.when(f == 0)
    def _():
        acc_ref[...] = jnp.zeros_like(acc_ref)

    hn = hn_ref[...]
    g = jnp.dot(hn, wg_ref[...], preferred_element_type=F32)
    u = jnp.dot(hn, wu_ref[...], preferred_element_type=F32)
    a = (g * _sigmoid(g) * u).astype(BF16)
    acc_ref[...] += jnp.dot(a, wd_ref[...], preferred_element_type=F32)

    @pl.when(f == pl.num_programs(1) - 1)
    def _():
        h2 = h_ref[...] + acc_ref[...]
        h2_ref[...] = h2
        n = h2 * lax.rsqrt(jnp.mean(h2 * h2, axis=-1, keepdims=True) + RMS_EPS)
        hn2_ref[...] = (n * nw_ref[...]).astype(hn2_ref.dtype)


def _ffn(hn, wg, wu, wd, h, norm_w, tm, tf):
    m, d = h.shape
    dff = wg.shape[1]
    row = pl.BlockSpec((tm, d), lambda i, f: (i, 0))
    return pl.pallas_call(
        _ffn_kernel,
        grid=(m // tm, dff // tf),
        in_specs=[row, pl.BlockSpec((d, tf), lambda i, f: (0, f)), pl.BlockSpec((d, tf), lambda i, f: (0, f)),
                  pl.BlockSpec((tf, d), lambda i, f: (f, 0)), row, pl.BlockSpec((1, d), lambda i, f: (0, 0))],
        out_specs=[row, row],
        out_shape=[jax.ShapeDtypeStruct((m, d), F32), jax.ShapeDtypeStruct((m, d), BF16)],
        scratch_shapes=[pltpu.VMEM((tm, d), F32)],
        compiler_params=_params(("parallel", "arbitrary")),
        name="ffn",
    )(hn, wg, wu, wd, h, norm_w.reshape(1, d))


def _ple_kernel(hn_ref, wg_ref, p_ref, wp_ref, h_ref, o_ref):
    gate = _sigmoid(jnp.dot(hn_ref[...], wg_ref[...], preferred_element_type=F32))
    proj = jnp.dot(p_ref[...].astype(BF16), wp_ref[...], preferred_element_type=F32)
    o_ref[...] = h_ref[...] + gate * proj


def _ple(hn, wg, p, wp, h, tm, tn):
    m, d = h.shape
    kp = p.shape[1]
    return pl.pallas_call(
        _ple_kernel,
        grid=(m // tm, d // tn),
        in_specs=[pl.BlockSpec((tm, d), lambda i, j: (i, 0)), pl.BlockSpec((d, tn), lambda i, j: (0, j)),
                  pl.BlockSpec((tm, kp), lambda i, j: (i, 0)), pl.BlockSpec((kp, tn), lambda i, j: (0, j)),
                  pl.BlockSpec((tm, tn), lambda i, j: (i, j))],
        out_specs=pl.BlockSpec((tm, tn), lambda i, j: (i, j)),
        out_shape=jax.ShapeDtypeStruct((m, d), F32),
        compiler_params=_params(("parallel", "arbitrary")),
        name="ple",
    )(hn, wg, p, wp, h)


def _layer(h, p, attn_norm, w_in, conv_w, a_log, dt_bias, gdn_norm, q_norm, k_norm, w_o, ffn_norm,
           w_gate, w_up, w_down, ple_norm, w_ple_gate, w_ple_proj):
    b, t, d = h.shape
    m = b * t
    x2 = h.reshape(m, d)
    tm = min(1024, m)

    o_ba = 4 * GDN_W
    o_mq = o_ba + 2 * GDN_HEADS
    w_in16 = w_in.astype(BF16)
    w_gdn = w_in16[:, :o_ba]
    w_ba = jnp.pad(w_in16[:, o_ba:o_mq], ((0, 0), (0, LANES - 2 * GDN_HEADS)))
    w_mq = w_in16[:, o_mq:o_mq + MOBA_W]
    w_mk = w_in16[:, o_mq + MOBA_W:o_mq + 2 * MOBA_W]
    w_mv = w_in16[:, o_mq + 2 * MOBA_W:o_mq + 3 * MOBA_W]

    xn = _rmsnorm(x2, attn_norm, min(512, m))
    gproj = _proj(xn, w_gdn, F32, tm, 512, "proj_gdn")
    ba = _proj(xn, w_ba, F32, tm, LANES, "proj_ba")
    mq = _proj_moba_q(xn, w_mq, q_norm, tm, 512)
    mk, kmean = _proj_moba_k(xn, w_mk, k_norm, tm, 512)
    mv = _proj(xn, w_mv, BF16, tm, 512, "proj_moba_v")

    o_gdn = _gdn(gproj.reshape(b, t, 4 * GDN_W), ba.reshape(b, t, LANES), conv_w.T, a_log, dt_bias,
                 gdn_norm, min(512, t))
    o_moba = _moba(mq.reshape(b, t, MOBA_W), mk.reshape(b, t, MOBA_W), mv.reshape(b, t, MOBA_W),
                   kmean.reshape(b, t // MOBA_BLOCK, MOBA_W))

    w_o16 = w_o.astype(BF16)
    h1, hn = _oproj(o_gdn.reshape(m, GDN_W), o_moba.reshape(m, MOBA_W), w_o16[:GDN_W], w_o16[GDN_W:],
                    x2, ffn_norm, min(256, m))
    h2, hn2 = _ffn(hn, w_gate.astype(BF16), w_up.astype(BF16), w_down.astype(BF16), h1, ple_norm,
                   min(512, m), 512)
    h3 = _ple(hn2, w_ple_gate.astype(BF16), p.reshape(m, PLE_DIM), w_ple_proj.astype(BF16), h2, tm, 512)
    return h3.reshape(b, t, d)


def kernel(x, p, attn_norm, w_in, conv_w, A_log, dt_bias, gdn_norm, q_norm, k_norm, w_o, ffn_norm,
           w_gate, w_up, w_down, ple_norm, w_ple_gate, w_ple_proj):
    h = x
    for i in range(p.shape[0]):
        h = _layer(h, p[i], attn_norm[i], w_in[i], conv_w[i], A_log[i], dt_bias[i], gdn_norm[i],
                   q_norm[i], k_norm[i], w_o[i], ffn_norm[i], w_gate[i], w_up[i], w_down[i],
                   ple_norm[i], w_ple_gate[i], w_ple_proj[i])
    return h
```

```python
import functools

import jax
import jax.numpy as jnp
from jax import lax
from jax.experimental import pallas as pl
from jax.experimental.pallas import tpu as pltpu

D_MODEL = 2048
PLE_DIM = 256
GDN_HEADS = 8
GDN_DK = 128
GDN_DV = 128
GDN_CONV = 4
GDN_CHUNK = 64
MOBA_HEADS = 8
MOBA_DH = 128
MOBA_BLOCK = 256
MOBA_TOPK = 3
RMS_EPS = 1e-6
GDN_W = GDN_HEADS * GDN_DK
MOBA_W = MOBA_HEADS * MOBA_DH
LANES = 128
LOG2E = 1.4426950408889634
MASK_NEG = -1e30

F32 = jnp.float32
BF16 = jnp.bfloat16
HI = lax.Precision.HIGHEST

VMEM_LIMIT = 56 * 1024 * 1024


def _params(sem):
    return pltpu.CompilerParams(dimension_semantics=sem, vmem_limit_bytes=VMEM_LIMIT)


def _sigmoid(x):
    return 1.0 / (1.0 + jnp.exp(-x))


def _dot_t(a, b, precision=None):
    return lax.dot_general(a, b, (((1,), (1,)), ((), ())), precision=precision,
                           preferred_element_type=F32)


def _rmsnorm_kernel(x_ref, w_ref, o_ref):
    x = x_ref[...]
    y = x * lax.rsqrt(jnp.mean(x * x, axis=-1, keepdims=True) + RMS_EPS)
    o_ref[...] = (y * w_ref[...]).astype(o_ref.dtype)


def _rmsnorm(x, w, tm):
    m, d = x.shape
    return pl.pallas_call(
        _rmsnorm_kernel,
        grid=(m // tm,),
        in_specs=[pl.BlockSpec((tm, d), lambda i: (i, 0)), pl.BlockSpec((1, d), lambda i: (0, 0))],
        out_specs=pl.BlockSpec((tm, d), lambda i: (i, 0)),
        out_shape=jax.ShapeDtypeStruct((m, d), BF16),
        compiler_params=_params(("parallel",)),
        name="rmsnorm_cast",
    )(x, w.reshape(1, d))


def _proj_kernel(x_ref, w_ref, o_ref):
    o_ref[...] = jnp.dot(x_ref[...], w_ref[...], preferred_element_type=F32).astype(o_ref.dtype)


def _proj(x, w, out_dtype, tm, tn, name):
    m, k = x.shape
    n = w.shape[1]
    return pl.pallas_call(
        _proj_kernel,
        grid=(m // tm, n // tn),
        in_specs=[pl.BlockSpec((tm, k), lambda i, j: (i, 0)), pl.BlockSpec((k, tn), lambda i, j: (0, j))],
        out_specs=pl.BlockSpec((tm, tn), lambda i, j: (i, j)),
        out_shape=jax.ShapeDtypeStruct((m, n), out_dtype),
        compiler_params=_params(("parallel", "arbitrary")),
        name=name,
    )(x, w)


def _head_rmsnorm(y, gain, scale):
    outs = []
    for h in range(y.shape[1] // LANES):
        yh = y[:, h * LANES:(h + 1) * LANES]
        r = lax.rsqrt(jnp.mean(yh * yh, axis=-1, keepdims=True) + RMS_EPS)
        outs.append(yh * r * gain[:, h * LANES:(h + 1) * LANES] * scale)
    return jnp.concatenate(outs, axis=1)


def _proj_qnorm_kernel(x_ref, w_ref, g_ref, o_ref, *, scale):
    y = jnp.dot(x_ref[...], w_ref[...], preferred_element_type=F32)
    o_ref[...] = _head_rmsnorm(y, g_ref[...], scale).astype(o_ref.dtype)


def _proj_knorm_kernel(x_ref, w_ref, g_ref, o_ref, km_ref, *, nb_seq):
    y = jnp.dot(x_ref[...], w_ref[...], preferred_element_type=F32)
    yn = _head_rmsnorm(y, g_ref[...], 1.0)
    tm = yn.shape[0]
    row = lax.broadcasted_iota(jnp.int32, (tm, LANES), 0) + pl.program_id(0) * tm
    lane = lax.broadcasted_iota(jnp.int32, (tm, LANES), 1)
    onehot = jnp.where(lane == lax.rem(row // MOBA_BLOCK, nb_seq), 1.0, 0.0).astype(o_ref.dtype)
    yb = yn.astype(o_ref.dtype)
    parts = []
    for h in range(yn.shape[1] // LANES):
        parts += [yb[:, h * LANES:(h + 1) * LANES], onehot]
    o_ref[...] = jnp.concatenate(parts, axis=1)
    for r in range(tm // MOBA_BLOCK):
        blk = yn[r * MOBA_BLOCK:(r + 1) * MOBA_BLOCK]
        km_ref[r] = jnp.mean(blk, axis=0, keepdims=True)


def _proj_moba_q(x, w, gain, tm, tn):
    m, k = x.shape
    n = w.shape[1]
    g = jnp.tile(gain.reshape(1, MOBA_DH), (1, n // MOBA_DH))
    return pl.pallas_call(
        functools.partial(_proj_qnorm_kernel, scale=MOBA_DH ** -0.5 * LOG2E),
        grid=(m // tm, n // tn),
        in_specs=[pl.BlockSpec((tm, k), lambda i, j: (i, 0)), pl.BlockSpec((k, tn), lambda i, j: (0, j)),
                  pl.BlockSpec((1, tn), lambda i, j: (0, j))],
        out_specs=pl.BlockSpec((tm, tn), lambda i, j: (i, j)),
        out_shape=jax.ShapeDtypeStruct((m, n), BF16),
        compiler_params=_params(("parallel", "arbitrary")),
        name="proj_moba_q",
    )(x, w, g)


def _proj_moba_k(x, w, gain, tm, tn, nb_seq):
    m, k = x.shape
    n = w.shape[1]
    g = jnp.tile(gain.reshape(1, MOBA_DH), (1, n // MOBA_DH))
    nb = tm // MOBA_BLOCK
    return pl.pallas_call(
        functools.partial(_proj_knorm_kernel, nb_seq=nb_seq),
        grid=(m // tm, n // tn),
        in_specs=[pl.BlockSpec((tm, k), lambda i, j: (i, 0)), pl.BlockSpec((k, tn), lambda i, j: (0, j)),
                  pl.BlockSpec((1, tn), lambda i, j: (0, j))],
        out_specs=[pl.BlockSpec((tm, 2 * tn), lambda i, j: (i, j)),
                   pl.BlockSpec((nb, 1, tn), lambda i, j: (i, 0, j))],
        out_shape=[jax.ShapeDtypeStruct((m, 2 * n), BF16),
                   jax.ShapeDtypeStruct((m // MOBA_BLOCK, 1, n), F32)],
        compiler_params=_params(("parallel", "arbitrary")),
        name="proj_moba_k",
    )(x, w, g)


def _gdn_kernel(alog_ref, dtb_ref, xq_ref, xk_ref, xv_ref, z_ref, ba_ref, wq_ref, wk_ref, wv_ref,
                gn_ref, o_ref, s_ref, tq_ref, tk_ref, tv_ref, q_s, k_s, v_s, g_s, b_s, *, tb):
    h = pl.program_id(1)
    t = pl.program_id(2)
    C = GDN_CHUNK

    @pl.when(t == 0)
    def _():
        s_ref[...] = jnp.zeros_like(s_ref)
        tq_ref[...] = jnp.zeros_like(tq_ref)
        tk_ref[...] = jnp.zeros_like(tk_ref)
        tv_ref[...] = jnp.zeros_like(tv_ref)

    def conv_silu(x_ref, tail_ref, w_ref):
        x = x_ref[0]
        xp = jnp.concatenate([tail_ref[...], x], axis=0)
        w = w_ref[...]
        y = x * w[GDN_CONV - 1:GDN_CONV]
        for j in range(GDN_CONV - 1):
            shifted = pltpu.roll(xp, GDN_CONV - 1 - j, axis=0)[8:8 + tb]
            y = y + shifted * w[j:j + 1]
        tail_ref[...] = x[tb - 8:tb]
        return y * _sigmoid(y)

    q = conv_silu(xq_ref, tq_ref, wq_ref)
    k = conv_silu(xk_ref, tk_ref, wk_ref)
    v_s[...] = conv_silu(xv_ref, tv_ref, wv_ref)
    q_s[...] = q * lax.rsqrt(jnp.sum(q * q, axis=-1, keepdims=True) + 1e-6) * (GDN_DK ** -0.5)
    k_s[...] = k * lax.rsqrt(jnp.sum(k * k, axis=-1, keepdims=True) + 1e-6)

    ba = ba_ref[0]
    lane = lax.broadcasted_iota(jnp.int32, ba.shape, 1)
    bcol = jnp.sum(jnp.where(lane == h, ba, 0.0), axis=1, keepdims=True)
    acol = jnp.sum(jnp.where(lane == h + GDN_HEADS, ba, 0.0), axis=1, keepdims=True)
    ones_row = jnp.ones((1, LANES), F32)
    a_exp = jnp.exp(ones_row * alog_ref[h])
    sp_in = acol + dtb_ref[h]
    softplus = jnp.maximum(sp_in, 0.0) + jnp.log1p(jnp.exp(-jnp.abs(sp_in)))
    g_s[...] = -(softplus * a_exp)
    b_s[...] = _sigmoid(bcol) * ones_row

    ri = lax.broadcasted_iota(jnp.int32, (C, C), 0)
    ci = lax.broadcasted_iota(jnp.int32, (C, C), 1)
    tril = ri >= ci
    strict = ri > ci
    eye = ri == ci
    tril_f = tril.astype(F32)
    ones_cc = jnp.ones((C, C), F32)
    eye_f = eye.astype(F32)
    gn = gn_ref[...]

    def chunk(c, carry):
        sl = pl.ds(pl.multiple_of(c * C, C), C)
        qc, kc, vc = q_s[sl, :], k_s[sl, :], v_s[sl, :]
        beta = b_s[sl, :]
        gcb = jnp.dot(tril_f, g_s[sl, :], precision=HI, preferred_element_type=F32)
        gcc = gcb[:, :C]
        grow = jnp.dot(ones_cc, jnp.where(eye, gcc, 0.0), precision=HI, preferred_element_type=F32)
        decay = jnp.exp(jnp.where(tril, gcc - grow, -jnp.inf))
        k_beta = kc * beta
        v_beta = vc * beta
        a_mat = jnp.where(strict, _dot_t(k_beta, kc, HI) * decay, 0.0)
        t_inv = eye_f - jnp.where((ri >> 1 == ci >> 1), a_mat, 0.0)
        s = 2
        while s < C:
            sh = s.bit_length()
            off = (ri >> sh == ci >> sh) & ((ri & (2 * s - 1)) >= s) & ((ci & (2 * s - 1)) < s)
            a_off = jnp.where(off, a_mat, 0.0)
            t_inv = t_inv - jnp.dot(t_inv, jnp.dot(a_off, t_inv, precision=HI, preferred_element_type=F32),
                                    precision=HI, preferred_element_type=F32)
            s *= 2
        egc = jnp.exp(gcb)
        u = jnp.dot(t_inv, v_beta, precision=HI, preferred_element_type=F32)
        w = jnp.dot(t_inv, k_beta * egc, precision=HI, preferred_element_type=F32)
        qk = jnp.where(tril, _dot_t(qc, kc, HI) * decay, 0.0)
        gl = gcb[C - 1:C, :]
        egl = jnp.exp(gl)
        k_dec = kc * jnp.exp(gl - gcb)
        q_dec = qc * egc
        s_mat = s_ref[...]
        v_new = u - jnp.dot(w, s_mat, precision=HI, preferred_element_type=F32)
        o = (jnp.dot(q_dec, s_mat, precision=HI, preferred_element_type=F32)
             + jnp.dot(qk, v_new, precision=HI, preferred_element_type=F32))
        s_ref[...] = s_mat * egl + lax.dot_general(k_dec, v_new, (((0,), (0,)), ((), ())), precision=HI,
                                                   preferred_element_type=F32)
        on = o * lax.rsqrt(jnp.mean(o * o, axis=-1, keepdims=True) + RMS_EPS) * gn
        zc = z_ref[0, sl, :]
        o_ref[0, sl, :] = (on * (zc * _sigmoid(zc))).astype(o_ref.dtype)
        return carry

    lax.fori_loop(0, tb // C, chunk, 0)


def _gdn(proj, ba, conv_wt, a_log, dt_bias, gnorm, tb):
    b, t, _ = proj.shape
    nh = GDN_HEADS
    blk = lambda off: pl.BlockSpec((1, tb, LANES), lambda bi, hi, ti: (bi, ti, off + hi))
    wblk = lambda off: pl.BlockSpec((GDN_CONV, LANES), lambda bi, hi, ti: (0, off + hi))
    smem = pl.BlockSpec(memory_space=pltpu.SMEM)
    return pl.pallas_call(
        functools.partial(_gdn_kernel, tb=tb),
        grid=(b, nh, t // tb),
        in_specs=[smem, smem, blk(0), blk(nh), blk(2 * nh), blk(3 * nh),
                  pl.BlockSpec((1, tb, LANES), lambda bi, hi, ti: (bi, ti, 0)),
                  wblk(0), wblk(nh), wblk(2 * nh),
                  pl.BlockSpec((1, LANES), lambda bi, hi, ti: (0, 0))],
        out_specs=pl.BlockSpec((1, tb, LANES), lambda bi, hi, ti: (bi, ti, hi)),
        out_shape=jax.ShapeDtypeStruct((b, t, GDN_W), BF16),
        scratch_shapes=[pltpu.VMEM((GDN_DK, GDN_DV), F32)] + [pltpu.VMEM((8, LANES), F32)] * 3
        + [pltpu.VMEM((tb, LANES), F32)] * 5,
        compiler_params=_params(("parallel", "parallel", "arbitrary")),
        name="gdn",
    )(a_log, dt_bias, proj, proj, proj, proj, ba, conv_wt, conv_wt, conv_wt, gnorm.reshape(1, GDN_DV))


def _moba_kernel(qt_ref, ka_ref, vt_ref, km_ref, o_ref, qa_ref, acc_ref, m_ref, l_ref, sa_ref, sb_ref, *,
                 nb, group):
    i = pl.program_id(2)
    BLK = MOBA_BLOCK
    nbp = -(-nb // 16) * 16
    qt = qt_ref[0]
    km = km_ref[0].astype(BF16)
    if nbp > nb:
        km = jnp.concatenate([km, jnp.zeros((nbp - nb, LANES), BF16)], axis=0)
    gate = jnp.dot(km, qt, preferred_element_type=F32)
    row = lax.broadcasted_iota(jnp.int32, (nbp, BLK), 0)
    past = row < i
    g = jnp.where(past, gate, -jnp.inf)
    sel = row < 0
    for _ in range(MOBA_TOPK):
        m = jnp.max(g, axis=0, keepdims=True)
        idx = jnp.min(jnp.where(g == m, row, nbp), axis=0, keepdims=True)
        hit = row == idx
        sel = sel | (hit & past)
        g = jnp.where(hit, -jnp.inf, g)
    bias = jnp.where(sel, 0.0, MASK_NEG).astype(BF16)
    qa_ref[0:LANES, :] = qt
    qa_ref[LANES:LANES + nbp, :] = bias
    if nbp < LANES:
        qa_ref[LANES + nbp:, :] = jnp.zeros((LANES - nbp, BLK), BF16)

    def pv(p, start, width):
        return jnp.dot(vt_ref[0, :, pl.ds(start, width)], p.astype(BF16), preferred_element_type=F32)

    start0 = pl.multiple_of(i * BLK, BLK)
    s0 = jnp.dot(ka_ref[0, pl.ds(start0, BLK), 0:LANES], qt, preferred_element_type=F32)
    key_i = lax.broadcasted_iota(jnp.int32, (BLK, BLK), 0)
    qry_i = lax.broadcasted_iota(jnp.int32, (BLK, BLK), 1)
    s0 = jnp.where(key_i <= qry_i, s0, MASK_NEG)
    m0 = jnp.max(s0, axis=0, keepdims=True)
    p0 = jnp.exp2(s0 - m0)
    m_ref[...] = m0
    l_ref[...] = jnp.sum(p0, axis=0, keepdims=True)
    acc_ref[...] = pv(p0, start0, BLK)

    span = group * BLK
    n_pairs = (i + 2 * group - 1) // (2 * group)
    last = nb // group - 1

    def qk(g_idx, dst_ref):
        start = pl.multiple_of(g_idx * span, span)
        dst_ref[...] = jnp.dot(ka_ref[0, pl.ds(start, span), :], qa_ref[...], preferred_element_type=F32)

    def update(src_ref, g_idx):
        s = src_ref[...]
        m_old = m_ref[...]
        m_new = jnp.maximum(m_old, jnp.max(s, axis=0, keepdims=True))
        alpha = jnp.exp2(m_old - m_new)
        p = jnp.exp2(s - m_new)
        l_ref[...] = alpha * l_ref[...] + jnp.sum(p, axis=0, keepdims=True)
        m_ref[...] = m_new
        acc_ref[...] = alpha * acc_ref[...] + pv(p, pl.multiple_of(g_idx * span, span), span)

    qk(0, sa_ref)

    def body(jj, carry):
        qk(2 * jj + 1, sb_ref)
        update(sa_ref, 2 * jj)
        qk(jnp.minimum(2 * jj + 2, last), sa_ref)
        update(sb_ref, 2 * jj + 1)
        return carry

    lax.fori_loop(0, n_pairs, body, 0)
    o = acc_ref[...] * (1.0 / l_ref[...])
    o_ref[0] = o.T.astype(o_ref.dtype)


def _moba(qt, kaug, vt, kmean):
    b, _, t = qt.shape
    nb = t // MOBA_BLOCK
    group = 2
    assert nb <= LANES and nb % (2 * group) == 0
    return pl.pallas_call(
        functools.partial(_moba_kernel, nb=nb, group=group),
        grid=(b, MOBA_HEADS, nb),
        in_specs=[pl.BlockSpec((1, LANES, MOBA_BLOCK), lambda bi, hi, qi: (bi, hi, qi)),
                  pl.BlockSpec((1, t, 2 * LANES), lambda bi, hi, qi: (bi, 0, hi)),
                  pl.BlockSpec((1, LANES, t), lambda bi, hi, qi: (bi, hi, 0)),
                  pl.BlockSpec((1, nb, LANES), lambda bi, hi, qi: (bi, 0, hi))],
        out_specs=pl.BlockSpec((1, MOBA_BLOCK, LANES), lambda bi, hi, qi: (bi, qi, hi)),
        out_shape=jax.ShapeDtypeStruct((b, t, MOBA_W), BF16),
        scratch_shapes=[pltpu.VMEM((2 * LANES, MOBA_BLOCK), BF16), pltpu.VMEM((LANES, MOBA_BLOCK), F32),
                        pltpu.VMEM((1, MOBA_BLOCK), F32), pltpu.VMEM((1, MOBA_BLOCK), F32),
                        pltpu.VMEM((group * MOBA_BLOCK, MOBA_BLOCK), F32),
                        pltpu.VMEM((group * MOBA_BLOCK, MOBA_BLOCK), F32)],
        compiler_params=_params(("parallel", "parallel", "arbitrary")),
        name="moba",
    )(qt, kaug, vt, kmean)


def _oproj_kernel(og_ref, om_ref, wa_ref, wb_ref, x_ref, nw_ref, h_ref, hn_ref):
    y = jnp.dot(og_ref[...], wa_ref[...], preferred_element_type=F32)
    y = y + jnp.dot(om_ref[...], wb_ref[...], preferred_element_type=F32)
    h = x_ref[...] + y
    h_ref[...] = h
    hn = h * lax.rsqrt(jnp.mean(h * h, axis=-1, keepdims=True) + RMS_EPS)
    hn_ref[...] = (hn * nw_ref[...]).astype(hn_ref.dtype)


def _oproj(og, om, wa, wb, x, norm_w, tm):
    m, d = x.shape
    ka, kb = og.shape[1], om.shape[1]
    row = lambda w: pl.BlockSpec((tm, w), lambda i: (i, 0))
    full = lambda r, c: pl.BlockSpec((r, c), lambda i: (0, 0))
    return pl.pallas_call(
        _oproj_kernel,
        grid=(m // tm,),
        in_specs=[row(ka), row(kb), full(ka, d), full(kb, d), row(d), full(1, d)],
        out_specs=[row(d), row(d)],
        out_shape=[jax.ShapeDtypeStruct((m, d), F32), jax.ShapeDtypeStruct((m, d), BF16)],
        compiler_params=_params(("parallel",)),
        name="oproj",
    )(og, om, wa, wb, x, norm_w.reshape(1, d))


def _ffn_kernel(hn_ref, wg_ref, wu_ref, wd_ref, h_ref, nw_ref, h2_ref, hn2_ref, acc_ref):
    f = pl.program_id(1)

    @pl.when(f == 0)
    def _():
        acc_ref[...] = jnp.zeros_like(acc_ref)

    hn = hn_ref[...]
    g = jnp.dot(hn, wg_ref[...], preferred_element_type=F32)
    u = jnp.dot(hn, wu_ref[...], preferred_element_type=F32)
    a = (g * _sigmoid(g) * u).astype(BF16)
    acc_ref[...] += jnp.dot(a, wd_ref[...], preferred_element_type=F32)

    @pl.when(f == pl.num_programs(1) - 1)
    def _():
        h2 = h_ref[...] + acc_ref[...]
        h2_ref[...] = h2
        n = h2 * lax.rsqrt(jnp.mean(h2 * h2, axis=-1, keepdims=True) + RMS_EPS)
        hn2_ref[...] = (n * nw_ref[...]).astype(hn2_ref.dtype)


def _ffn(hn, wg, wu, wd, h, norm_w, tm, tf):
    m, d = h.shape
    dff = wg.shape[1]
    row = pl.BlockSpec((tm, d), lambda i, f: (i, 0))
    return pl.pallas_call(
        _ffn_kernel,
        grid=(m // tm, dff // tf),
        in_specs=[row, pl.BlockSpec((d, tf), lambda i, f: (0, f)), pl.BlockSpec((d, tf), lambda i, f: (0, f)),
                  pl.BlockSpec((tf, d), lambda i, f: (f, 0)), row, pl.BlockSpec((1, d), lambda i, f: (0, 0))],
        out_specs=[row, row],
        out_shape=[jax.ShapeDtypeStruct((m, d), F32), jax.ShapeDtypeStruct((m, d), BF16)],
        scratch_shapes=[pltpu.VMEM((tm, d), F32)],
        compiler_params=_params(("parallel", "arbitrary")),
        name="ffn",
    )(hn, wg, wu, wd, h, norm_w.reshape(1, d))


def _ple_kernel(hn_ref, wg_ref, p_ref, wp_ref, h_ref, o_ref):
    gate = _sigmoid(jnp.dot(hn_ref[...], wg_ref[...], preferred_element_type=F32))
    proj = jnp.dot(p_ref[...].astype(BF16), wp_ref[...], preferred_element_type=F32)
    o_ref[...] = h_ref[...] + gate * proj


def _ple(hn, wg, p, wp, h, tm, tn):
    m, d = h.shape
    kp = p.shape[1]
    return pl.pallas_call(
        _ple_kernel,
        grid=(m // tm, d // tn),
        in_specs=[pl.BlockSpec((tm, d), lambda i, j: (i, 0)), pl.BlockSpec((d, tn), lambda i, j: (0, j)),
                  pl.BlockSpec((tm, kp), lambda i, j: (i, 0)), pl.BlockSpec((kp, tn), lambda i, j: (0, j)),
                  pl.BlockSpec((tm, tn), lambda i, j: (i, j))],
        out_specs=pl.BlockSpec((tm, tn), lambda i, j: (i, j)),
        out_shape=jax.ShapeDtypeStruct((m, d), F32),
        compiler_params=_params(("parallel", "arbitrary")),
        name="ple",
    )(hn, wg, p, wp, h)


def _layer(h, p, attn_norm, w_in, conv_w, a_log, dt_bias, gdn_norm, q_norm, k_norm, w_o, ffn_norm,
           w_gate, w_up, w_down, ple_norm, w_ple_gate, w_ple_proj):
    b, t, d = h.shape
    m = b * t
    x2 = h.reshape(m, d)
    tm = min(1024, m)

    o_ba = 4 * GDN_W
    o_mq = o_ba + 2 * GDN_HEADS
    w_in16 = w_in.astype(BF16)
    w_gdn = w_in16[:, :o_ba]
    w_ba = jnp.pad(w_in16[:, o_ba:o_mq], ((0, 0), (0, LANES - 2 * GDN_HEADS)))
    w_mq = w_in16[:, o_mq:o_mq + MOBA_W]
    w_mk = w_in16[:, o_mq + MOBA_W:o_mq + 2 * MOBA_W]
    w_mv = w_in16[:, o_mq + 2 * MOBA_W:o_mq + 3 * MOBA_W]

    xn = _rmsnorm(x2, attn_norm, min(512, m))
    gproj = _proj(xn, w_gdn, F32, tm, 512, "proj_gdn")
    ba = _proj(xn, w_ba, F32, tm, LANES, "proj_ba")
    mq = _proj_moba_q(xn, w_mq, q_norm, tm, 512)
    kaug, kmean = _proj_moba_k(xn, w_mk, k_norm, tm, 512, t // MOBA_BLOCK)
    mv = _proj(xn, w_mv, BF16, tm, 512, "proj_moba_v")

    o_gdn = _gdn(gproj.reshape(b, t, 4 * GDN_W), ba.reshape(b, t, LANES), conv_w.T, a_log, dt_bias,
                 gdn_norm, min(512, t))
    o_moba = _moba(mq.reshape(b, t, MOBA_W).swapaxes(1, 2), kaug.reshape(b, t, 2 * MOBA_W),
                   mv.reshape(b, t, MOBA_W).swapaxes(1, 2), kmean.reshape(b, t // MOBA_BLOCK, MOBA_W))

    w_o16 = w_o.astype(BF16)
    h1, hn = _oproj(o_gdn.reshape(m, GDN_W), o_moba.reshape(m, MOBA_W), w_o16[:GDN_W], w_o16[GDN_W:],
                    x2, ffn_norm, min(256, m))
    h2, hn2 = _ffn(hn, w_gate.astype(BF16), w_up.astype(BF16), w_down.astype(BF16), h1, ple_norm,
                   min(512, m), 512)
    h3 = _ple(hn2, w_ple_gate.astype(BF16), p.reshape(m, PLE_DIM), w_ple_proj.astype(BF16), h2, tm, 512)
    return h3.reshape(b, t, d)


def kernel(x, p, attn_norm, w_in, conv_w, A_log, dt_bias, gdn_norm, q_norm, k_norm, w_o, ffn_norm,
           w_gate, w_up, w_down, ple_norm, w_ple_gate, w_ple_proj):
    h = x
    for i in range(p.shape[0]):
        h = _layer(h, p[i], attn_norm[i], w_in[i], conv_w[i], A_log[i], dt_bias[i], gdn_norm[i],
                   q_norm[i], k_norm[i], w_o[i], ffn_norm[i], w_gate[i], w_up[i], w_down[i],
                   ple_norm[i], w_ple_gate[i], w_ple_proj[i])
    return h
```

```python
import functools

import jax
import jax.numpy as jnp
from jax import lax
from jax.experimental import pallas as pl
from jax.experimental.pallas import tpu as pltpu

D_MODEL = 2048
PLE_DIM = 256
GDN_HEADS = 8
GDN_DK = 128
GDN_DV = 128
GDN_CONV = 4
GDN_CHUNK = 64
MOBA_HEADS = 8
MOBA_DH = 128
MOBA_BLOCK = 256
MOBA_TOPK = 3
RMS_EPS = 1e-6
GDN_W = GDN_HEADS * GDN_DK
MOBA_W = MOBA_HEADS * MOBA_DH
LANES = 128
LOG2E = 1.4426950408889634
MASK_NEG = -1e30

F32 = jnp.float32
BF16 = jnp.bfloat16
HI = lax.Precision.HIGHEST

VMEM_LIMIT = 56 * 1024 * 1024


def _params(sem):
    return pltpu.CompilerParams(dimension_semantics=sem, vmem_limit_bytes=VMEM_LIMIT)


def _sigmoid(x):
    return 1.0 / (1.0 + jnp.exp(-x))


def _dot_t(a, b, precision=None):
    return lax.dot_general(a, b, (((1,), (1,)), ((), ())), precision=precision,
                           preferred_element_type=F32)


def _rmsnorm_kernel(x_ref, w_ref, o_ref):
    x = x_ref[...]
    y = x * lax.rsqrt(jnp.mean(x * x, axis=-1, keepdims=True) + RMS_EPS)
    o_ref[...] = (y * w_ref[...]).astype(o_ref.dtype)


def _rmsnorm(x, w, tm):
    m, d = x.shape
    return pl.pallas_call(
        _rmsnorm_kernel,
        grid=(m // tm,),
        in_specs=[pl.BlockSpec((tm, d), lambda i: (i, 0)), pl.BlockSpec((1, d), lambda i: (0, 0))],
        out_specs=pl.BlockSpec((tm, d), lambda i: (i, 0)),
        out_shape=jax.ShapeDtypeStruct((m, d), BF16),
        compiler_params=_params(("parallel",)),
        name="rmsnorm_cast",
    )(x, w.reshape(1, d))


def _proj_kernel(x_ref, w_ref, o_ref):
    o_ref[...] = jnp.dot(x_ref[...], w_ref[...], preferred_element_type=F32).astype(o_ref.dtype)


def _proj(x, w, out_dtype, tm, tn, name):
    m, k = x.shape
    n = w.shape[1]
    return pl.pallas_call(
        _proj_kernel,
        grid=(m // tm, n // tn),
        in_specs=[pl.BlockSpec((tm, k), lambda i, j: (i, 0)), pl.BlockSpec((k, tn), lambda i, j: (0, j))],
        out_specs=pl.BlockSpec((tm, tn), lambda i, j: (i, j)),
        out_shape=jax.ShapeDtypeStruct((m, n), out_dtype),
        compiler_params=_params(("parallel", "arbitrary")),
        name=name,
    )(x, w)


def _head_rmsnorm(y, gain, scale):
    outs = []
    for h in range(y.shape[1] // LANES):
        yh = y[:, h * LANES:(h + 1) * LANES]
        r = lax.rsqrt(jnp.mean(yh * yh, axis=-1, keepdims=True) + RMS_EPS)
        outs.append(yh * r * gain[:, h * LANES:(h + 1) * LANES] * scale)
    return jnp.concatenate(outs, axis=1)


def _proj_qnorm_kernel(x_ref, w_ref, g_ref, o_ref, *, scale):
    y = jnp.dot(x_ref[...], w_ref[...], preferred_element_type=F32)
    o_ref[...] = _head_rmsnorm(y, g_ref[...], scale).astype(o_ref.dtype)


def _proj_knorm_kernel(x_ref, w_ref, g_ref, o_ref, km_ref, *, nb_seq):
    y = jnp.dot(x_ref[...], w_ref[...], preferred_element_type=F32)
    yn = _head_rmsnorm(y, g_ref[...], 1.0)
    tm = yn.shape[0]
    row = lax.broadcasted_iota(jnp.int32, (tm, LANES), 0) + pl.program_id(0) * tm
    lane = lax.broadcasted_iota(jnp.int32, (tm, LANES), 1)
    onehot = jnp.where(lane == lax.rem(row // MOBA_BLOCK, nb_seq), 1.0, 0.0).astype(o_ref.dtype)
    yb = yn.astype(o_ref.dtype)
    parts = []
    for h in range(yn.shape[1] // LANES):
        parts += [yb[:, h * LANES:(h + 1) * LANES], onehot]
    o_ref[...] = jnp.concatenate(parts, axis=1)
    for r in range(tm // MOBA_BLOCK):
        blk = yn[r * MOBA_BLOCK:(r + 1) * MOBA_BLOCK]
        km_ref[r] = jnp.mean(blk, axis=0, keepdims=True)


def _proj_moba_q(x, w, gain, tm, tn):
    m, k = x.shape
    n = w.shape[1]
    g = jnp.tile(gain.reshape(1, MOBA_DH), (1, n // MOBA_DH))
    return pl.pallas_call(
        functools.partial(_proj_qnorm_kernel, scale=MOBA_DH ** -0.5 * LOG2E),
        grid=(m // tm, n // tn),
        in_specs=[pl.BlockSpec((tm, k), lambda i, j: (i, 0)), pl.BlockSpec((k, tn), lambda i, j: (0, j)),
                  pl.BlockSpec((1, tn), lambda i, j: (0, j))],
        out_specs=pl.BlockSpec((tm, tn), lambda i, j: (i, j)),
        out_shape=jax.ShapeDtypeStruct((m, n), BF16),
        compiler_params=_params(("parallel", "arbitrary")),
        name="proj_moba_q",
    )(x, w, g)


def _proj_moba_k(x, w, gain, tm, tn, nb_seq):
    m, k = x.shape
    n = w.shape[1]
    g = jnp.tile(gain.reshape(1, MOBA_DH), (1, n // MOBA_DH))
    nb = tm // MOBA_BLOCK
    return pl.pallas_call(
        functools.partial(_proj_knorm_kernel, nb_seq=nb_seq),
        grid=(m // tm, n // tn),
        in_specs=[pl.BlockSpec((tm, k), lambda i, j: (i, 0)), pl.BlockSpec((k, tn), lambda i, j: (0, j)),
                  pl.BlockSpec((1, tn), lambda i, j: (0, j))],
        out_specs=[pl.BlockSpec((tm, 2 * tn), lambda i, j: (i, j)),
                   pl.BlockSpec((nb, 1, tn), lambda i, j: (i, 0, j))],
        out_shape=[jax.ShapeDtypeStruct((m, 2 * n), BF16),
                   jax.ShapeDtypeStruct((m // MOBA_BLOCK, 1, n), F32)],
        compiler_params=_params(("parallel", "arbitrary")),
        name="proj_moba_k",
    )(x, w, g)


def _gdn_kernel(alog_ref, dtb_ref, xq_ref, xk_ref, xv_ref, z_ref, ba_ref, wq_ref, wk_ref, wv_ref,
                gn_ref, o_ref, s_ref, tq_ref, tk_ref, tv_ref, q_s, k_s, v_s, g_s, b_s, *, tb):
    h = pl.program_id(1)
    t = pl.program_id(2)
    C = GDN_CHUNK

    @pl.when(t == 0)
    def _():
        s_ref[...] = jnp.zeros_like(s_ref)
        tq_ref[...] = jnp.zeros_like(tq_ref)
        tk_ref[...] = jnp.zeros_like(tk_ref)
        tv_ref[...] = jnp.zeros_like(tv_ref)

    def conv_silu(x_ref, tail_ref, w_ref):
        x = x_ref[0]
        xp = jnp.concatenate([tail_ref[...], x], axis=0)
        w = w_ref[...]
        y = x * w[GDN_CONV - 1:GDN_CONV]
        for j in range(GDN_CONV - 1):
            shifted = pltpu.roll(xp, GDN_CONV - 1 - j, axis=0)[8:8 + tb]
            y = y + shifted * w[j:j + 1]
        tail_ref[...] = x[tb - 8:tb]
        return y * _sigmoid(y)

    q = conv_silu(xq_ref, tq_ref, wq_ref)
    k = conv_silu(xk_ref, tk_ref, wk_ref)
    v_s[...] = conv_silu(xv_ref, tv_ref, wv_ref)
    q_s[...] = q * lax.rsqrt(jnp.sum(q * q, axis=-1, keepdims=True) + 1e-6) * (GDN_DK ** -0.5)
    k_s[...] = k * lax.rsqrt(jnp.sum(k * k, axis=-1, keepdims=True) + 1e-6)

    ba = ba_ref[0]
    lane = lax.broadcasted_iota(jnp.int32, ba.shape, 1)
    bcol = jnp.sum(jnp.where(lane == h, ba, 0.0), axis=1, keepdims=True)
    acol = jnp.sum(jnp.where(lane == h + GDN_HEADS, ba, 0.0), axis=1, keepdims=True)
    ones_row = jnp.ones((1, LANES), F32)
    a_exp = jnp.exp(ones_row * alog_ref[h])
    sp_in = acol + dtb_ref[h]
    softplus = jnp.maximum(sp_in, 0.0) + jnp.log1p(jnp.exp(-jnp.abs(sp_in)))
    g_s[...] = -(softplus * a_exp)
    b_s[...] = _sigmoid(bcol) * ones_row

    ri = lax.broadcasted_iota(jnp.int32, (C, C), 0)
    ci = lax.broadcasted_iota(jnp.int32, (C, C), 1)
    tril = ri >= ci
    strict = ri > ci
    eye = ri == ci
    tril_f = tril.astype(F32)
    ones_cc = jnp.ones((C, C), F32)
    eye_f = eye.astype(F32)
    gn = gn_ref[...]

    def chunk(c, carry):
        sl = pl.ds(pl.multiple_of(c * C, C), C)
        qc, kc, vc = q_s[sl, :], k_s[sl, :], v_s[sl, :]
        beta = b_s[sl, :]
        gcb = jnp.dot(tril_f, g_s[sl, :], precision=HI, preferred_element_type=F32)
        gcc = gcb[:, :C]
        grow = jnp.dot(ones_cc, jnp.where(eye, gcc, 0.0), precision=HI, preferred_element_type=F32)
        decay = jnp.exp(jnp.where(tril, gcc - grow, -jnp.inf))
        k_beta = kc * beta
        v_beta = vc * beta
        a_mat = jnp.where(strict, _dot_t(k_beta, kc, HI) * decay, 0.0)
        t_inv = eye_f - jnp.where((ri >> 1 == ci >> 1), a_mat, 0.0)
        s = 2
        while s < C:
            sh = s.bit_length()
            off = (ri >> sh == ci >> sh) & ((ri & (2 * s - 1)) >= s) & ((ci & (2 * s - 1)) < s)
            a_off = jnp.where(off, a_mat, 0.0)
            t_inv = t_inv - jnp.dot(t_inv, jnp.dot(a_off, t_inv, precision=HI, preferred_element_type=F32),
                                    precision=HI, preferred_element_type=F32)
            s *= 2
        egc = jnp.exp(gcb)
        u = jnp.dot(t_inv, v_beta, precision=HI, preferred_element_type=F32)
        w = jnp.dot(t_inv, k_beta * egc, precision=HI, preferred_element_type=F32)
        qk = jnp.where(tril, _dot_t(qc, kc, HI) * decay, 0.0)
        gl = gcb[C - 1:C, :]
        egl = jnp.exp(gl)
        k_dec = kc * jnp.exp(gl - gcb)
        q_dec = qc * egc
        s_mat = s_ref[...]
        v_new = u - jnp.dot(w, s_mat, precision=HI, preferred_element_type=F32)
        o = (jnp.dot(q_dec, s_mat, precision=HI, preferred_element_type=F32)
             + jnp.dot(qk, v_new, precision=HI, preferred_element_type=F32))
        s_ref[...] = s_mat * egl + lax.dot_general(k_dec, v_new, (((0,), (0,)), ((), ())), precision=HI,
                                                   preferred_element_type=F32)
        on = o * lax.rsqrt(jnp.mean(o * o, axis=-1, keepdims=True) + RMS_EPS) * gn
        zc = z_ref[0, sl, :]
        o_ref[0, sl, :] = (on * (zc * _sigmoid(zc))).astype(o_ref.dtype)
        return carry

    lax.fori_loop(0, tb // C, chunk, 0)


def _gdn(proj, ba, conv_wt, a_log, dt_bias, gnorm, tb):
    b, t, _ = proj.shape
    nh = GDN_HEADS
    blk = lambda off: pl.BlockSpec((1, tb, LANES), lambda bi, hi, ti: (bi, ti, off + hi))
    wblk = lambda off: pl.BlockSpec((GDN_CONV, LANES), lambda bi, hi, ti: (0, off + hi))
    smem = pl.BlockSpec(memory_space=pltpu.SMEM)
    return pl.pallas_call(
        functools.partial(_gdn_kernel, tb=tb),
        grid=(b, nh, t // tb),
        in_specs=[smem, smem, blk(0), blk(nh), blk(2 * nh), blk(3 * nh),
                  pl.BlockSpec((1, tb, LANES), lambda bi, hi, ti: (bi, ti, 0)),
                  wblk(0), wblk(nh), wblk(2 * nh),
                  pl.BlockSpec((1, LANES), lambda bi, hi, ti: (0, 0))],
        out_specs=pl.BlockSpec((1, tb, LANES), lambda bi, hi, ti: (bi, ti, hi)),
        out_shape=jax.ShapeDtypeStruct((b, t, GDN_W), BF16),
        scratch_shapes=[pltpu.VMEM((GDN_DK, GDN_DV), F32)] + [pltpu.VMEM((8, LANES), F32)] * 3
        + [pltpu.VMEM((tb, LANES), F32)] * 5,
        compiler_params=_params(("parallel", "parallel", "arbitrary")),
        name="gdn",
    )(a_log, dt_bias, proj, proj, proj, proj, ba, conv_wt, conv_wt, conv_wt, gnorm.reshape(1, GDN_DV))


def _bdot(a, b):
    return jnp.dot(a.astype(BF16), b.astype(BF16), preferred_element_type=F32)


def _gdn_all_kernel(x_ref, ba_ref, cw_ref, alog_ref, dtb_ref, gn_ref, o_ref, s_ref, tail_ref, qkv_s, *, tb):
    t = pl.program_id(1)
    C = GDN_CHUNK
    nc = tb // C
    nh = GDN_HEADS

    @pl.when(t == 0)
    def _():
        s_ref[...] = jnp.zeros_like(s_ref)
        tail_ref[...] = jnp.zeros_like(tail_ref)

    def conv_group(gi, carry):
        off = pl.multiple_of(gi * LANES, LANES)
        x = x_ref[0, :, pl.ds(off, LANES)]
        xp = jnp.concatenate([tail_ref[:, pl.ds(off, LANES)], x], axis=0)
        w = cw_ref[:, pl.ds(off, LANES)]
        y = x * w[GDN_CONV - 1:GDN_CONV]
        for j in range(GDN_CONV - 1):
            y = y + pltpu.roll(xp, GDN_CONV - 1 - j, axis=0)[8:8 + tb] * w[j:j + 1]
        tail_ref[:, pl.ds(off, LANES)] = x[tb - 8:tb]
        y = y * _sigmoid(y)
        r = lax.rsqrt(jnp.sum(y * y, axis=-1, keepdims=True) + 1e-6)
        fac = jnp.where(gi < nh, r * (GDN_DK ** -0.5), jnp.where(gi < 2 * nh, r, jnp.ones_like(r)))
        qkv_s[:, pl.ds(off, LANES)] = y * fac
        return carry

    lax.fori_loop(0, 3 * nh, conv_group, 0)

    ba = ba_ref[0]
    beta_all = _sigmoid(ba)
    sp_in = ba + dtb_ref[...]
    softplus = jnp.maximum(sp_in, 0.0) + jnp.log1p(jnp.exp(-jnp.abs(sp_in)))
    gc = -(jnp.exp(alog_ref[...]) * softplus)
    row_in_chunk = lax.broadcasted_iota(jnp.int32, (tb, LANES), 0) & (C - 1)
    sh = 1
    while sh < C:
        gc = gc + jnp.where(row_in_chunk >= sh, pltpu.roll(gc, sh, axis=0), 0.0)
        sh *= 2
    gl_rows = [gc[(c + 1) * C - 1:(c + 1) * C] for c in range(nc)]
    gl_b = jnp.concatenate([jnp.broadcast_to(g, (C, LANES)) for g in gl_rows], axis=0)
    egc = jnp.exp(gc)
    kdsc = jnp.exp(gl_b - gc)
    egl = [jnp.exp(g) for g in gl_rows]
    gc_t = gc.T

    ri = lax.broadcasted_iota(jnp.int32, (C, C), 0)
    ci = lax.broadcasted_iota(jnp.int32, (C, C), 1)
    tril = ri >= ci
    strict = ri > ci
    eye_f = (ri == ci).astype(F32)
    lvl_masks = []
    s = 1
    while s < C:
        sh2 = s.bit_length()
        lvl_masks.append((ri >> sh2 == ci >> sh2) & ((ri & (2 * s - 1)) >= s) & ((ci & (2 * s - 1)) < s))
        s *= 2
    gn = gn_ref[...]

    states = [s_ref[h] for h in range(nh)]
    for c in range(nc):
        r0 = c * C
        heads = range(nh)
        rows = slice(r0, r0 + C)
        grp = lambda g, h: slice((g * nh + h) * LANES, (g * nh + h + 1) * LANES)
        qc = [qkv_s[rows, grp(0, h)] for h in heads]
        kc = [qkv_s[rows, grp(1, h)] for h in heads]
        vc = [qkv_s[rows, grp(2, h)] for h in heads]
        bcol = [beta_all[rows, h:h + 1] for h in heads]
        ecol = [egc[rows, nh + h:nh + h + 1] for h in heads]
        decay = [jnp.exp(jnp.where(tril, gc[rows, nh + h:nh + h + 1] - gc_t[nh + h:nh + h + 1, rows], -jnp.inf))
                 for h in heads]
        k_beta = [kc[h] * bcol[h] for h in heads]
        st = [_dot_t(jnp.concatenate([k_beta[h], qc[h]], axis=0).astype(BF16), kc[h].astype(BF16))
              for h in heads]
        a_mat = [jnp.where(strict, st[h][:C] * decay[h], 0.0) for h in heads]
        qk = [jnp.where(tril, st[h][C:] * decay[h], 0.0) for h in heads]
        t_inv = [eye_f - jnp.where(lvl_masks[0], a_mat[h], 0.0) for h in heads]
        for msk in lvl_masks[1:]:
            x_mid = [_bdot(jnp.where(msk, a_mat[h], 0.0), t_inv[h]) for h in heads]
            y_mid = [_bdot(t_inv[h], x_mid[h]) for h in heads]
            t_inv = [t_inv[h] - y_mid[h] for h in heads]
        uw = [_bdot(t_inv[h], jnp.concatenate([vc[h] * bcol[h], k_beta[h] * ecol[h]], axis=1))
              for h in heads]
        wq = [_bdot(jnp.concatenate([uw[h][:, LANES:], qc[h] * ecol[h]], axis=0), states[h])
              for h in heads]
        v_new = [uw[h][:, :LANES] - wq[h][:C] for h in heads]
        o = [wq[h][C:] + _bdot(qk[h], v_new[h]) for h in heads]
        k_dec = [kc[h] * kdsc[rows, nh + h:nh + h + 1] for h in heads]
        states = [states[h] * egl[c][:, nh + h:nh + h + 1] + lax.dot_general(
            k_dec[h].astype(BF16), v_new[h].astype(BF16), (((0,), (0,)), ((), ())),
            preferred_element_type=F32) for h in heads]
        for h in heads:
            on = o[h] * lax.rsqrt(jnp.mean(o[h] * o[h], axis=-1, keepdims=True) + RMS_EPS) * gn
            zc = x_ref[0, rows, grp(3, h)]
            o_ref[0, rows, grp(0, h)] = (on * (zc * _sigmoid(zc))).astype(o_ref.dtype)
    for h in range(nh):
        s_ref[h] = states[h]


def _gdn_all(proj, ba, conv_wt, a_log, dt_bias, gnorm, tb):
    b, t, width = proj.shape
    nh = GDN_HEADS
    lane_vec = lambda v: jnp.pad(v.reshape(1, nh), ((0, 0), (nh, LANES - 2 * nh)))
    full = lambda shape: pl.BlockSpec(shape, lambda bi, ti: (0,) * len(shape))
    return pl.pallas_call(
        functools.partial(_gdn_all_kernel, tb=tb),
        grid=(b, t // tb),
        in_specs=[pl.BlockSpec((1, tb, width), lambda bi, ti: (bi, ti, 0)),
                  pl.BlockSpec((1, tb, LANES), lambda bi, ti: (bi, ti, 0)),
                  full((GDN_CONV, 3 * GDN_W)), full((1, LANES)), full((1, LANES)), full((1, GDN_DV))],
        out_specs=pl.BlockSpec((1, tb, GDN_W), lambda bi, ti: (bi, ti, 0)),
        out_shape=jax.ShapeDtypeStruct((b, t, GDN_W), BF16),
        scratch_shapes=[pltpu.VMEM((nh, GDN_DK, GDN_DV), F32), pltpu.VMEM((8, 3 * GDN_W), F32),
                        pltpu.VMEM((tb, 3 * GDN_W), F32)],
        compiler_params=_params(("parallel", "arbitrary")),
        name="gdn",
    )(proj, ba, conv_wt, lane_vec(a_log), lane_vec(dt_bias), gnorm.reshape(1, GDN_DV))


def _moba_kernel(qt_ref, ka_ref, vt_ref, km_ref, o_ref, qa_ref, acc_ref, m_ref, l_ref, sa_ref, sb_ref, *,
                 nb, group):
    i = pl.program_id(2)
    BLK = MOBA_BLOCK
    nbp = -(-nb // 16) * 16
    qt = qt_ref[0]
    km = km_ref[0].astype(BF16)
    if nbp > nb:
        km = jnp.concatenate([km, jnp.zeros((nbp - nb, LANES), BF16)], axis=0)
    gate = jnp.dot(km, qt, preferred_element_type=F32)
    row = lax.broadcasted_iota(jnp.int32, (nbp, BLK), 0)
    past = row < i
    g = jnp.where(past, gate, -jnp.inf)
    sel = row < 0
    for _ in range(MOBA_TOPK):
        m = jnp.max(g, axis=0, keepdims=True)
        idx = jnp.min(jnp.where(g == m, row, nbp), axis=0, keepdims=True)
        hit = row == idx
        sel = sel | (hit & past)
        g = jnp.where(hit, -jnp.inf, g)
    bias = jnp.where(sel, 0.0, MASK_NEG).astype(BF16)
    qa_ref[0:LANES, :] = qt
    qa_ref[LANES:LANES + nbp, :] = bias
    if nbp < LANES:
        qa_ref[LANES + nbp:, :] = jnp.zeros((LANES - nbp, BLK), BF16)

    def pv(p, start, width):
        return jnp.dot(vt_ref[0, :, pl.ds(start, width)], p.astype(BF16), preferred_element_type=F32)

    start0 = pl.multiple_of(i * BLK, BLK)
    s0 = jnp.dot(ka_ref[0, pl.ds(start0, BLK), 0:LANES], qt, preferred_element_type=F32)
    key_i = lax.broadcasted_iota(jnp.int32, (BLK, BLK), 0)
    qry_i = lax.broadcasted_iota(jnp.int32, (BLK, BLK), 1)
    s0 = jnp.where(key_i <= qry_i, s0, MASK_NEG)
    m0 = jnp.max(s0, axis=0, keepdims=True)
    p0 = jnp.exp2(s0 - m0)
    m_ref[...] = m0
    l_ref[...] = jnp.sum(p0, axis=0, keepdims=True)
    acc_ref[...] = pv(p0, start0, BLK)

    span = group * BLK
    n_pairs = (i + 2 * group - 1) // (2 * group)
    last = nb // group - 1

    def qk(g_idx, dst_ref):
        start = pl.multiple_of(g_idx * span, span)
        dst_ref[...] = jnp.dot(ka_ref[0, pl.ds(start, span), :], qa_ref[...], preferred_element_type=F32)

    def update(src_ref, g_idx):
        s = src_ref[...]
        m_old = m_ref[...]
        m_new = jnp.maximum(m_old, jnp.max(s, axis=0, keepdims=True))
        alpha = jnp.exp2(m_old - m_new)
        p = jnp.exp2(s - m_new)
        l_ref[...] = alpha * l_ref[...] + jnp.sum(p, axis=0, keepdims=True)
        m_ref[...] = m_new
        acc_ref[...] = alpha * acc_ref[...] + pv(p, pl.multiple_of(g_idx * span, span), span)

    qk(0, sa_ref)

    def body(jj, carry):
        qk(2 * jj + 1, sb_ref)
        update(sa_ref, 2 * jj)
        qk(jnp.minimum(2 * jj + 2, last), sa_ref)
        update(sb_ref, 2 * jj + 1)
        return carry

    lax.fori_loop(0, n_pairs, body, 0)
    o = acc_ref[...] * (1.0 / l_ref[...])
    o_ref[0] = o.T.astype(o_ref.dtype)


def _moba(qt, kaug, vt, kmean):
    b, _, t = qt.shape
    nb = t // MOBA_BLOCK
    group = 2
    assert nb <= LANES and nb % (2 * group) == 0
    return pl.pallas_call(
        functools.partial(_moba_kernel, nb=nb, group=group),
        grid=(b, MOBA_HEADS, nb),
        in_specs=[pl.BlockSpec((1, LANES, MOBA_BLOCK), lambda bi, hi, qi: (bi, hi, qi)),
                  pl.BlockSpec((1, t, 2 * LANES), lambda bi, hi, qi: (bi, 0, hi)),
                  pl.BlockSpec((1, LANES, t), lambda bi, hi, qi: (bi, hi, 0)),
                  pl.BlockSpec((1, nb, LANES), lambda bi, hi, qi: (bi, 0, hi))],
        out_specs=pl.BlockSpec((1, MOBA_BLOCK, LANES), lambda bi, hi, qi: (bi, qi, hi)),
        out_shape=jax.ShapeDtypeStruct((b, t, MOBA_W), BF16),
        scratch_shapes=[pltpu.VMEM((2 * LANES, MOBA_BLOCK), BF16), pltpu.VMEM((LANES, MOBA_BLOCK), F32),
                        pltpu.VMEM((1, MOBA_BLOCK), F32), pltpu.VMEM((1, MOBA_BLOCK), F32),
                        pltpu.VMEM((group * MOBA_BLOCK, MOBA_BLOCK), F32),
                        pltpu.VMEM((group * MOBA_BLOCK, MOBA_BLOCK), F32)],
        compiler_params=_params(("parallel", "parallel", "arbitrary")),
        name="moba",
    )(qt, kaug, vt, kmean)


def _oproj_kernel(og_ref, om_ref, wa_ref, wb_ref, x_ref, nw_ref, h_ref, hn_ref):
    y = jnp.dot(og_ref[...], wa_ref[...], preferred_element_type=F32)
    y = y + jnp.dot(om_ref[...], wb_ref[...], preferred_element_type=F32)
    h = x_ref[...] + y
    h_ref[...] = h
    hn = h * lax.rsqrt(jnp.mean(h * h, axis=-1, keepdims=True) + RMS_EPS)
    hn_ref[...] = (hn * nw_ref[...]).astype(hn_ref.dtype)


def _oproj(og, om, wa, wb, x, norm_w, tm):
    m, d = x.shape
    ka, kb = og.shape[1], om.shape[1]
    row = lambda w: pl.BlockSpec((tm, w), lambda i: (i, 0))
    full = lambda r, c: pl.BlockSpec((r, c), lambda i: (0, 0))
    return pl.pallas_call(
        _oproj_kernel,
        grid=(m // tm,),
        in_specs=[row(ka), row(kb), full(ka, d), full(kb, d), row(d), full(1, d)],
        out_specs=[row(d), row(d)],
        out_shape=[jax.ShapeDtypeStruct((m, d), F32), jax.ShapeDtypeStruct((m, d), BF16)],
        compiler_params=_params(("parallel",)),
        name="oproj",
    )(og, om, wa, wb, x, norm_w.reshape(1, d))


def _ffn_kernel(hn_ref, wg_ref, wu_ref, wd_ref, h_ref, nw_ref, h2_ref, hn2_ref, acc_ref):
    f = pl.program_id(1)

    @pl.when(f == 0)
    def _():
        acc_ref[...] = jnp.zeros_like(acc_ref)

    hn = hn_ref[...]
    g = jnp.dot(hn, wg_ref[...], preferred_element_type=F32)
    u = jnp.dot(hn, wu_ref[...], preferred_element_type=F32)
    a = (g * _sigmoid(g) * u).astype(BF16)
    acc_ref[...] += jnp.dot(a, wd_ref[...], preferred_element_type=F32)

    @pl.when(f == pl.num_programs(1) - 1)
    def _():
        h2 = h_ref[...] + acc_ref[...]
        h2_ref[...] = h2
        n = h2 * lax.rsqrt(jnp.mean(h2 * h2, axis=-1, keepdims=True) + RMS_EPS)
        hn2_ref[...] = (n * nw_ref[...]).astype(hn2_ref.dtype)


def _ffn(hn, wg, wu, wd, h, norm_w, tm, tf):
    m, d = h.shape
    dff = wg.shape[1]
    row = pl.BlockSpec((tm, d), lambda i, f: (i, 0))
    return pl.pallas_call(
        _ffn_kernel,
        grid=(m // tm, dff // tf),
        in_specs=[row, pl.BlockSpec((d, tf), lambda i, f: (0, f)), pl.BlockSpec((d, tf), lambda i, f: (0, f)),
                  pl.BlockSpec((tf, d), lambda i, f: (f, 0)), row, pl.BlockSpec((1, d), lambda i, f: (0, 0))],
        out_specs=[row, row],
        out_shape=[jax.ShapeDtypeStruct((m, d), F32), jax.ShapeDtypeStruct((m, d), BF16)],
        scratch_shapes=[pltpu.VMEM((tm, d), F32)],
        compiler_params=_params(("parallel", "arbitrary")),
        name="ffn",
    )(hn, wg, wu, wd, h, norm_w.reshape(1, d))


def _ple_kernel(hn_ref, wg_ref, p_ref, wp_ref, h_ref, o_ref):
    gate = _sigmoid(jnp.dot(hn_ref[...], wg_ref[...], preferred_element_type=F32))
    proj = jnp.dot(p_ref[...].astype(BF16), wp_ref[...], preferred_element_type=F32)
    o_ref[...] = h_ref[...] + gate * proj


def _ple(hn, wg, p, wp, h, tm, tn):
    m, d = h.shape
    kp = p.shape[1]
    return pl.pallas_call(
        _ple_kernel,
        grid=(m // tm, d // tn),
        in_specs=[pl.BlockSpec((tm, d), lambda i, j: (i, 0)), pl.BlockSpec((d, tn), lambda i, j: (0, j)),
                  pl.BlockSpec((tm, kp), lambda i, j: (i, 0)), pl.BlockSpec((kp, tn), lambda i, j: (0, j)),
                  pl.BlockSpec((tm, tn), lambda i, j: (i, j))],
        out_specs=pl.BlockSpec((tm, tn), lambda i, j: (i, j)),
        out_shape=jax.ShapeDtypeStruct((m, d), F32),
        compiler_params=_params(("parallel", "arbitrary")),
        name="ple",
    )(hn, wg, p, wp, h)


def _layer(h, p, attn_norm, w_in, conv_w, a_log, dt_bias, gdn_norm, q_norm, k_norm, w_o, ffn_norm,
           w_gate, w_up, w_down, ple_norm, w_ple_gate, w_ple_proj):
    b, t, d = h.shape
    m = b * t
    x2 = h.reshape(m, d)
    tm = min(1024, m)

    o_ba = 4 * GDN_W
    o_mq = o_ba + 2 * GDN_HEADS
    w_in16 = w_in.astype(BF16)
    w_gdn = w_in16[:, :o_ba]
    w_ba = jnp.pad(w_in16[:, o_ba:o_mq], ((0, 0), (0, LANES - 2 * GDN_HEADS)))
    w_mq = w_in16[:, o_mq:o_mq + MOBA_W]
    w_mk = w_in16[:, o_mq + MOBA_W:o_mq + 2 * MOBA_W]
    w_mv = w_in16[:, o_mq + 2 * MOBA_W:o_mq + 3 * MOBA_W]

    xn = _rmsnorm(x2, attn_norm, min(512, m))
    gproj = _proj(xn, w_gdn, F32, tm, 512, "proj_gdn")
    ba = _proj(xn, w_ba, F32, tm, LANES, "proj_ba")
    mq = _proj_moba_q(xn, w_mq, q_norm, tm, 512)
    kaug, kmean = _proj_moba_k(xn, w_mk, k_norm, tm, 512, t // MOBA_BLOCK)
    mv = _proj(xn, w_mv, BF16, tm, 512, "proj_moba_v")

    o_gdn = _gdn_all(gproj.reshape(b, t, 4 * GDN_W), ba.reshape(b, t, LANES), conv_w.T, a_log, dt_bias,
                     gdn_norm, min(256, t))
    o_moba = _moba(mq.reshape(b, t, MOBA_W).swapaxes(1, 2), kaug.reshape(b, t, 2 * MOBA_W),
                   mv.reshape(b, t, MOBA_W).swapaxes(1, 2), kmean.reshape(b, t // MOBA_BLOCK, MOBA_W))

    w_o16 = w_o.astype(BF16)
    h1, hn = _oproj(o_gdn.reshape(m, GDN_W), o_moba.reshape(m, MOBA_W), w_o16[:GDN_W], w_o16[GDN_W:],
                    x2, ffn_norm, min(256, m))
    h2, hn2 = _ffn(hn, w_gate.astype(BF16), w_up.astype(BF16), w_down.astype(BF16), h1, ple_norm,
                   min(512, m), 512)
    h3 = _ple(hn2, w_ple_gate.astype(BF16), p.reshape(m, PLE_DIM), w_ple_proj.astype(BF16), h2, tm, 512)
    return h3.reshape(b, t, d)


def kernel(x, p, attn_norm, w_in, conv_w, A_log, dt_bias, gdn_norm, q_norm, k_norm, w_o, ffn_norm,
           w_gate, w_up, w_down, ple_norm, w_ple_gate, w_ple_proj):
    h = x
    for i in range(p.shape[0]):
        h = _layer(h, p[i], attn_norm[i], w_in[i], conv_w[i], A_log[i], dt_bias[i], gdn_norm[i],
                   q_norm[i], k_norm[i], w_o[i], ffn_norm[i], w_gate[i], w_up[i], w_down[i],
                   ple_norm[i], w_ple_gate[i], w_ple_proj[i])
    return h
```

```python
import functools

import jax
import jax.numpy as jnp
from jax import lax
from jax.experimental import pallas as pl
from jax.experimental.pallas import tpu as pltpu

D_MODEL = 2048
PLE_DIM = 256
GDN_HEADS = 8
GDN_DK = 128
GDN_DV = 128
GDN_CONV = 4
GDN_CHUNK = 64
MOBA_HEADS = 8
MOBA_DH = 128
MOBA_BLOCK = 256
MOBA_TOPK = 3
RMS_EPS = 1e-6
GDN_W = GDN_HEADS * GDN_DK
MOBA_W = MOBA_HEADS * MOBA_DH
LANES = 128
LOG2E = 1.4426950408889634
MASK_NEG = -1e30

F32 = jnp.float32
BF16 = jnp.bfloat16
HI = lax.Precision.HIGHEST

VMEM_LIMIT = 56 * 1024 * 1024


def _params(sem):
    return pltpu.CompilerParams(dimension_semantics=sem, vmem_limit_bytes=VMEM_LIMIT)


def _sigmoid(x):
    return 1.0 / (1.0 + jnp.exp(-x))


def _dot_t(a, b, precision=None):
    return lax.dot_general(a, b, (((1,), (1,)), ((), ())), precision=precision,
                           preferred_element_type=F32)


def _rmsnorm_kernel(x_ref, w_ref, o_ref):
    x = x_ref[...]
    y = x * lax.rsqrt(jnp.mean(x * x, axis=-1, keepdims=True) + RMS_EPS)
    o_ref[...] = (y * w_ref[...]).astype(o_ref.dtype)


def _rmsnorm(x, w, tm):
    m, d = x.shape
    return pl.pallas_call(
        _rmsnorm_kernel,
        grid=(m // tm,),
        in_specs=[pl.BlockSpec((tm, d), lambda i: (i, 0)), pl.BlockSpec((1, d), lambda i: (0, 0))],
        out_specs=pl.BlockSpec((tm, d), lambda i: (i, 0)),
        out_shape=jax.ShapeDtypeStruct((m, d), BF16),
        compiler_params=_params(("parallel",)),
        name="rmsnorm_cast",
    )(x, w.reshape(1, d))


def _proj_kernel(x_ref, w_ref, o_ref):
    o_ref[...] = jnp.dot(x_ref[...], w_ref[...], preferred_element_type=F32).astype(o_ref.dtype)


def _proj(x, w, out_dtype, tm, tn, name):
    m, k = x.shape
    n = w.shape[1]
    return pl.pallas_call(
        _proj_kernel,
        grid=(m // tm, n // tn),
        in_specs=[pl.BlockSpec((tm, k), lambda i, j: (i, 0)), pl.BlockSpec((k, tn), lambda i, j: (0, j))],
        out_specs=pl.BlockSpec((tm, tn), lambda i, j: (i, j)),
        out_shape=jax.ShapeDtypeStruct((m, n), out_dtype),
        compiler_params=_params(("parallel", "arbitrary")),
        name=name,
    )(x, w)


def _head_rmsnorm(y, gain, scale):
    outs = []
    for h in range(y.shape[1] // LANES):
        yh = y[:, h * LANES:(h + 1) * LANES]
        r = lax.rsqrt(jnp.mean(yh * yh, axis=-1, keepdims=True) + RMS_EPS)
        outs.append(yh * r * gain[:, h * LANES:(h + 1) * LANES] * scale)
    return jnp.concatenate(outs, axis=1)


def _proj_qnorm_kernel(x_ref, w_ref, g_ref, o_ref, *, scale):
    y = jnp.dot(x_ref[...], w_ref[...], preferred_element_type=F32)
    o_ref[...] = _head_rmsnorm(y, g_ref[...], scale).astype(o_ref.dtype)


def _proj_knorm_kernel(x_ref, w_ref, g_ref, o_ref, km_ref, *, nb_seq):
    y = jnp.dot(x_ref[...], w_ref[...], preferred_element_type=F32)
    yn = _head_rmsnorm(y, g_ref[...], 1.0)
    tm = yn.shape[0]
    row = lax.broadcasted_iota(jnp.int32, (tm, LANES), 0) + pl.program_id(0) * tm
    lane = lax.broadcasted_iota(jnp.int32, (tm, LANES), 1)
    onehot = jnp.where(lane == lax.rem(row // MOBA_BLOCK, nb_seq), 1.0, 0.0).astype(o_ref.dtype)
    yb = yn.astype(o_ref.dtype)
    parts = []
    for h in range(yn.shape[1] // LANES):
        parts += [yb[:, h * LANES:(h + 1) * LANES], onehot]
    o_ref[...] = jnp.concatenate(parts, axis=1)
    for r in range(tm // MOBA_BLOCK):
        blk = yn[r * MOBA_BLOCK:(r + 1) * MOBA_BLOCK]
        km_ref[r] = jnp.mean(blk, axis=0, keepdims=True)


def _proj_moba_q(x, w, gain, tm, tn):
    m, k = x.shape
    n = w.shape[1]
    g = jnp.tile(gain.reshape(1, MOBA_DH), (1, n // MOBA_DH))
    return pl.pallas_call(
        functools.partial(_proj_qnorm_kernel, scale=MOBA_DH ** -0.5 * LOG2E),
        grid=(m // tm, n // tn),
        in_specs=[pl.BlockSpec((tm, k), lambda i, j: (i, 0)), pl.BlockSpec((k, tn), lambda i, j: (0, j)),
                  pl.BlockSpec((1, tn), lambda i, j: (0, j))],
        out_specs=pl.BlockSpec((tm, tn), lambda i, j: (i, j)),
        out_shape=jax.ShapeDtypeStruct((m, n), BF16),
        compiler_params=_params(("parallel", "arbitrary")),
        name="proj_moba_q",
    )(x, w, g)


def _proj_moba_k(x, w, gain, tm, tn, nb_seq):
    m, k = x.shape
    n = w.shape[1]
    g = jnp.tile(gain.reshape(1, MOBA_DH), (1, n // MOBA_DH))
    nb = tm // MOBA_BLOCK
    return pl.pallas_call(
        functools.partial(_proj_knorm_kernel, nb_seq=nb_seq),
        grid=(m // tm, n // tn),
        in_specs=[pl.BlockSpec((tm, k), lambda i, j: (i, 0)), pl.BlockSpec((k, tn), lambda i, j: (0, j)),
                  pl.BlockSpec((1, tn), lambda i, j: (0, j))],
        out_specs=[pl.BlockSpec((tm, 2 * tn), lambda i, j: (i, j)),
                   pl.BlockSpec((nb, 1, tn), lambda i, j: (i, 0, j))],
        out_shape=[jax.ShapeDtypeStruct((m, 2 * n), BF16),
                   jax.ShapeDtypeStruct((m // MOBA_BLOCK, 1, n), F32)],
        compiler_params=_params(("parallel", "arbitrary")),
        name="proj_moba_k",
    )(x, w, g)


def _gdn_kernel(alog_ref, dtb_ref, xq_ref, xk_ref, xv_ref, z_ref, ba_ref, wq_ref, wk_ref, wv_ref,
                gn_ref, o_ref, s_ref, tq_ref, tk_ref, tv_ref, q_s, k_s, v_s, g_s, b_s, *, tb):
    h = pl.program_id(1)
    t = pl.program_id(2)
    C = GDN_CHUNK

    @pl.when(t == 0)
    def _():
        s_ref[...] = jnp.zeros_like(s_ref)
        tq_ref[...] = jnp.zeros_like(tq_ref)
        tk_ref[...] = jnp.zeros_like(tk_ref)
        tv_ref[...] = jnp.zeros_like(tv_ref)

    def conv_silu(x_ref, tail_ref, w_ref):
        x = x_ref[0]
        xp = jnp.concatenate([tail_ref[...], x], axis=0)
        w = w_ref[...]
        y = x * w[GDN_CONV - 1:GDN_CONV]
        for j in range(GDN_CONV - 1):
            shifted = pltpu.roll(xp, GDN_CONV - 1 - j, axis=0)[8:8 + tb]
            y = y + shifted * w[j:j + 1]
        tail_ref[...] = x[tb - 8:tb]
        return y * _sigmoid(y)

    q = conv_silu(xq_ref, tq_ref, wq_ref)
    k = conv_silu(xk_ref, tk_ref, wk_ref)
    v_s[...] = conv_silu(xv_ref, tv_ref, wv_ref)
    q_s[...] = q * lax.rsqrt(jnp.sum(q * q, axis=-1, keepdims=True) + 1e-6) * (GDN_DK ** -0.5)
    k_s[...] = k * lax.rsqrt(jnp.sum(k * k, axis=-1, keepdims=True) + 1e-6)

    ba = ba_ref[0]
    lane = lax.broadcasted_iota(jnp.int32, ba.shape, 1)
    bcol = jnp.sum(jnp.where(lane == h, ba, 0.0), axis=1, keepdims=True)
    acol = jnp.sum(jnp.where(lane == h + GDN_HEADS, ba, 0.0), axis=1, keepdims=True)
    ones_row = jnp.ones((1, LANES), F32)
    a_exp = jnp.exp(ones_row * alog_ref[h])
    sp_in = acol + dtb_ref[h]
    softplus = jnp.maximum(sp_in, 0.0) + jnp.log1p(jnp.exp(-jnp.abs(sp_in)))
    g_s[...] = -(softplus * a_exp)
    b_s[...] = _sigmoid(bcol) * ones_row

    ri = lax.broadcasted_iota(jnp.int32, (C, C), 0)
    ci = lax.broadcasted_iota(jnp.int32, (C, C), 1)
    tril = ri >= ci
    strict = ri > ci
    eye = ri == ci
    tril_f = tril.astype(F32)
    ones_cc = jnp.ones((C, C), F32)
    eye_f = eye.astype(F32)
    gn = gn_ref[...]

    def chunk(c, carry):
        sl = pl.ds(pl.multiple_of(c * C, C), C)
        qc, kc, vc = q_s[sl, :], k_s[sl, :], v_s[sl, :]
        beta = b_s[sl, :]
        gcb = jnp.dot(tril_f, g_s[sl, :], precision=HI, preferred_element_type=F32)
        gcc = gcb[:, :C]
        grow = jnp.dot(ones_cc, jnp.where(eye, gcc, 0.0), precision=HI, preferred_element_type=F32)
        decay = jnp.exp(jnp.where(tril, gcc - grow, -jnp.inf))
        k_beta = kc * beta
        v_beta = vc * beta
        a_mat = jnp.where(strict, _dot_t(k_beta, kc, HI) * decay, 0.0)
        t_inv = eye_f - jnp.where((ri >> 1 == ci >> 1), a_mat, 0.0)
        s = 2
        while s < C:
            sh = s.bit_length()
            off = (ri >> sh == ci >> sh) & ((ri & (2 * s - 1)) >= s) & ((ci & (2 * s - 1)) < s)
            a_off = jnp.where(off, a_mat, 0.0)
            t_inv = t_inv - jnp.dot(t_inv, jnp.dot(a_off, t_inv, precision=HI, preferred_element_type=F32),
                                    precision=HI, preferred_element_type=F32)
            s *= 2
        egc = jnp.exp(gcb)
        u = jnp.dot(t_inv, v_beta, precision=HI, preferred_element_type=F32)
        w = jnp.dot(t_inv, k_beta * egc, precision=HI, preferred_element_type=F32)
        qk = jnp.where(tril, _dot_t(qc, kc, HI) * decay, 0.0)
        gl = gcb[C - 1:C, :]
        egl = jnp.exp(gl)
        k_dec = kc * jnp.exp(gl - gcb)
        q_dec = qc * egc
        s_mat = s_ref[...]
        v_new = u - jnp.dot(w, s_mat, precision=HI, preferred_element_type=F32)
        o = (jnp.dot(q_dec, s_mat, precision=HI, preferred_element_type=F32)
             + jnp.dot(qk, v_new, precision=HI, preferred_element_type=F32))
        s_ref[...] = s_mat * egl + lax.dot_general(k_dec, v_new, (((0,), (0,)), ((), ())), precision=HI,
                                                   preferred_element_type=F32)
        on = o * lax.rsqrt(jnp.mean(o * o, axis=-1, keepdims=True) + RMS_EPS) * gn
        zc = z_ref[0, sl, :]
        o_ref[0, sl, :] = (on * (zc * _sigmoid(zc))).astype(o_ref.dtype)
        return carry

    lax.fori_loop(0, tb // C, chunk, 0)


def _gdn(proj, ba, conv_wt, a_log, dt_bias, gnorm, tb):
    b, t, _ = proj.shape
    nh = GDN_HEADS
    blk = lambda off: pl.BlockSpec((1, tb, LANES), lambda bi, hi, ti: (bi, ti, off + hi))
    wblk = lambda off: pl.BlockSpec((GDN_CONV, LANES), lambda bi, hi, ti: (0, off + hi))
    smem = pl.BlockSpec(memory_space=pltpu.SMEM)
    return pl.pallas_call(
        functools.partial(_gdn_kernel, tb=tb),
        grid=(b, nh, t // tb),
        in_specs=[smem, smem, blk(0), blk(nh), blk(2 * nh), blk(3 * nh),
                  pl.BlockSpec((1, tb, LANES), lambda bi, hi, ti: (bi, ti, 0)),
                  wblk(0), wblk(nh), wblk(2 * nh),
                  pl.BlockSpec((1, LANES), lambda bi, hi, ti: (0, 0))],
        out_specs=pl.BlockSpec((1, tb, LANES), lambda bi, hi, ti: (bi, ti, hi)),
        out_shape=jax.ShapeDtypeStruct((b, t, GDN_W), BF16),
        scratch_shapes=[pltpu.VMEM((GDN_DK, GDN_DV), F32)] + [pltpu.VMEM((8, LANES), F32)] * 3
        + [pltpu.VMEM((tb, LANES), F32)] * 5,
        compiler_params=_params(("parallel", "parallel", "arbitrary")),
        name="gdn",
    )(a_log, dt_bias, proj, proj, proj, proj, ba, conv_wt, conv_wt, conv_wt, gnorm.reshape(1, GDN_DV))


def _bdot(a, b):
    return jnp.dot(a.astype(BF16), b.astype(BF16), preferred_element_type=F32)


def _gdn_all_kernel(x_ref, ba_ref, cw_ref, alog_ref, dtb_ref, gn_ref, o_ref, s_ref, tail_ref, qkv_s, *, tb):
    t = pl.program_id(1)
    C = GDN_CHUNK
    nc = tb // C
    nh = GDN_HEADS

    @pl.when(t == 0)
    def _():
        s_ref[...] = jnp.zeros_like(s_ref)
        tail_ref[...] = jnp.zeros_like(tail_ref)

    def conv_group(gi, carry):
        off = pl.multiple_of(gi * LANES, LANES)
        x = x_ref[0, :, pl.ds(off, LANES)]
        xp = jnp.concatenate([tail_ref[:, pl.ds(off, LANES)], x], axis=0)
        w = cw_ref[:, pl.ds(off, LANES)]
        y = x * w[GDN_CONV - 1:GDN_CONV]
        for j in range(GDN_CONV - 1):
            y = y + pltpu.roll(xp, GDN_CONV - 1 - j, axis=0)[8:8 + tb] * w[j:j + 1]
        tail_ref[:, pl.ds(off, LANES)] = x[tb - 8:tb]
        y = y * _sigmoid(y)
        r = lax.rsqrt(jnp.sum(y * y, axis=-1, keepdims=True) + 1e-6)
        fac = jnp.where(gi < nh, r * (GDN_DK ** -0.5), jnp.where(gi < 2 * nh, r, jnp.ones_like(r)))
        qkv_s[:, pl.ds(off, LANES)] = y * fac
        return carry

    lax.fori_loop(0, 3 * nh, conv_group, 0)

    ba = ba_ref[0]
    beta_all = _sigmoid(ba)
    sp_in = ba + dtb_ref[...]
    softplus = jnp.maximum(sp_in, 0.0) + jnp.log1p(jnp.exp(-jnp.abs(sp_in)))
    gc = -(jnp.exp(alog_ref[...]) * softplus)
    row_in_chunk = lax.broadcasted_iota(jnp.int32, (tb, LANES), 0) & (C - 1)
    sh = 1
    while sh < C:
        gc = gc + jnp.where(row_in_chunk >= sh, pltpu.roll(gc, sh, axis=0), 0.0)
        sh *= 2
    gl_rows = [gc[(c + 1) * C - 1:(c + 1) * C] for c in range(nc)]
    gl_b = jnp.concatenate([jnp.broadcast_to(g, (C, LANES)) for g in gl_rows], axis=0)
    egc = jnp.exp(gc)
    kdsc = jnp.exp(gl_b - gc)
    egl = [jnp.exp(g) for g in gl_rows]
    gc_t = gc.T

    ri = lax.broadcasted_iota(jnp.int32, (C, C), 0)
    ci = lax.broadcasted_iota(jnp.int32, (C, C), 1)
    tril = ri >= ci
    strict = ri > ci
    eye_f = (ri == ci).astype(F32)
    lvl_masks = []
    s = 1
    while s < C:
        sh2 = s.bit_length()
        lvl_masks.append((ri >> sh2 == ci >> sh2) & ((ri & (2 * s - 1)) >= s) & ((ci & (2 * s - 1)) < s))
        s *= 2
    gn = gn_ref[...]

    states = [s_ref[h] for h in range(nh)]
    for c in range(nc):
        r0 = c * C
        heads = range(nh)
        rows = slice(r0, r0 + C)
        grp = lambda g, h: slice((g * nh + h) * LANES, (g * nh + h + 1) * LANES)
        qc = [qkv_s[rows, grp(0, h)] for h in heads]
        kc = [qkv_s[rows, grp(1, h)] for h in heads]
        vc = [qkv_s[rows, grp(2, h)] for h in heads]
        bcol = [beta_all[rows, h:h + 1] for h in heads]
        ecol = [egc[rows, nh + h:nh + h + 1] for h in heads]
        decay = [jnp.exp(jnp.where(tril, gc[rows, nh + h:nh + h + 1] - gc_t[nh + h:nh + h + 1, rows], -jnp.inf))
                 for h in heads]
        k_beta = [kc[h] * bcol[h] for h in heads]
        st = [_dot_t(jnp.concatenate([k_beta[h], qc[h]], axis=0).astype(BF16), kc[h].astype(BF16))
              for h in heads]
        a_mat = [jnp.where(strict, st[h][:C] * decay[h], 0.0) for h in heads]
        qk = [jnp.where(tril, st[h][C:] * decay[h], 0.0) for h in heads]
        t_inv = [eye_f - jnp.where(lvl_masks[0], a_mat[h], 0.0) for h in heads]
        for msk in lvl_masks[1:]:
            x_mid = [_bdot(jnp.where(msk, a_mat[h], 0.0), t_inv[h]) for h in heads]
            y_mid = [_bdot(t_inv[h], x_mid[h]) for h in heads]
            t_inv = [t_inv[h] - y_mid[h] for h in heads]
        uw = [_bdot(t_inv[h], jnp.concatenate([vc[h] * bcol[h], k_beta[h] * ecol[h]], axis=1))
              for h in heads]
        wq = [_bdot(jnp.concatenate([uw[h][:, LANES:], qc[h] * ecol[h]], axis=0), states[h])
              for h in heads]
        v_new = [uw[h][:, :LANES] - wq[h][:C] for h in heads]
        o = [wq[h][C:] + _bdot(qk[h], v_new[h]) for h in heads]
        k_dec = [kc[h] * kdsc[rows, nh + h:nh + h + 1] for h in heads]
        states = [states[h] * egl[c][:, nh + h:nh + h + 1] + lax.dot_general(
            k_dec[h].astype(BF16), v_new[h].astype(BF16), (((0,), (0,)), ((), ())),
            preferred_element_type=F32) for h in heads]
        for h in heads:
            on = o[h] * lax.rsqrt(jnp.mean(o[h] * o[h], axis=-1, keepdims=True) + RMS_EPS) * gn
            zc = x_ref[0, rows, grp(3, h)]
            o_ref[0, rows, grp(0, h)] = (on * (zc * _sigmoid(zc))).astype(o_ref.dtype)
    for h in range(nh):
        s_ref[h] = states[h]


def _gdn_all(proj, ba, conv_wt, a_log, dt_bias, gnorm, tb):
    b, t, width = proj.shape
    nh = GDN_HEADS
    lane_vec = lambda v: jnp.pad(v.reshape(1, nh), ((0, 0), (nh, LANES - 2 * nh)))
    full = lambda shape: pl.BlockSpec(shape, lambda bi, ti: (0,) * len(shape))
    return pl.pallas_call(
        functools.partial(_gdn_all_kernel, tb=tb),
        grid=(b, t // tb),
        in_specs=[pl.BlockSpec((1, tb, width), lambda bi, ti: (bi, ti, 0)),
                  pl.BlockSpec((1, tb, LANES), lambda bi, ti: (bi, ti, 0)),
                  full((GDN_CONV, 3 * GDN_W)), full((1, LANES)), full((1, LANES)), full((1, GDN_DV))],
        out_specs=pl.BlockSpec((1, tb, GDN_W), lambda bi, ti: (bi, ti, 0)),
        out_shape=jax.ShapeDtypeStruct((b, t, GDN_W), BF16),
        scratch_shapes=[pltpu.VMEM((nh, GDN_DK, GDN_DV), F32), pltpu.VMEM((8, 3 * GDN_W), F32),
                        pltpu.VMEM((tb, 3 * GDN_W), F32)],
        compiler_params=_params(("parallel", "arbitrary")),
        name="gdn",
    )(proj, ba, conv_wt, lane_vec(a_log), lane_vec(dt_bias), gnorm.reshape(1, GDN_DV))


def _moba_kernel(qt_ref, ka_ref, vt_ref, km_ref, o_ref, qa_ref, acc_ref, m_ref, sa_ref, sb_ref, pa_ref, pb_ref,
                 aa_ref, ab_ref, *, nb, group):
    i = pl.program_id(2)
    BLK = MOBA_BLOCK
    nbp = -(-nb // 16) * 16
    qt = qt_ref[0]
    km = km_ref[0].astype(BF16)
    if nbp > nb:
        km = jnp.concatenate([km, jnp.zeros((nbp - nb, LANES), BF16)], axis=0)
    gate = jnp.dot(km, qt, preferred_element_type=F32)
    row = lax.broadcasted_iota(jnp.int32, (nbp, BLK), 0)
    past = row < i
    g = jnp.where(past, gate, -jnp.inf)
    sel = row < 0
    for _ in range(MOBA_TOPK):
        m = jnp.max(g, axis=0, keepdims=True)
        idx = jnp.min(jnp.where(g == m, row, nbp), axis=0, keepdims=True)
        hit = row == idx
        sel = sel | (hit & past)
        g = jnp.where(hit, -jnp.inf, g)
    bias = jnp.where(sel, 0.0, MASK_NEG).astype(BF16)
    qa_ref[0:LANES, :] = qt
    qa_ref[LANES:LANES + nbp, :] = bias
    if nbp < LANES:
        qa_ref[LANES + nbp:, :] = jnp.zeros((LANES - nbp, BLK), BF16)

    ROWS_L = 16

    def pv(p, start, width):
        lhs = jnp.concatenate([vt_ref[0, :, pl.ds(start, width)], jnp.ones((ROWS_L, width), BF16)], axis=0)
        return jnp.dot(lhs, p, preferred_element_type=F32)

    start0 = pl.multiple_of(i * BLK, BLK)
    s0 = jnp.dot(ka_ref[0, pl.ds(start0, BLK), 0:LANES], qt, preferred_element_type=F32)
    key_i = lax.broadcasted_iota(jnp.int32, (BLK, BLK), 0)
    qry_i = lax.broadcasted_iota(jnp.int32, (BLK, BLK), 1)
    s0 = jnp.where(key_i <= qry_i, s0, MASK_NEG)
    m0 = jnp.max(s0, axis=0, keepdims=True)
    m_ref[...] = m0
    acc_ref[...] = pv(jnp.exp2(s0 - m0).astype(BF16), start0, BLK)

    span = group * BLK
    n_pairs = (i + 2 * group - 1) // (2 * group)
    last = nb // group - 1

    def qk(g_idx, dst_ref):
        start = pl.multiple_of(g_idx * span, span)
        dst_ref[...] = jnp.dot(ka_ref[0, pl.ds(start, span), :], qa_ref[...], preferred_element_type=F32)

    def softmax(src_ref, p_dst, a_dst):
        s = src_ref[...]
        m_old = m_ref[...]
        m_new = jnp.maximum(m_old, jnp.max(s, axis=0, keepdims=True))
        a_dst[...] = jnp.exp2(m_old - m_new)
        m_ref[...] = m_new
        p_dst[...] = jnp.exp2(s - m_new).astype(BF16)

    def accum(p_src, a_src, g_idx):
        acc_ref[...] = a_src[...] * acc_ref[...] + pv(p_src[...], pl.multiple_of(g_idx * span, span), span)

    qk(0, sa_ref)
    pb_ref[...] = jnp.zeros_like(pb_ref)
    ab_ref[...] = jnp.ones_like(ab_ref)

    def body(jj, carry):
        accum(pb_ref, ab_ref, jnp.maximum(2 * jj - 1, 0))
        qk(2 * jj + 1, sb_ref)
        softmax(sa_ref, pa_ref, aa_ref)
        accum(pa_ref, aa_ref, 2 * jj)
        qk(jnp.minimum(2 * jj + 2, last), sa_ref)
        softmax(sb_ref, pb_ref, ab_ref)
        return carry

    lax.fori_loop(0, n_pairs, body, 0)
    accum(pb_ref, ab_ref, jnp.maximum(2 * n_pairs - 1, 0))
    acc = acc_ref[...]
    o = acc[0:LANES] * (1.0 / acc[LANES:LANES + 1])
    o_ref[0] = o.T.astype(o_ref.dtype)


def _moba(qt, kaug, vt, kmean):
    b, _, t = qt.shape
    nb = t // MOBA_BLOCK
    group = 4 if nb % 8 == 0 else 2
    assert nb <= LANES and nb % (2 * group) == 0
    span = group * MOBA_BLOCK
    return pl.pallas_call(
        functools.partial(_moba_kernel, nb=nb, group=group),
        grid=(b, MOBA_HEADS, nb),
        in_specs=[pl.BlockSpec((1, LANES, MOBA_BLOCK), lambda bi, hi, qi: (bi, hi, qi)),
                  pl.BlockSpec((1, t, 2 * LANES), lambda bi, hi, qi: (bi, 0, hi)),
                  pl.BlockSpec((1, LANES, t), lambda bi, hi, qi: (bi, hi, 0)),
                  pl.BlockSpec((1, nb, LANES), lambda bi, hi, qi: (bi, 0, hi))],
        out_specs=pl.BlockSpec((1, MOBA_BLOCK, LANES), lambda bi, hi, qi: (bi, qi, hi)),
        out_shape=jax.ShapeDtypeStruct((b, t, MOBA_W), BF16),
        scratch_shapes=[pltpu.VMEM((2 * LANES, MOBA_BLOCK), BF16),
                        pltpu.VMEM((LANES + 16, MOBA_BLOCK), F32),
                        pltpu.VMEM((1, MOBA_BLOCK), F32),
                        pltpu.VMEM((span, MOBA_BLOCK), F32), pltpu.VMEM((span, MOBA_BLOCK), F32),
                        pltpu.VMEM((span, MOBA_BLOCK), BF16), pltpu.VMEM((span, MOBA_BLOCK), BF16),
                        pltpu.VMEM((1, MOBA_BLOCK), F32), pltpu.VMEM((1, MOBA_BLOCK), F32)],
        compiler_params=_params(("parallel", "parallel", "arbitrary")),
        name="moba",
    )(qt, kaug, vt, kmean)


def _oproj_kernel(og_ref, om_ref, wa_ref, wb_ref, x_ref, nw_ref, h_ref, hn_ref):
    y = jnp.dot(og_ref[...], wa_ref[...], preferred_element_type=F32)
    y = y + jnp.dot(om_ref[...], wb_ref[...], preferred_element_type=F32)
    h = x_ref[...] + y
    h_ref[...] = h
    hn = h * lax.rsqrt(jnp.mean(h * h, axis=-1, keepdims=True) + RMS_EPS)
    hn_ref[...] = (hn * nw_ref[...]).astype(hn_ref.dtype)


def _oproj(og, om, wa, wb, x, norm_w, tm):
    m, d = x.shape
    ka, kb = og.shape[1], om.shape[1]
    row = lambda w: pl.BlockSpec((tm, w), lambda i: (i, 0))
    full = lambda r, c: pl.BlockSpec((r, c), lambda i: (0, 0))
    return pl.pallas_call(
        _oproj_kernel,
        grid=(m // tm,),
        in_specs=[row(ka), row(kb), full(ka, d), full(kb, d), row(d), full(1, d)],
        out_specs=[row(d), row(d)],
        out_shape=[jax.ShapeDtypeStruct((m, d), F32), jax.ShapeDtypeStruct((m, d), BF16)],
        compiler_params=_params(("parallel",)),
        name="oproj",
    )(og, om, wa, wb, x, norm_w.reshape(1, d))


def _ffn_kernel(hn_ref, wg_ref, wu_ref, wd_ref, h_ref, nw_ref, h2_ref, hn2_ref, acc_ref):
    f = pl.program_id(1)

    @pl.when(f == 0)
    def _():
        acc_ref[...] = jnp.zeros_like(acc_ref)

    hn = hn_ref[...]
    g = jnp.dot(hn, wg_ref[...], preferred_element_type=F32)
    u = jnp.dot(hn, wu_ref[...], preferred_element_type=F32)
    a = (g * _sigmoid(g) * u).astype(BF16)
    acc_ref[...] += jnp.dot(a, wd_ref[...], preferred_element_type=F32)

    @pl.when(f == pl.num_programs(1) - 1)
    def _():
        h2 = h_ref[...] + acc_ref[...]
        h2_ref[...] = h2
        n = h2 * lax.rsqrt(jnp.mean(h2 * h2, axis=-1, keepdims=True) + RMS_EPS)
        hn2_ref[...] = (n * nw_ref[...]).astype(hn2_ref.dtype)


def _ffn(hn, wg, wu, wd, h, norm_w, tm, tf):
    m, d = h.shape
    dff = wg.shape[1]
    row = pl.BlockSpec((tm, d), lambda i, f: (i, 0))
    return pl.pallas_call(
        _ffn_kernel,
        grid=(m // tm, dff // tf),
        in_specs=[row, pl.BlockSpec((d, tf), lambda i, f: (0, f)), pl.BlockSpec((d, tf), lambda i, f: (0, f)),
                  pl.BlockSpec((tf, d), lambda i, f: (f, 0)), row, pl.BlockSpec((1, d), lambda i, f: (0, 0))],
        out_specs=[row, row],
        out_shape=[jax.ShapeDtypeStruct((m, d), F32), jax.ShapeDtypeStruct((m, d), BF16)],
        scratch_shapes=[pltpu.VMEM((tm, d), F32)],
        compiler_params=_params(("parallel", "arbitrary")),
        name="ffn",
    )(hn, wg, wu, wd, h, norm_w.reshape(1, d))


def _ple_kernel(hn_ref, wg_ref, p_ref, wp_ref, h_ref, o_ref):
    gate = _sigmoid(jnp.dot(hn_ref[...], wg_ref[...], preferred_element_type=F32))
    proj = jnp.dot(p_ref[...].astype(BF16), wp_ref[...], preferred_element_type=F32)
    o_ref[...] = h_ref[...] + gate * proj


def _ple(hn, wg, p, wp, h, tm, tn):
    m, d = h.shape
    kp = p.shape[1]
    return pl.pallas_call(
        _ple_kernel,
        grid=(m // tm, d // tn),
        in_specs=[pl.BlockSpec((tm, d), lambda i, j: (i, 0)), pl.BlockSpec((d, tn), lambda i, j: (0, j)),
                  pl.BlockSpec((tm, kp), lambda i, j: (i, 0)), pl.BlockSpec((kp, tn), lambda i, j: (0, j)),
                  pl.BlockSpec((tm, tn), lambda i, j: (i, j))],
        out_specs=pl.BlockSpec((tm, tn), lambda i, j: (i, j)),
        out_shape=jax.ShapeDtypeStruct((m, d), F32),
        compiler_params=_params(("parallel", "arbitrary")),
        name="ple",
    )(hn, wg, p, wp, h)


def _layer(h, p, attn_norm, w_in, conv_w, a_log, dt_bias, gdn_norm, q_norm, k_norm, w_o, ffn_norm,
           w_gate, w_up, w_down, ple_norm, w_ple_gate, w_ple_proj):
    b, t, d = h.shape
    m = b * t
    x2 = h.reshape(m, d)
    tm = min(1024, m)

    o_ba = 4 * GDN_W
    o_mq = o_ba + 2 * GDN_HEADS
    w_in16 = w_in.astype(BF16)
    w_gdn = w_in16[:, :o_ba]
    w_ba = jnp.pad(w_in16[:, o_ba:o_mq], ((0, 0), (0, LANES - 2 * GDN_HEADS)))
    w_mq = w_in16[:, o_mq:o_mq + MOBA_W]
    w_mk = w_in16[:, o_mq + MOBA_W:o_mq + 2 * MOBA_W]
    w_mv = w_in16[:, o_mq + 2 * MOBA_W:o_mq + 3 * MOBA_W]

    xn = _rmsnorm(x2, attn_norm, min(512, m))
    gproj = _proj(xn, w_gdn, F32, tm, 512, "proj_gdn")
    ba = _proj(xn, w_ba, F32, tm, LANES, "proj_ba")
    mq = _proj_moba_q(xn, w_mq, q_norm, tm, 512)
    kaug, kmean = _proj_moba_k(xn, w_mk, k_norm, tm, 512, t // MOBA_BLOCK)
    mv = _proj(xn, w_mv, BF16, tm, 512, "proj_moba_v")

    o_gdn = _gdn_all(gproj.reshape(b, t, 4 * GDN_W), ba.reshape(b, t, LANES), conv_w.T, a_log, dt_bias,
                     gdn_norm, min(256, t))
    o_moba = _moba(mq.reshape(b, t, MOBA_W).swapaxes(1, 2), kaug.reshape(b, t, 2 * MOBA_W),
                   mv.reshape(b, t, MOBA_W).swapaxes(1, 2), kmean.reshape(b, t // MOBA_BLOCK, MOBA_W))

    w_o16 = w_o.astype(BF16)
    h1, hn = _oproj(o_gdn.reshape(m, GDN_W), o_moba.reshape(m, MOBA_W), w_o16[:GDN_W], w_o16[GDN_W:],
                    x2, ffn_norm, min(256, m))
    h2, hn2 = _ffn(hn, w_gate.astype(BF16), w_up.astype(BF16), w_down.astype(BF16), h1, ple_norm,
                   min(512, m), 512)
    h3 = _ple(hn2, w_ple_gate.astype(BF16), p.reshape(m, PLE_DIM), w_ple_proj.astype(BF16), h2, tm, 512)
    return h3.reshape(b, t, d)


def kernel(x, p, attn_norm, w_in, conv_w, A_log, dt_bias, gdn_norm, q_norm, k_norm, w_o, ffn_norm,
           w_gate, w_up, w_down, ple_norm, w_ple_gate, w_ple_proj):
    h = x
    for i in range(p.shape[0]):
        h = _layer(h, p[i], attn_norm[i], w_in[i], conv_w[i], A_log[i], dt_bias[i], gdn_norm[i],
                   q_norm[i], k_norm[i], w_o[i], ffn_norm[i], w_gate[i], w_up[i], w_down[i],
                   ple_norm[i], w_ple_gate[i], w_ple_proj[i])
    return h
```

```python
import functools

import jax
import jax.numpy as jnp
from jax import lax
from jax.experimental import pallas as pl
from jax.experimental.pallas import tpu as pltpu

D_MODEL = 2048
PLE_DIM = 256
GDN_HEADS = 8
GDN_DK = 128
GDN_DV = 128
GDN_CONV = 4
GDN_CHUNK = 64
MOBA_HEADS = 8
MOBA_DH = 128
MOBA_BLOCK = 256
MOBA_TOPK = 3
RMS_EPS = 1e-6
GDN_W = GDN_HEADS * GDN_DK
MOBA_W = MOBA_HEADS * MOBA_DH
LANES = 128
LOG2E = 1.4426950408889634
MASK_NEG = -1e30

F32 = jnp.float32
BF16 = jnp.bfloat16

VMEM_LIMIT = 56 * 1024 * 1024


def _params(sem):
    return pltpu.CompilerParams(dimension_semantics=sem, vmem_limit_bytes=VMEM_LIMIT)


def _sigmoid(x):
    return 1.0 / (1.0 + jnp.exp(-x))


def _dot_t(a, b):
    return lax.dot_general(a, b, (((1,), (1,)), ((), ())), preferred_element_type=F32)


def _bdot(a, b):
    return jnp.dot(a.astype(BF16), b.astype(BF16), preferred_element_type=F32)


def _rmsnorm_kernel(x_ref, w_ref, o_ref):
    x = x_ref[...]
    y = x * lax.rsqrt(jnp.mean(x * x, axis=-1, keepdims=True) + RMS_EPS)
    o_ref[...] = (y * w_ref[...]).astype(o_ref.dtype)


def _rmsnorm(x, w, tm):
    m, d = x.shape
    return pl.pallas_call(
        _rmsnorm_kernel,
        grid=(m // tm,),
        in_specs=[pl.BlockSpec((tm, d), lambda i: (i, 0)), pl.BlockSpec((1, d), lambda i: (0, 0))],
        out_specs=pl.BlockSpec((tm, d), lambda i: (i, 0)),
        out_shape=jax.ShapeDtypeStruct((m, d), BF16),
        compiler_params=_params(("parallel",)),
        name="rmsnorm_cast",
    )(x, w.reshape(1, d))


def _proj_kernel(x_ref, w_ref, o_ref):
    o_ref[...] = jnp.dot(x_ref[...], w_ref[...], preferred_element_type=F32).astype(o_ref.dtype)


def _proj(x, w, out_dtype, tm, tn, name):
    m, k = x.shape
    n = w.shape[1]
    return pl.pallas_call(
        _proj_kernel,
        grid=(m // tm, n // tn),
        in_specs=[pl.BlockSpec((tm, k), lambda i, j: (i, 0)), pl.BlockSpec((k, tn), lambda i, j: (0, j))],
        out_specs=pl.BlockSpec((tm, tn), lambda i, j: (i, j)),
        out_shape=jax.ShapeDtypeStruct((m, n), out_dtype),
        compiler_params=_params(("parallel", "arbitrary")),
        name=name,
    )(x, w)


def _head_rmsnorm(y, gain, scale):
    outs = []
    for h in range(y.shape[1] // LANES):
        yh = y[:, h * LANES:(h + 1) * LANES]
        r = lax.rsqrt(jnp.mean(yh * yh, axis=-1, keepdims=True) + RMS_EPS)
        outs.append(yh * r * gain[:, h * LANES:(h + 1) * LANES] * scale)
    return jnp.concatenate(outs, axis=1)


def _proj_qnorm_kernel(x_ref, w_ref, g_ref, o_ref, *, scale):
    y = jnp.dot(x_ref[...], w_ref[...], preferred_element_type=F32)
    o_ref[...] = _head_rmsnorm(y, g_ref[...], scale).astype(o_ref.dtype)


def _proj_knorm_kernel(x_ref, w_ref, g_ref, o_ref, km_ref, *, nb_seq):
    y = jnp.dot(x_ref[...], w_ref[...], preferred_element_type=F32)
    yn = _head_rmsnorm(y, g_ref[...], 1.0)
    tm = yn.shape[0]
    row = lax.broadcasted_iota(jnp.int32, (tm, LANES), 0) + pl.program_id(0) * tm
    lane = lax.broadcasted_iota(jnp.int32, (tm, LANES), 1)
    onehot = jnp.where(lane == lax.rem(row // MOBA_BLOCK, nb_seq), 1.0, 0.0).astype(o_ref.dtype)
    yb = yn.astype(o_ref.dtype)
    parts = []
    for h in range(yn.shape[1] // LANES):
        parts += [yb[:, h * LANES:(h + 1) * LANES], onehot]
    o_ref[...] = jnp.concatenate(parts, axis=1)
    for r in range(tm // MOBA_BLOCK):
        blk = yn[r * MOBA_BLOCK:(r + 1) * MOBA_BLOCK]
        km_ref[r] = jnp.mean(blk, axis=0, keepdims=True)


def _proj_moba_q(x, w, gain, tm, tn):
    m, k = x.shape
    n = w.shape[1]
    g = jnp.tile(gain.reshape(1, MOBA_DH), (1, n // MOBA_DH))
    return pl.pallas_call(
        functools.partial(_proj_qnorm_kernel, scale=MOBA_DH ** -0.5 * LOG2E),
        grid=(m // tm, n // tn),
        in_specs=[pl.BlockSpec((tm, k), lambda i, j: (i, 0)), pl.BlockSpec((k, tn), lambda i, j: (0, j)),
                  pl.BlockSpec((1, tn), lambda i, j: (0, j))],
        out_specs=pl.BlockSpec((tm, tn), lambda i, j: (i, j)),
        out_shape=jax.ShapeDtypeStruct((m, n), BF16),
        compiler_params=_params(("parallel", "arbitrary")),
        name="proj_moba_q",
    )(x, w, g)


def _proj_moba_k(x, w, gain, tm, tn, nb_seq):
    m, k = x.shape
    n = w.shape[1]
    g = jnp.tile(gain.reshape(1, MOBA_DH), (1, n // MOBA_DH))
    nb = tm // MOBA_BLOCK
    return pl.pallas_call(
        functools.partial(_proj_knorm_kernel, nb_seq=nb_seq),
        grid=(m // tm, n // tn),
        in_specs=[pl.BlockSpec((tm, k), lambda i, j: (i, 0)), pl.BlockSpec((k, tn), lambda i, j: (0, j)),
                  pl.BlockSpec((1, tn), lambda i, j: (0, j))],
        out_specs=[pl.BlockSpec((tm, 2 * tn), lambda i, j: (i, j)),
                   pl.BlockSpec((nb, 1, tn), lambda i, j: (i, 0, j))],
        out_shape=[jax.ShapeDtypeStruct((m, 2 * n), BF16),
                   jax.ShapeDtypeStruct((m // MOBA_BLOCK, 1, n), F32)],
        compiler_params=_params(("parallel", "arbitrary")),
        name="proj_moba_k",
    )(x, w, g)


def _gdn_all_kernel(x_ref, ba_ref, cw_ref, alog_ref, dtb_ref, gn_ref, o_ref, s_ref, tail_ref, qkv_s, *, tb):
    t = pl.program_id(1)
    C = GDN_CHUNK
    nc = tb // C
    nh = GDN_HEADS

    @pl.when(t == 0)
    def _():
        s_ref[...] = jnp.zeros_like(s_ref)
        tail_ref[...] = jnp.zeros_like(tail_ref)

    def conv_group(gi, carry):
        off = pl.multiple_of(gi * LANES, LANES)
        x = x_ref[0, :, pl.ds(off, LANES)]
        xp = jnp.concatenate([tail_ref[:, pl.ds(off, LANES)], x], axis=0)
        w = cw_ref[:, pl.ds(off, LANES)]
        y = x * w[GDN_CONV - 1:GDN_CONV]
        for j in range(GDN_CONV - 1):
            y = y + pltpu.roll(xp, GDN_CONV - 1 - j, axis=0)[8:8 + tb] * w[j:j + 1]
        tail_ref[:, pl.ds(off, LANES)] = x[tb - 8:tb]
        y = y * _sigmoid(y)
        r = lax.rsqrt(jnp.sum(y * y, axis=-1, keepdims=True) + 1e-6)
        fac = jnp.where(gi < nh, r * (GDN_DK ** -0.5), jnp.where(gi < 2 * nh, r, jnp.ones_like(r)))
        qkv_s[:, pl.ds(off, LANES)] = y * fac
        return carry

    lax.fori_loop(0, 3 * nh, conv_group, 0)

    ba = ba_ref[0]
    beta_all = _sigmoid(ba)
    sp_in = ba + dtb_ref[...]
    softplus = jnp.maximum(sp_in, 0.0) + jnp.log1p(jnp.exp(-jnp.abs(sp_in)))
    gc = -(jnp.exp(alog_ref[...]) * softplus)
    row_in_chunk = lax.broadcasted_iota(jnp.int32, (tb, LANES), 0) & (C - 1)
    sh = 1
    while sh < C:
        gc = gc + jnp.where(row_in_chunk >= sh, pltpu.roll(gc, sh, axis=0), 0.0)
        sh *= 2
    gl_rows = [gc[(c + 1) * C - 1:(c + 1) * C] for c in range(nc)]
    gl_b = jnp.concatenate([jnp.broadcast_to(g, (C, LANES)) for g in gl_rows], axis=0)
    egc = jnp.exp(gc)
    kdsc = jnp.exp(gl_b - gc)
    egl = [jnp.exp(g) for g in gl_rows]
    gc_t = gc.T

    ri = lax.broadcasted_iota(jnp.int32, (C, C), 0)
    ci = lax.broadcasted_iota(jnp.int32, (C, C), 1)
    tril = ri >= ci
    strict = ri > ci
    eye_f = (ri == ci).astype(F32)
    lvl_masks = []
    s = 1
    while s < C:
        sh2 = s.bit_length()
        lvl_masks.append((ri >> sh2 == ci >> sh2) & ((ri & (2 * s - 1)) >= s) & ((ci & (2 * s - 1)) < s))
        s *= 2
    gn = gn_ref[...]

    states = [s_ref[h] for h in range(nh)]
    for c in range(nc):
        r0 = c * C
        heads = range(nh)
        rows = slice(r0, r0 + C)
        grp = lambda g, h: slice((g * nh + h) * LANES, (g * nh + h + 1) * LANES)
        qc = [qkv_s[rows, grp(0, h)] for h in heads]
        kc = [qkv_s[rows, grp(1, h)] for h in heads]
        vc = [qkv_s[rows, grp(2, h)] for h in heads]
        bcol = [beta_all[rows, h:h + 1] for h in heads]
        ecol = [egc[rows, nh + h:nh + h + 1] for h in heads]
        decay = [jnp.exp(jnp.where(tril, gc[rows, nh + h:nh + h + 1] - gc_t[nh + h:nh + h + 1, rows], -jnp.inf))
                 for h in heads]
        k_beta = [kc[h] * bcol[h] for h in heads]
        st = [_dot_t(jnp.concatenate([k_beta[h], qc[h]], axis=0).astype(BF16), kc[h].astype(BF16))
              for h in heads]
        a_mat = [jnp.where(strict, st[h][:C] * decay[h], 0.0) for h in heads]
        qk = [jnp.where(tril, st[h][C:] * decay[h], 0.0) for h in heads]
        t_inv = [eye_f - jnp.where(lvl_masks[0], a_mat[h], 0.0) for h in heads]
        for msk in lvl_masks[1:]:
            x_mid = [_bdot(jnp.where(msk, a_mat[h], 0.0), t_inv[h]) for h in heads]
            y_mid = [_bdot(t_inv[h], x_mid[h]) for h in heads]
            t_inv = [t_inv[h] - y_mid[h] for h in heads]
        uw = [_bdot(t_inv[h], jnp.concatenate([vc[h] * bcol[h], k_beta[h] * ecol[h]], axis=1))
              for h in heads]
        wq = [_bdot(jnp.concatenate([uw[h][:, LANES:], qc[h] * ecol[h]], axis=0), states[h])
              for h in heads]
        v_new = [uw[h][:, :LANES] - wq[h][:C] for h in heads]
        o = [wq[h][C:] + _bdot(qk[h], v_new[h]) for h in heads]
        k_dec = [kc[h] * kdsc[rows, nh + h:nh + h + 1] for h in heads]
        states = [states[h] * egl[c][:, nh + h:nh + h + 1] + lax.dot_general(
            k_dec[h].astype(BF16), v_new[h].astype(BF16), (((0,), (0,)), ((), ())),
            preferred_element_type=F32) for h in heads]
        for h in heads:
            on = o[h] * lax.rsqrt(jnp.mean(o[h] * o[h], axis=-1, keepdims=True) + RMS_EPS) * gn
            zc = x_ref[0, rows, grp(3, h)]
            o_ref[0, rows, grp(0, h)] = (on * (zc * _sigmoid(zc))).astype(o_ref.dtype)
    for h in range(nh):
        s_ref[h] = states[h]


def _gdn_all(proj, ba, conv_wt, a_log, dt_bias, gnorm, tb):
    b, t, width = proj.shape
    nh = GDN_HEADS
    lane_vec = lambda v: jnp.pad(v.reshape(1, nh), ((0, 0), (nh, LANES - 2 * nh)))
    full = lambda shape: pl.BlockSpec(shape, lambda bi, ti: (0,) * len(shape))
    return pl.pallas_call(
        functools.partial(_gdn_all_kernel, tb=tb),
        grid=(b, t // tb),
        in_specs=[pl.BlockSpec((1, tb, width), lambda bi, ti: (bi, ti, 0)),
                  pl.BlockSpec((1, tb, LANES), lambda bi, ti: (bi, ti, 0)),
                  full((GDN_CONV, 3 * GDN_W)), full((1, LANES)), full((1, LANES)), full((1, GDN_DV))],
        out_specs=pl.BlockSpec((1, tb, GDN_W), lambda bi, ti: (bi, ti, 0)),
        out_shape=jax.ShapeDtypeStruct((b, t, GDN_W), BF16),
        scratch_shapes=[pltpu.VMEM((nh, GDN_DK, GDN_DV), F32), pltpu.VMEM((8, 3 * GDN_W), F32),
                        pltpu.VMEM((tb, 3 * GDN_W), F32)],
        compiler_params=_params(("parallel", "arbitrary")),
        name="gdn",
    )(proj, ba, conv_wt, lane_vec(a_log), lane_vec(dt_bias), gnorm.reshape(1, GDN_DV))


MOBA_QW = 2 * MOBA_BLOCK
MOBA_ROWS_L = 16


def _moba_select_kernel(qt_ref, km_ref, qa_ref, *, nb, tq):
    nbp = -(-nb // 16) * 16
    qt = qt_ref[0]
    km = km_ref[0].astype(BF16)
    if nbp > nb:
        km = jnp.concatenate([km, jnp.zeros((nbp - nb, LANES), BF16)], axis=0)
    gate = jnp.dot(km, qt, preferred_element_type=F32)
    row = lax.broadcasted_iota(jnp.int32, (nbp, tq), 0)
    qblk = (lax.broadcasted_iota(jnp.int32, (nbp, tq), 1) + pl.program_id(2) * tq) // MOBA_BLOCK
    past = row < qblk
    g = jnp.where(past, gate, -jnp.inf)
    sel = row < 0
    for _ in range(MOBA_TOPK):
        m = jnp.max(g, axis=0, keepdims=True)
        idx = jnp.min(jnp.where(g == m, row, nbp), axis=0, keepdims=True)
        hit = row == idx
        sel = sel | (hit & past)
        g = jnp.where(hit, -jnp.inf, g)
    qa_ref[0, 0:LANES, :] = qt
    qa_ref[0, LANES:LANES + nbp, :] = jnp.where(sel, 0.0, MASK_NEG).astype(BF16)
    if nbp < LANES:
        qa_ref[0, LANES + nbp:, :] = jnp.zeros((LANES - nbp, tq), BF16)


def _moba_select(qt, kmean, tq):
    b, _, t = qt.shape
    nb = t // MOBA_BLOCK
    assert nb <= LANES
    return pl.pallas_call(
        functools.partial(_moba_select_kernel, nb=nb, tq=tq),
        grid=(b, MOBA_HEADS, t // tq),
        in_specs=[pl.BlockSpec((1, LANES, tq), lambda bi, hi, qi: (bi, hi, qi)),
                  pl.BlockSpec((1, nb, LANES), lambda bi, hi, qi: (bi, 0, hi))],
        out_specs=pl.BlockSpec((1, 2 * LANES, tq), lambda bi, hi, qi: (bi, hi, qi)),
        out_shape=jax.ShapeDtypeStruct((b, 2 * MOBA_W, t), BF16),
        compiler_params=_params(("parallel", "parallel", "parallel")),
        name="moba_select",
    )(qt, kmean)


def _moba_kernel(qa_ref, ka_ref, vt_ref, o_ref, acc_ref, m_ref, sa_ref, sb_ref, pa_ref, pb_ref, aa_ref, ab_ref,
                 *, nb, group):
    i2 = pl.program_id(2)
    BLK = MOBA_BLOCK

    def pv(p, start, width):
        lhs = jnp.concatenate([vt_ref[0, :, pl.ds(start, width)], jnp.ones((MOBA_ROWS_L, width), BF16)], axis=0)
        return jnp.dot(lhs, p, preferred_element_type=F32)

    key_i = lax.broadcasted_iota(jnp.int32, (BLK, BLK), 0)
    qry_i = lax.broadcasted_iota(jnp.int32, (BLK, BLK), 1)
    starts = [pl.multiple_of((2 * i2 + hf) * BLK, BLK) for hf in range(2)]
    s_own = [jnp.dot(ka_ref[0, pl.ds(starts[hf], BLK), 0:LANES], qa_ref[0, 0:LANES, hf * BLK:(hf + 1) * BLK],
                     preferred_element_type=F32) for hf in range(2)]
    s_own = [jnp.where(key_i <= qry_i, s, MASK_NEG) for s in s_own]
    m_own = [jnp.max(s, axis=0, keepdims=True) for s in s_own]
    p_own = [jnp.exp2(s_own[hf] - m_own[hf]).astype(BF16) for hf in range(2)]
    for hf in range(2):
        m_ref[:, hf * BLK:(hf + 1) * BLK] = m_own[hf]
        acc_ref[:, hf * BLK:(hf + 1) * BLK] = pv(p_own[hf], starts[hf], BLK)

    span = group * BLK
    n_pairs = (2 * i2 + 2 * group) // (2 * group)
    last = nb // group - 1

    halves = [slice(hf * BLK, (hf + 1) * BLK) for hf in range(2)]

    def qk(g_idx, dst_ref):
        start = pl.multiple_of(g_idx * span, span)
        for hs in halves:
            dst_ref[:, hs] = jnp.dot(ka_ref[0, pl.ds(start, span), :], qa_ref[0, :, hs],
                                     preferred_element_type=F32)

    def softmax(src_ref, p_dst, a_dst):
        for hs in halves:
            s = src_ref[:, hs]
            m_old = m_ref[:, hs]
            m_new = jnp.maximum(m_old, jnp.max(s, axis=0, keepdims=True))
            a_dst[:, hs] = jnp.exp2(m_old - m_new)
            m_ref[:, hs] = m_new
            p_dst[:, hs] = jnp.exp2(s - m_new).astype(BF16)

    def accum(p_src, a_src, g_idx):
        start = pl.multiple_of(g_idx * span, span)
        for hs in halves:
            acc_ref[:, hs] = a_src[:, hs] * acc_ref[:, hs] + pv(p_src[:, hs], start, span)

    qk(0, sa_ref)
    pb_ref[...] = jnp.zeros_like(pb_ref)
    ab_ref[...] = jnp.ones_like(ab_ref)

    def body(jj, carry):
        qk(2 * jj + 1, sb_ref)
        accum(pb_ref, ab_ref, jnp.maximum(2 * jj - 1, 0))
        softmax(sa_ref, pa_ref, aa_ref)
        qk(jnp.minimum(2 * jj + 2, last), sa_ref)
        accum(pa_ref, aa_ref, 2 * jj)
        softmax(sb_ref, pb_ref, ab_ref)
        return carry

    lax.fori_loop(0, n_pairs, body, 0)
    accum(pb_ref, ab_ref, 2 * n_pairs - 1)
    acc = acc_ref[...]
    o = acc[0:LANES] * (1.0 / acc[LANES:LANES + 1])
    o_ref[0] = o.T.astype(o_ref.dtype)


def _moba(qaug, kaug, vt):
    b, _, t = vt.shape
    nb = t // MOBA_BLOCK
    group = 4 if nb % 8 == 0 else 2
    assert nb <= LANES and nb % (2 * group) == 0
    span = group * MOBA_BLOCK
    return pl.pallas_call(
        functools.partial(_moba_kernel, nb=nb, group=group),
        grid=(b, MOBA_HEADS, t // MOBA_QW),
        in_specs=[pl.BlockSpec((1, 2 * LANES, MOBA_QW), lambda bi, hi, qi: (bi, hi, qi)),
                  pl.BlockSpec((1, t, 2 * LANES), lambda bi, hi, qi: (bi, 0, hi)),
                  pl.BlockSpec((1, LANES, t), lambda bi, hi, qi: (bi, hi, 0))],
        out_specs=pl.BlockSpec((1, MOBA_QW, LANES), lambda bi, hi, qi: (bi, qi, hi)),
        out_shape=jax.ShapeDtypeStruct((b, t, MOBA_W), BF16),
        scratch_shapes=[pltpu.VMEM((LANES + MOBA_ROWS_L, MOBA_QW), F32),
                        pltpu.VMEM((1, MOBA_QW), F32),
                        pltpu.VMEM((span, MOBA_QW), F32), pltpu.VMEM((span, MOBA_QW), F32),
                        pltpu.VMEM((span, MOBA_QW), BF16), pltpu.VMEM((span, MOBA_QW), BF16),
                        pltpu.VMEM((1, MOBA_QW), F32), pltpu.VMEM((1, MOBA_QW), F32)],
        compiler_params=_params(("parallel", "parallel", "arbitrary")),
        name="moba",
    )(qaug, kaug, vt)


def _oproj_kernel(og_ref, om_ref, wa_ref, wb_ref, x_ref, nw_ref, h_ref, hn_ref):
    y = jnp.dot(og_ref[...], wa_ref[...], preferred_element_type=F32)
    y = y + jnp.dot(om_ref[...], wb_ref[...], preferred_element_type=F32)
    h = x_ref[...] + y
    h_ref[...] = h
    hn = h * lax.rsqrt(jnp.mean(h * h, axis=-1, keepdims=True) + RMS_EPS)
    hn_ref[...] = (hn * nw_ref[...]).astype(hn_ref.dtype)


def _oproj(og, om, wa, wb, x, norm_w, tm):
    m, d = x.shape
    ka, kb = og.shape[1], om.shape[1]
    row = lambda w: pl.BlockSpec((tm, w), lambda i: (i, 0))
    full = lambda r, c: pl.BlockSpec((r, c), lambda i: (0, 0))
    return pl.pallas_call(
        _oproj_kernel,
        grid=(m // tm,),
        in_specs=[row(ka), row(kb), full(ka, d), full(kb, d), row(d), full(1, d)],
        out_specs=[row(d), row(d)],
        out_shape=[jax.ShapeDtypeStruct((m, d), F32), jax.ShapeDtypeStruct((m, d), BF16)],
        compiler_params=_params(("parallel",)),
        name="oproj",
    )(og, om, wa, wb, x, norm_w.reshape(1, d))


def _ffn_kernel(hn_ref, wg_ref, wu_ref, wd_ref, h_ref, nw_ref, h2_ref, hn2_ref, acc_ref):
    f = pl.program_id(1)

    @pl.when(f == 0)
    def _():
        acc_ref[...] = jnp.zeros_like(acc_ref)

    hn = hn_ref[...]
    g = jnp.dot(hn, wg_ref[...], preferred_element_type=F32)
    u = jnp.dot(hn, wu_ref[...], preferred_element_type=F32)
    a = (g * _sigmoid(g) * u).astype(BF16)
    acc_ref[...] += jnp.dot(a, wd_ref[...], preferred_element_type=F32)

    @pl.when(f == pl.num_programs(1) - 1)
    def _():
        h2 = h_ref[...] + acc_ref[...]
        h2_ref[...] = h2
        n = h2 * lax.rsqrt(jnp.mean(h2 * h2, axis=-1, keepdims=True) + RMS_EPS)
        hn2_ref[...] = (n * nw_ref[...]).astype(hn2_ref.dtype)


def _ffn(hn, wg, wu, wd, h, norm_w, tm, tf):
    m, d = h.shape
    dff = wg.shape[1]
    row = pl.BlockSpec((tm, d), lambda i, f: (i, 0))
    return pl.pallas_call(
        _ffn_kernel,
        grid=(m // tm, dff // tf),
        in_specs=[row, pl.BlockSpec((d, tf), lambda i, f: (0, f)), pl.BlockSpec((d, tf), lambda i, f: (0, f)),
                  pl.BlockSpec((tf, d), lambda i, f: (f, 0)), row, pl.BlockSpec((1, d), lambda i, f: (0, 0))],
        out_specs=[row, row],
        out_shape=[jax.ShapeDtypeStruct((m, d), F32), jax.ShapeDtypeStruct((m, d), BF16)],
        scratch_shapes=[pltpu.VMEM((tm, d), F32)],
        compiler_params=_params(("parallel", "arbitrary")),
        name="ffn",
    )(hn, wg, wu, wd, h, norm_w.reshape(1, d))


def _ple_kernel(hn_ref, wg_ref, p_ref, wp_ref, h_ref, o_ref):
    gate = _sigmoid(jnp.dot(hn_ref[...], wg_ref[...], preferred_element_type=F32))
    proj = jnp.dot(p_ref[...].astype(BF16), wp_ref[...], preferred_element_type=F32)
    o_ref[...] = h_ref[...] + gate * proj


def _ple(hn, wg, p, wp, h, tm, tn):
    m, d = h.shape
    kp = p.shape[1]
    return pl.pallas_call(
        _ple_kernel,
        grid=(m // tm, d // tn),
        in_specs=[pl.BlockSpec((tm, d), lambda i, j: (i, 0)), pl.BlockSpec((d, tn), lambda i, j: (0, j)),
                  pl.BlockSpec((tm, kp), lambda i, j: (i, 0)), pl.BlockSpec((kp, tn), lambda i, j: (0, j)),
                  pl.BlockSpec((tm, tn), lambda i, j: (i, j))],
        out_specs=pl.BlockSpec((tm, tn), lambda i, j: (i, j)),
        out_shape=jax.ShapeDtypeStruct((m, d), F32),
        compiler_params=_params(("parallel", "arbitrary")),
        name="ple",
    )(hn, wg, p, wp, h)


def _layer(h, p, attn_norm, w_in, conv_w, a_log, dt_bias, gdn_norm, q_norm, k_norm, w_o, ffn_norm,
           w_gate, w_up, w_down, ple_norm, w_ple_gate, w_ple_proj):
    b, t, d = h.shape
    m = b * t
    x2 = h.reshape(m, d)
    tm = min(1024, m)

    o_ba = 4 * GDN_W
    o_mq = o_ba + 2 * GDN_HEADS
    w_in16 = w_in.astype(BF16)
    w_gdn = w_in16[:, :o_ba]
    w_ba = jnp.pad(w_in16[:, o_ba:o_mq], ((0, 0), (0, LANES - 2 * GDN_HEADS)))
    w_mq = w_in16[:, o_mq:o_mq + MOBA_W]
    w_mk = w_in16[:, o_mq + MOBA_W:o_mq + 2 * MOBA_W]
    w_mv = w_in16[:, o_mq + 2 * MOBA_W:o_mq + 3 * MOBA_W]

    xn = _rmsnorm(x2, attn_norm, min(512, m))
    gproj = _proj(xn, w_gdn, F32, tm, 512, "proj_gdn")
    ba = _proj(xn, w_ba, F32, tm, LANES, "proj_ba")
    mq = _proj_moba_q(xn, w_mq, q_norm, tm, 512)
    kaug, kmean = _proj_moba_k(xn, w_mk, k_norm, tm, 512, t // MOBA_BLOCK)
    mv = _proj(xn, w_mv, BF16, tm, 512, "proj_moba_v")

    o_gdn = _gdn_all(gproj.reshape(b, t, 4 * GDN_W), ba.reshape(b, t, LANES), conv_w.T, a_log, dt_bias,
                     gdn_norm, min(256, t))
    qaug = _moba_select(mq.reshape(b, t, MOBA_W).swapaxes(1, 2), kmean.reshape(b, t // MOBA_BLOCK, MOBA_W),
                        min(2048, t))
    o_moba = _moba(qaug, kaug.reshape(b, t, 2 * MOBA_W), mv.reshape(b, t, MOBA_W).swapaxes(1, 2))

    w_o16 = w_o.astype(BF16)
    h1, hn = _oproj(o_gdn.reshape(m, GDN_W), o_moba.reshape(m, MOBA_W), w_o16[:GDN_W], w_o16[GDN_W:],
                    x2, ffn_norm, min(256, m))
    h2, hn2 = _ffn(hn, w_gate.astype(BF16), w_up.astype(BF16), w_down.astype(BF16), h1, ple_norm,
                   min(512, m), 512)
    h3 = _ple(hn2, w_ple_gate.astype(BF16), p.reshape(m, PLE_DIM), w_ple_proj.astype(BF16), h2, tm, 512)
    return h3.reshape(b, t, d)


def kernel(x, p, attn_norm, w_in, conv_w, A_log, dt_bias, gdn_norm, q_norm, k_norm, w_o, ffn_norm,
           w_gate, w_up, w_down, ple_norm, w_ple_gate, w_ple_proj):
    h = x
    for i in range(p.shape[0]):
        h = _layer(h, p[i], attn_norm[i], w_in[i], conv_w[i], A_log[i], dt_bias[i], gdn_norm[i],
                   q_norm[i], k_norm[i], w_o[i], ffn_norm[i], w_gate[i], w_up[i], w_down[i],
                   ple_norm[i], w_ple_gate[i], w_ple_proj[i])
    return h
```

```python
import functools

import jax
import jax.numpy as jnp
from jax import lax
from jax.experimental import pallas as pl
from jax.experimental.pallas import tpu as pltpu

D_MODEL = 2048
PLE_DIM = 256
GDN_HEADS = 8
GDN_DK = 128
GDN_DV = 128
GDN_CONV = 4
GDN_CHUNK = 64
MOBA_HEADS = 8
MOBA_DH = 128
MOBA_BLOCK = 256
MOBA_TOPK = 3
RMS_EPS = 1e-6
GDN_W = GDN_HEADS * GDN_DK
MOBA_W = MOBA_HEADS * MOBA_DH
LANES = 128
LOG2E = 1.4426950408889634
MASK_NEG = -1e30

F32 = jnp.float32
BF16 = jnp.bfloat16

VMEM_LIMIT = 56 * 1024 * 1024


def _params(sem):
    return pltpu.CompilerParams(dimension_semantics=sem, vmem_limit_bytes=VMEM_LIMIT)


def _sigmoid(x):
    return 1.0 / (1.0 + jnp.exp(-x))


def _dot_t(a, b):
    return lax.dot_general(a, b, (((1,), (1,)), ((), ())), preferred_element_type=F32)


def _bdot(a, b):
    return jnp.dot(a.astype(BF16), b.astype(BF16), preferred_element_type=F32)


def _rmsnorm_kernel(x_ref, w_ref, o_ref):
    x = x_ref[...]
    y = x * lax.rsqrt(jnp.mean(x * x, axis=-1, keepdims=True) + RMS_EPS)
    o_ref[...] = (y * w_ref[...]).astype(o_ref.dtype)


def _rmsnorm(x, w, tm):
    m, d = x.shape
    return pl.pallas_call(
        _rmsnorm_kernel,
        grid=(m // tm,),
        in_specs=[pl.BlockSpec((tm, d), lambda i: (i, 0)), pl.BlockSpec((1, d), lambda i: (0, 0))],
        out_specs=pl.BlockSpec((tm, d), lambda i: (i, 0)),
        out_shape=jax.ShapeDtypeStruct((m, d), BF16),
        compiler_params=_params(("parallel",)),
        name="rmsnorm_cast",
    )(x, w.reshape(1, d))


def _proj_kernel(x_ref, w_ref, o_ref):
    o_ref[...] = jnp.dot(x_ref[...], w_ref[...], preferred_element_type=F32).astype(o_ref.dtype)


def _proj(x, w, out_dtype, tm, tn, name):
    m, k = x.shape
    n = w.shape[1]
    return pl.pallas_call(
        _proj_kernel,
        grid=(m // tm, n // tn),
        in_specs=[pl.BlockSpec((tm, k), lambda i, j: (i, 0)), pl.BlockSpec((k, tn), lambda i, j: (0, j))],
        out_specs=pl.BlockSpec((tm, tn), lambda i, j: (i, j)),
        out_shape=jax.ShapeDtypeStruct((m, n), out_dtype),
        compiler_params=_params(("parallel", "arbitrary")),
        name=name,
    )(x, w)


def _head_rmsnorm(y, gain, scale):
    outs = []
    for h in range(y.shape[1] // LANES):
        yh = y[:, h * LANES:(h + 1) * LANES]
        r = lax.rsqrt(jnp.mean(yh * yh, axis=-1, keepdims=True) + RMS_EPS)
        outs.append(yh * r * gain[:, h * LANES:(h + 1) * LANES] * scale)
    return jnp.concatenate(outs, axis=1)


def _proj_qnorm_kernel(x_ref, w_ref, g_ref, o_ref, *, scale):
    y = jnp.dot(x_ref[...], w_ref[...], preferred_element_type=F32)
    o_ref[...] = _head_rmsnorm(y, g_ref[...], scale).astype(o_ref.dtype)


def _proj_knorm_kernel(x_ref, w_ref, g_ref, o_ref, km_ref, *, nb_seq):
    y = jnp.dot(x_ref[...], w_ref[...], preferred_element_type=F32)
    yn = _head_rmsnorm(y, g_ref[...], 1.0)
    tm = yn.shape[0]
    row = lax.broadcasted_iota(jnp.int32, (tm, LANES), 0) + pl.program_id(0) * tm
    lane = lax.broadcasted_iota(jnp.int32, (tm, LANES), 1)
    onehot = jnp.where(lane == lax.rem(row // MOBA_BLOCK, nb_seq), 1.0, 0.0).astype(o_ref.dtype)
    yb = yn.astype(o_ref.dtype)
    parts = []
    for h in range(yn.shape[1] // LANES):
        parts += [yb[:, h * LANES:(h + 1) * LANES], onehot]
    o_ref[...] = jnp.concatenate(parts, axis=1)
    for r in range(tm // MOBA_BLOCK):
        blk = yn[r * MOBA_BLOCK:(r + 1) * MOBA_BLOCK]
        km_ref[r] = jnp.mean(blk, axis=0, keepdims=True)


def _proj_moba_q(x, w, gain, tm, tn):
    m, k = x.shape
    n = w.shape[1]
    g = jnp.tile(gain.reshape(1, MOBA_DH), (1, n // MOBA_DH))
    return pl.pallas_call(
        functools.partial(_proj_qnorm_kernel, scale=MOBA_DH ** -0.5 * LOG2E),
        grid=(m // tm, n // tn),
        in_specs=[pl.BlockSpec((tm, k), lambda i, j: (i, 0)), pl.BlockSpec((k, tn), lambda i, j: (0, j)),
                  pl.BlockSpec((1, tn), lambda i, j: (0, j))],
        out_specs=pl.BlockSpec((tm, tn), lambda i, j: (i, j)),
        out_shape=jax.ShapeDtypeStruct((m, n), BF16),
        compiler_params=_params(("parallel", "arbitrary")),
        name="proj_moba_q",
    )(x, w, g)


def _proj_moba_k(x, w, gain, tm, tn, nb_seq):
    m, k = x.shape
    n = w.shape[1]
    g = jnp.tile(gain.reshape(1, MOBA_DH), (1, n // MOBA_DH))
    nb = tm // MOBA_BLOCK
    return pl.pallas_call(
        functools.partial(_proj_knorm_kernel, nb_seq=nb_seq),
        grid=(m // tm, n // tn),
        in_specs=[pl.BlockSpec((tm, k), lambda i, j: (i, 0)), pl.BlockSpec((k, tn), lambda i, j: (0, j)),
                  pl.BlockSpec((1, tn), lambda i, j: (0, j))],
        out_specs=[pl.BlockSpec((tm, 2 * tn), lambda i, j: (i, j)),
                   pl.BlockSpec((nb, 1, tn), lambda i, j: (i, 0, j))],
        out_shape=[jax.ShapeDtypeStruct((m, 2 * n), BF16),
                   jax.ShapeDtypeStruct((m // MOBA_BLOCK, 1, n), F32)],
        compiler_params=_params(("parallel", "arbitrary")),
        name="proj_moba_k",
    )(x, w, g)


def _gdn_all_kernel(x_ref, ba_ref, cw_ref, alog_ref, dtb_ref, gn_ref, o_ref, s_ref, tail_ref, qkv_s, *, tb):
    t = pl.program_id(1)
    C = GDN_CHUNK
    nc = tb // C
    nh = GDN_HEADS

    @pl.when(t == 0)
    def _():
        s_ref[...] = jnp.zeros_like(s_ref)
        tail_ref[...] = jnp.zeros_like(tail_ref)

    def conv_group(gi, carry):
        off = pl.multiple_of(gi * LANES, LANES)
        x = x_ref[0, :, pl.ds(off, LANES)]
        xp = jnp.concatenate([tail_ref[:, pl.ds(off, LANES)], x], axis=0)
        w = cw_ref[:, pl.ds(off, LANES)]
        y = x * w[GDN_CONV - 1:GDN_CONV]
        for j in range(GDN_CONV - 1):
            y = y + pltpu.roll(xp, GDN_CONV - 1 - j, axis=0)[8:8 + tb] * w[j:j + 1]
        tail_ref[:, pl.ds(off, LANES)] = x[tb - 8:tb]
        y = y * _sigmoid(y)
        r = lax.rsqrt(jnp.sum(y * y, axis=-1, keepdims=True) + 1e-6)
        fac = jnp.where(gi < nh, r * (GDN_DK ** -0.5), jnp.where(gi < 2 * nh, r, jnp.ones_like(r)))
        qkv_s[:, pl.ds(off, LANES)] = y * fac
        return carry

    lax.fori_loop(0, 3 * nh, conv_group, 0)

    ba = ba_ref[0]
    beta_all = _sigmoid(ba)
    sp_in = ba + dtb_ref[...]
    softplus = jnp.maximum(sp_in, 0.0) + jnp.log1p(jnp.exp(-jnp.abs(sp_in)))
    gc = -(jnp.exp(alog_ref[...]) * softplus)
    row_in_chunk = lax.broadcasted_iota(jnp.int32, (tb, LANES), 0) & (C - 1)
    sh = 1
    while sh < C:
        gc = gc + jnp.where(row_in_chunk >= sh, pltpu.roll(gc, sh, axis=0), 0.0)
        sh *= 2
    gl_rows = [gc[(c + 1) * C - 1:(c + 1) * C] for c in range(nc)]
    gl_b = jnp.concatenate([jnp.broadcast_to(g, (C, LANES)) for g in gl_rows], axis=0)
    egc = jnp.exp(gc)
    kdsc = jnp.exp(gl_b - gc)
    egl = [jnp.exp(g) for g in gl_rows]
    gc_t = gc.T

    ri = lax.broadcasted_iota(jnp.int32, (C, C), 0)
    ci = lax.broadcasted_iota(jnp.int32, (C, C), 1)
    tril = ri >= ci
    strict = ri > ci
    eye_f = (ri == ci).astype(F32)
    lvl_masks = []
    s = 1
    while s < C:
        sh2 = s.bit_length()
        lvl_masks.append((ri >> sh2 == ci >> sh2) & ((ri & (2 * s - 1)) >= s) & ((ci & (2 * s - 1)) < s))
        s *= 2
    gn = gn_ref[...]

    states = [s_ref[h] for h in range(nh)]
    for c in range(nc):
        r0 = c * C
        heads = range(nh)
        rows = slice(r0, r0 + C)
        grp = lambda g, h: slice((g * nh + h) * LANES, (g * nh + h + 1) * LANES)
        qc = [qkv_s[rows, grp(0, h)] for h in heads]
        kc = [qkv_s[rows, grp(1, h)] for h in heads]
        vc = [qkv_s[rows, grp(2, h)] for h in heads]
        bcol = [beta_all[rows, h:h + 1] for h in heads]
        ecol = [egc[rows, nh + h:nh + h + 1] for h in heads]
        decay = [jnp.exp(jnp.where(tril, gc[rows, nh + h:nh + h + 1] - gc_t[nh + h:nh + h + 1, rows], -jnp.inf))
                 for h in heads]
        k_beta = [kc[h] * bcol[h] for h in heads]
        st = [_dot_t(jnp.concatenate([k_beta[h], qc[h]], axis=0).astype(BF16), kc[h].astype(BF16))
              for h in heads]
        a_mat = [jnp.where(strict, st[h][:C] * decay[h], 0.0) for h in heads]
        qk = [jnp.where(tril, st[h][C:] * decay[h], 0.0) for h in heads]
        t_inv = [eye_f - jnp.where(lvl_masks[0], a_mat[h], 0.0) for h in heads]
        for msk in lvl_masks[1:]:
            x_mid = [_bdot(jnp.where(msk, a_mat[h], 0.0), t_inv[h]) for h in heads]
            y_mid = [_bdot(t_inv[h], x_mid[h]) for h in heads]
            t_inv = [t_inv[h] - y_mid[h] for h in heads]
        uw = [_bdot(t_inv[h], jnp.concatenate([vc[h] * bcol[h], k_beta[h] * ecol[h]], axis=1))
              for h in heads]
        wq = [_bdot(jnp.concatenate([uw[h][:, LANES:], qc[h] * ecol[h]], axis=0), states[h])
              for h in heads]
        v_new = [uw[h][:, :LANES] - wq[h][:C] for h in heads]
        o = [wq[h][C:] + _bdot(qk[h], v_new[h]) for h in heads]
        k_dec = [kc[h] * kdsc[rows, nh + h:nh + h + 1] for h in heads]
        states = [states[h] * egl[c][:, nh + h:nh + h + 1] + lax.dot_general(
            k_dec[h].astype(BF16), v_new[h].astype(BF16), (((0,), (0,)), ((), ())),
            preferred_element_type=F32) for h in heads]
        for h in heads:
            on = o[h] * lax.rsqrt(jnp.mean(o[h] * o[h], axis=-1, keepdims=True) + RMS_EPS) * gn
            zc = x_ref[0, rows, grp(3, h)]
            o_ref[0, rows, grp(0, h)] = (on * (zc * _sigmoid(zc))).astype(o_ref.dtype)
    for h in range(nh):
        s_ref[h] = states[h]


def _gdn_all(proj, ba, conv_wt, a_log, dt_bias, gnorm, tb):
    b, t, width = proj.shape
    nh = GDN_HEADS
    lane_vec = lambda v: jnp.pad(v.reshape(1, nh), ((0, 0), (nh, LANES - 2 * nh)))
    full = lambda shape: pl.BlockSpec(shape, lambda bi, ti: (0,) * len(shape))
    return pl.pallas_call(
        functools.partial(_gdn_all_kernel, tb=tb),
        grid=(b, t // tb),
        in_specs=[pl.BlockSpec((1, tb, width), lambda bi, ti: (bi, ti, 0)),
                  pl.BlockSpec((1, tb, LANES), lambda bi, ti: (bi, ti, 0)),
                  full((GDN_CONV, 3 * GDN_W)), full((1, LANES)), full((1, LANES)), full((1, GDN_DV))],
        out_specs=pl.BlockSpec((1, tb, GDN_W), lambda bi, ti: (bi, ti, 0)),
        out_shape=jax.ShapeDtypeStruct((b, t, GDN_W), BF16),
        scratch_shapes=[pltpu.VMEM((nh, GDN_DK, GDN_DV), F32), pltpu.VMEM((8, 3 * GDN_W), F32),
                        pltpu.VMEM((tb, 3 * GDN_W), F32)],
        compiler_params=_params(("parallel", "arbitrary")),
        name="gdn",
    )(proj, ba, conv_wt, lane_vec(a_log), lane_vec(dt_bias), gnorm.reshape(1, GDN_DV))


MOBA_QW = 2 * MOBA_BLOCK
MOBA_ROWS_L = 16


def _moba_select_kernel(qt_ref, km_ref, qa_ref, *, nb, tq):
    nbp = -(-nb // 16) * 16
    qt = qt_ref[0]
    km = km_ref[0].astype(BF16)
    if nbp > nb:
        km = jnp.concatenate([km, jnp.zeros((nbp - nb, LANES), BF16)], axis=0)
    gate = jnp.dot(km, qt, preferred_element_type=F32)
    row = lax.broadcasted_iota(jnp.int32, (nbp, tq), 0)
    qblk = (lax.broadcasted_iota(jnp.int32, (nbp, tq), 1) + pl.program_id(2) * tq) // MOBA_BLOCK
    past = row < qblk
    g = jnp.where(past, gate, -jnp.inf)
    sel = row < 0
    for _ in range(MOBA_TOPK):
        m = jnp.max(g, axis=0, keepdims=True)
        idx = jnp.min(jnp.where(g == m, row, nbp), axis=0, keepdims=True)
        hit = row == idx
        sel = sel | (hit & past)
        g = jnp.where(hit, -jnp.inf, g)
    qa_ref[0, 0:LANES, :] = qt
    qa_ref[0, LANES:LANES + nbp, :] = jnp.where(sel, 0.0, MASK_NEG).astype(BF16)
    if nbp < LANES:
        qa_ref[0, LANES + nbp:, :] = jnp.zeros((LANES - nbp, tq), BF16)


def _moba_select(qt, kmean, tq):
    b, _, t = qt.shape
    nb = t // MOBA_BLOCK
    assert nb <= LANES
    return pl.pallas_call(
        functools.partial(_moba_select_kernel, nb=nb, tq=tq),
        grid=(b, MOBA_HEADS, t // tq),
        in_specs=[pl.BlockSpec((1, LANES, tq), lambda bi, hi, qi: (bi, hi, qi)),
                  pl.BlockSpec((1, nb, LANES), lambda bi, hi, qi: (bi, 0, hi))],
        out_specs=pl.BlockSpec((1, 2 * LANES, tq), lambda bi, hi, qi: (bi, hi, qi)),
        out_shape=jax.ShapeDtypeStruct((b, 2 * MOBA_W, t), BF16),
        compiler_params=_params(("parallel", "parallel", "parallel")),
        name="moba_select",
    )(qt, kmean)


def _moba_kernel(qa_ref, ka_ref, vt_ref, o_ref, acc_ref, m_ref, sa_ref, sb_ref, xa_ref, xb_ref, *, nb, group):
    i2 = pl.program_id(2)
    BLK = MOBA_BLOCK

    def pv(p, start, width):
        lhs = jnp.concatenate([vt_ref[0, :, pl.ds(start, width)], jnp.ones((MOBA_ROWS_L, width), BF16)], axis=0)
        return jnp.dot(lhs, p, preferred_element_type=F32)

    key_i = lax.broadcasted_iota(jnp.int32, (BLK, BLK), 0)
    qry_i = lax.broadcasted_iota(jnp.int32, (BLK, BLK), 1)
    starts = [pl.multiple_of((2 * i2 + hf) * BLK, BLK) for hf in range(2)]
    s_own = [jnp.dot(ka_ref[0, pl.ds(starts[hf], BLK), 0:LANES], qa_ref[0, 0:LANES, hf * BLK:(hf + 1) * BLK],
                     preferred_element_type=F32) for hf in range(2)]
    s_own = [jnp.where(key_i <= qry_i, s, MASK_NEG) for s in s_own]
    m_own = [jnp.max(s, axis=0, keepdims=True) for s in s_own]
    p_own = [jnp.exp2(s_own[hf] - m_own[hf]).astype(BF16) for hf in range(2)]
    for hf in range(2):
        m_ref[:, hf * BLK:(hf + 1) * BLK] = m_own[hf]
        acc_ref[:, hf * BLK:(hf + 1) * BLK] = pv(p_own[hf], starts[hf], BLK)

    span = group * BLK
    n_pairs = (2 * i2 + 2 * group) // (2 * group)
    last = nb // group - 1
    halves = [slice(hf * BLK, (hf + 1) * BLK) for hf in range(2)]

    def qk(g_idx, dst_ref, mx_dst, hs):
        start = pl.multiple_of(g_idx * span, span)
        s = jnp.dot(ka_ref[0, pl.ds(start, span), :], qa_ref[0, :, hs], preferred_element_type=F32)
        dst_ref[:, hs] = s
        mx_dst[:, hs] = jnp.max(s, axis=0, keepdims=True)

    def softmax_pv(src_ref, mx_src, g_idx, hs):
        m_old = m_ref[:, hs]
        m_new = jnp.maximum(m_old, mx_src[:, hs])
        alpha = jnp.exp2(m_old - m_new)
        m_ref[:, hs] = m_new
        p = jnp.exp2(src_ref[:, hs] - m_new).astype(BF16)
        acc_ref[:, hs] = alpha * acc_ref[:, hs] + pv(p, pl.multiple_of(g_idx * span, span), span)

    for hs in halves:
        qk(0, sa_ref, xa_ref, hs)

    def body(jj, carry):
        for hs in halves:
            qk(2 * jj + 1, sb_ref, xb_ref, hs)
            softmax_pv(sa_ref, xa_ref, 2 * jj, hs)
        for hs in halves:
            qk(jnp.minimum(2 * jj + 2, last), sa_ref, xa_ref, hs)
            softmax_pv(sb_ref, xb_ref, 2 * jj + 1, hs)
        return carry

    lax.fori_loop(0, n_pairs, body, 0)
    acc = acc_ref[...]
    o = acc[0:LANES] * (1.0 / acc[LANES:LANES + 1])
    o_ref[0] = o.T.astype(o_ref.dtype)


def _moba(qaug, kaug, vt):
    b, _, t = vt.shape
    nb = t // MOBA_BLOCK
    group = 4 if nb % 8 == 0 else 2
    assert nb <= LANES and nb % (2 * group) == 0
    span = group * MOBA_BLOCK
    return pl.pallas_call(
        functools.partial(_moba_kernel, nb=nb, group=group),
        grid=(b, MOBA_HEADS, t // MOBA_QW),
        in_specs=[pl.BlockSpec((1, 2 * LANES, MOBA_QW), lambda bi, hi, qi: (bi, hi, qi)),
                  pl.BlockSpec((1, t, 2 * LANES), lambda bi, hi, qi: (bi, 0, hi)),
                  pl.BlockSpec((1, LANES, t), lambda bi, hi, qi: (bi, hi, 0))],
        out_specs=pl.BlockSpec((1, MOBA_QW, LANES), lambda bi, hi, qi: (bi, qi, hi)),
        out_shape=jax.ShapeDtypeStruct((b, t, MOBA_W), BF16),
        scratch_shapes=[pltpu.VMEM((LANES + MOBA_ROWS_L, MOBA_QW), F32),
                        pltpu.VMEM((1, MOBA_QW), F32),
                        pltpu.VMEM((span, MOBA_QW), F32), pltpu.VMEM((span, MOBA_QW), F32),
                        pltpu.VMEM((1, MOBA_QW), F32), pltpu.VMEM((1, MOBA_QW), F32)],
        compiler_params=_params(("parallel", "parallel", "arbitrary")),
        name="moba",
    )(qaug, kaug, vt)


def _oproj_kernel(og_ref, om_ref, wa_ref, wb_ref, x_ref, nw_ref, h_ref, hn_ref):
    y = jnp.dot(og_ref[...], wa_ref[...], preferred_element_type=F32)
    y = y + jnp.dot(om_ref[...], wb_ref[...], preferred_element_type=F32)
    h = x_ref[...] + y
    h_ref[...] = h
    hn = h * lax.rsqrt(jnp.mean(h * h, axis=-1, keepdims=True) + RMS_EPS)
    hn_ref[...] = (hn * nw_ref[...]).astype(hn_ref.dtype)


def _oproj(og, om, wa, wb, x, norm_w, tm):
    m, d = x.shape
    ka, kb = og.shape[1], om.shape[1]
    row = lambda w: pl.BlockSpec((tm, w), lambda i: (i, 0))
    full = lambda r, c: pl.BlockSpec((r, c), lambda i: (0, 0))
    return pl.pallas_call(
        _oproj_kernel,
        grid=(m // tm,),
        in_specs=[row(ka), row(kb), full(ka, d), full(kb, d), row(d), full(1, d)],
        out_specs=[row(d), row(d)],
        out_shape=[jax.ShapeDtypeStruct((m, d), F32), jax.ShapeDtypeStruct((m, d), BF16)],
        compiler_params=_params(("parallel",)),
        name="oproj",
    )(og, om, wa, wb, x, norm_w.reshape(1, d))


def _ffn_kernel(hn_ref, wg_ref, wu_ref, wd_ref, h_ref, nw_ref, h2_ref, hn2_ref, acc_ref):
    f = pl.program_id(1)

    @pl.when(f == 0)
    def _():
        acc_ref[...] = jnp.zeros_like(acc_ref)

    hn = hn_ref[...]
    g = jnp.dot(hn, wg_ref[...], preferred_element_type=F32)
    u = jnp.dot(hn, wu_ref[...], preferred_element_type=F32)
    a = (g * _sigmoid(g) * u).astype(BF16)
    acc_ref[...] += jnp.dot(a, wd_ref[...], preferred_element_type=F32)

    @pl.when(f == pl.num_programs(1) - 1)
    def _():
        h2 = h_ref[...] + acc_ref[...]
        h2_ref[...] = h2
        n = h2 * lax.rsqrt(jnp.mean(h2 * h2, axis=-1, keepdims=True) + RMS_EPS)
        hn2_ref[...] = (n * nw_ref[...]).astype(hn2_ref.dtype)


def _ffn(hn, wg, wu, wd, h, norm_w, tm, tf):
    m, d = h.shape
    dff = wg.shape[1]
    row = pl.BlockSpec((tm, d), lambda i, f: (i, 0))
    return pl.pallas_call(
        _ffn_kernel,
        grid=(m // tm, dff // tf),
        in_specs=[row, pl.BlockSpec((d, tf), lambda i, f: (0, f)), pl.BlockSpec((d, tf), lambda i, f: (0, f)),
                  pl.BlockSpec((tf, d), lambda i, f: (f, 0)), row, pl.BlockSpec((1, d), lambda i, f: (0, 0))],
        out_specs=[row, row],
        out_shape=[jax.ShapeDtypeStruct((m, d), F32), jax.ShapeDtypeStruct((m, d), BF16)],
        scratch_shapes=[pltpu.VMEM((tm, d), F32)],
        compiler_params=_params(("parallel", "arbitrary")),
        name="ffn",
    )(hn, wg, wu, wd, h, norm_w.reshape(1, d))


def _ple_kernel(hn_ref, wg_ref, p_ref, wp_ref, h_ref, o_ref):
    gate = _sigmoid(jnp.dot(hn_ref[...], wg_ref[...], preferred_element_type=F32))
    proj = jnp.dot(p_ref[...].astype(BF16), wp_ref[...], preferred_element_type=F32)
    o_ref[...] = h_ref[...] + gate * proj


def _ple(hn, wg, p, wp, h, tm, tn):
    m, d = h.shape
    kp = p.shape[1]
    return pl.pallas_call(
        _ple_kernel,
        grid=(m // tm, d // tn),
        in_specs=[pl.BlockSpec((tm, d), lambda i, j: (i, 0)), pl.BlockSpec((d, tn), lambda i, j: (0, j)),
                  pl.BlockSpec((tm, kp), lambda i, j: (i, 0)), pl.BlockSpec((kp, tn), lambda i, j: (0, j)),
                  pl.BlockSpec((tm, tn), lambda i, j: (i, j))],
        out_specs=pl.BlockSpec((tm, tn), lambda i, j: (i, j)),
        out_shape=jax.ShapeDtypeStruct((m, d), F32),
        compiler_params=_params(("parallel", "arbitrary")),
        name="ple",
    )(hn, wg, p, wp, h)


def _layer(h, p, attn_norm, w_in, conv_w, a_log, dt_bias, gdn_norm, q_norm, k_norm, w_o, ffn_norm,
           w_gate, w_up, w_down, ple_norm, w_ple_gate, w_ple_proj):
    b, t, d = h.shape
    m = b * t
    x2 = h.reshape(m, d)
    tm = min(1024, m)

    o_ba = 4 * GDN_W
    o_mq = o_ba + 2 * GDN_HEADS
    w_in16 = w_in.astype(BF16)
    w_gdn = w_in16[:, :o_ba]
    w_ba = jnp.pad(w_in16[:, o_ba:o_mq], ((0, 0), (0, LANES - 2 * GDN_HEADS)))
    w_mq = w_in16[:, o_mq:o_mq + MOBA_W]
    w_mk = w_in16[:, o_mq + MOBA_W:o_mq + 2 * MOBA_W]
    w_mv = w_in16[:, o_mq + 2 * MOBA_W:o_mq + 3 * MOBA_W]

    xn = _rmsnorm(x2, attn_norm, min(512, m))
    gproj = _proj(xn, w_gdn, F32, tm, 512, "proj_gdn")
    ba = _proj(xn, w_ba, F32, tm, LANES, "proj_ba")
    mq = _proj_moba_q(xn, w_mq, q_norm, tm, 512)
    kaug, kmean = _proj_moba_k(xn, w_mk, k_norm, tm, 512, t // MOBA_BLOCK)
    mv = _proj(xn, w_mv, BF16, tm, 512, "proj_moba_v")

    o_gdn = _gdn_all(gproj.reshape(b, t, 4 * GDN_W), ba.reshape(b, t, LANES), conv_w.T, a_log, dt_bias,
                     gdn_norm, min(256, t))
    qaug = _moba_select(mq.reshape(b, t, MOBA_W).swapaxes(1, 2), kmean.reshape(b, t // MOBA_BLOCK, MOBA_W),
                        min(2048, t))
    o_moba = _moba(qaug, kaug.reshape(b, t, 2 * MOBA_W), mv.reshape(b, t, MOBA_W).swapaxes(1, 2))

    w_o16 = w_o.astype(BF16)
    h1, hn = _oproj(o_gdn.reshape(m, GDN_W), o_moba.reshape(m, MOBA_W), w_o16[:GDN_W], w_o16[GDN_W:],
                    x2, ffn_norm, min(256, m))
    h2, hn2 = _ffn(hn, w_gate.astype(BF16), w_up.astype(BF16), w_down.astype(BF16), h1, ple_norm,
                   min(512, m), 512)
    h3 = _ple(hn2, w_ple_gate.astype(BF16), p.reshape(m, PLE_DIM), w_ple_proj.astype(BF16), h2, tm, 512)
    return h3.reshape(b, t, d)


def kernel(x, p, attn_norm, w_in, conv_w, A_log, dt_bias, gdn_norm, q_norm, k_norm, w_o, ffn_norm,
           w_gate, w_up, w_down, ple_norm, w_ple_gate, w_ple_proj):
    h = x
    for i in range(p.shape[0]):
        h = _layer(h, p[i], attn_norm[i], w_in[i], conv_w[i], A_log[i], dt_bias[i], gdn_norm[i],
                   q_norm[i], k_norm[i], w_o[i], ffn_norm[i], w_gate[i], w_up[i], w_down[i],
                   ple_norm[i], w_ple_gate[i], w_ple_proj[i])
    return h
```

```python
import functools

import jax
import jax.numpy as jnp
from jax import lax
from jax.experimental import pallas as pl
from jax.experimental.pallas import tpu as pltpu

D_MODEL = 2048
PLE_DIM = 256
GDN_HEADS = 8
GDN_DK = 128
GDN_DV = 128
GDN_CONV = 4
GDN_CHUNK = 64
MOBA_HEADS = 8
MOBA_DH = 128
MOBA_BLOCK = 256
MOBA_TOPK = 3
RMS_EPS = 1e-6
GDN_W = GDN_HEADS * GDN_DK
MOBA_W = MOBA_HEADS * MOBA_DH
LANES = 128
LOG2E = 1.4426950408889634
MASK_NEG = -1e30

F32 = jnp.float32
BF16 = jnp.bfloat16

VMEM_LIMIT = 56 * 1024 * 1024


def _params(sem):
    return pltpu.CompilerParams(dimension_semantics=sem, vmem_limit_bytes=VMEM_LIMIT)


def _sigmoid(x):
    return 1.0 / (1.0 + jnp.exp(-x))


def _dot_t(a, b):
    return lax.dot_general(a, b, (((1,), (1,)), ((), ())), preferred_element_type=F32)


def _bdot(a, b):
    return jnp.dot(a.astype(BF16), b.astype(BF16), preferred_element_type=F32)


def _rmsnorm_kernel(x_ref, w_ref, o_ref):
    x = x_ref[...]
    y = x * lax.rsqrt(jnp.mean(x * x, axis=-1, keepdims=True) + RMS_EPS)
    o_ref[...] = (y * w_ref[...]).astype(o_ref.dtype)


def _rmsnorm(x, w, tm):
    m, d = x.shape
    return pl.pallas_call(
        _rmsnorm_kernel,
        grid=(m // tm,),
        in_specs=[pl.BlockSpec((tm, d), lambda i: (i, 0)), pl.BlockSpec((1, d), lambda i: (0, 0))],
        out_specs=pl.BlockSpec((tm, d), lambda i: (i, 0)),
        out_shape=jax.ShapeDtypeStruct((m, d), BF16),
        compiler_params=_params(("parallel",)),
        name="rmsnorm_cast",
    )(x, w.reshape(1, d))


def _proj_kernel(x_ref, w_ref, o_ref):
    o_ref[...] = jnp.dot(x_ref[...], w_ref[...], preferred_element_type=F32).astype(o_ref.dtype)


def _proj(x, w, out_dtype, tm, tn, name):
    m, k = x.shape
    n = w.shape[1]
    return pl.pallas_call(
        _proj_kernel,
        grid=(m // tm, n // tn),
        in_specs=[pl.BlockSpec((tm, k), lambda i, j: (i, 0)), pl.BlockSpec((k, tn), lambda i, j: (0, j))],
        out_specs=pl.BlockSpec((tm, tn), lambda i, j: (i, j)),
        out_shape=jax.ShapeDtypeStruct((m, n), out_dtype),
        compiler_params=_params(("parallel", "arbitrary")),
        name=name,
    )(x, w)


def _head_rmsnorm(y, gain, scale):
    outs = []
    for h in range(y.shape[1] // LANES):
        yh = y[:, h * LANES:(h + 1) * LANES]
        r = lax.rsqrt(jnp.mean(yh * yh, axis=-1, keepdims=True) + RMS_EPS)
        outs.append(yh * r * gain[:, h * LANES:(h + 1) * LANES] * scale)
    return jnp.concatenate(outs, axis=1)


def _proj_qnorm_kernel(x_ref, w_ref, g_ref, o_ref, *, scale):
    y = jnp.dot(x_ref[...], w_ref[...], preferred_element_type=F32)
    o_ref[...] = _head_rmsnorm(y, g_ref[...], scale).astype(o_ref.dtype)


def _proj_knorm_kernel(x_ref, w_ref, g_ref, o_ref, km_ref, *, nb_seq):
    y = jnp.dot(x_ref[...], w_ref[...], preferred_element_type=F32)
    yn = _head_rmsnorm(y, g_ref[...], 1.0)
    tm = yn.shape[0]
    row = lax.broadcasted_iota(jnp.int32, (tm, LANES), 0) + pl.program_id(0) * tm
    lane = lax.broadcasted_iota(jnp.int32, (tm, LANES), 1)
    onehot = jnp.where(lane == lax.rem(row // MOBA_BLOCK, nb_seq), 1.0, 0.0).astype(o_ref.dtype)
    yb = yn.astype(o_ref.dtype)
    parts = []
    for h in range(yn.shape[1] // LANES):
        parts += [yb[:, h * LANES:(h + 1) * LANES], onehot]
    o_ref[...] = jnp.concatenate(parts, axis=1)
    for r in range(tm // MOBA_BLOCK):
        blk = yn[r * MOBA_BLOCK:(r + 1) * MOBA_BLOCK]
        km_ref[r] = jnp.mean(blk, axis=0, keepdims=True)


def _proj_moba_q(x, w, gain, tm, tn):
    m, k = x.shape
    n = w.shape[1]
    g = jnp.tile(gain.reshape(1, MOBA_DH), (1, n // MOBA_DH))
    return pl.pallas_call(
        functools.partial(_proj_qnorm_kernel, scale=MOBA_DH ** -0.5 * LOG2E),
        grid=(m // tm, n // tn),
        in_specs=[pl.BlockSpec((tm, k), lambda i, j: (i, 0)), pl.BlockSpec((k, tn), lambda i, j: (0, j)),
                  pl.BlockSpec((1, tn), lambda i, j: (0, j))],
        out_specs=pl.BlockSpec((tm, tn), lambda i, j: (i, j)),
        out_shape=jax.ShapeDtypeStruct((m, n), BF16),
        compiler_params=_params(("parallel", "arbitrary")),
        name="proj_moba_q",
    )(x, w, g)


def _proj_moba_k(x, w, gain, tm, tn, nb_seq):
    m, k = x.shape
    n = w.shape[1]
    g = jnp.tile(gain.reshape(1, MOBA_DH), (1, n // MOBA_DH))
    nb = tm // MOBA_BLOCK
    return pl.pallas_call(
        functools.partial(_proj_knorm_kernel, nb_seq=nb_seq),
        grid=(m // tm, n // tn),
        in_specs=[pl.BlockSpec((tm, k), lambda i, j: (i, 0)), pl.BlockSpec((k, tn), lambda i, j: (0, j)),
                  pl.BlockSpec((1, tn), lambda i, j: (0, j))],
        out_specs=[pl.BlockSpec((tm, 2 * tn), lambda i, j: (i, j)),
                   pl.BlockSpec((nb, 1, tn), lambda i, j: (i, 0, j))],
        out_shape=[jax.ShapeDtypeStruct((m, 2 * n), BF16),
                   jax.ShapeDtypeStruct((m // MOBA_BLOCK, 1, n), F32)],
        compiler_params=_params(("parallel", "arbitrary")),
        name="proj_moba_k",
    )(x, w, g)


def _gdn_all_kernel(x_ref, ba_ref, cw_ref, alog_ref, dtb_ref, gn_ref, o_ref, s_ref, tail_ref, qkv_s, *, tb):
    t = pl.program_id(0)
    C = GDN_CHUNK
    nc = tb // C
    nh = GDN_HEADS
    nbat = x_ref.shape[0]

    @pl.when(t == 0)
    def _():
        s_ref[...] = jnp.zeros_like(s_ref)
        tail_ref[...] = jnp.zeros_like(tail_ref)

    def conv_group(gi, l2_scale):
        off = pl.multiple_of(gi * LANES, LANES)
        w = cw_ref[:, pl.ds(off, LANES)]
        for bi in range(nbat):
            x = x_ref[bi, :, pl.ds(off, LANES)]
            xp = jnp.concatenate([tail_ref[bi * 8:(bi + 1) * 8, pl.ds(off, LANES)], x], axis=0)
            y = x * w[GDN_CONV - 1:GDN_CONV]
            for j in range(GDN_CONV - 1):
                y = y + pltpu.roll(xp, GDN_CONV - 1 - j, axis=0)[8:8 + tb] * w[j:j + 1]
            tail_ref[bi * 8:(bi + 1) * 8, pl.ds(off, LANES)] = x[tb - 8:tb]
            y = y * _sigmoid(y)
            if l2_scale is not None:
                y = y * (lax.rsqrt(jnp.sum(y * y, axis=-1, keepdims=True) + 1e-6) * l2_scale)
            qkv_s[bi * tb:(bi + 1) * tb, pl.ds(off, LANES)] = y

    for g0, l2_scale in ((0, GDN_DK ** -0.5), (nh, 1.0), (2 * nh, None)):
        lax.fori_loop(g0, g0 + nh, lambda gi, carry, sc=l2_scale: conv_group(gi, sc), None)

    row_in_chunk = lax.broadcasted_iota(jnp.int32, (tb, LANES), 0) & (C - 1)
    beta_all, gc, egc, kdsc, egl, gc_t = [], [], [], [], [], []
    for bi in range(nbat):
        ba = ba_ref[bi]
        beta_all.append(_sigmoid(ba))
        sp_in = ba + dtb_ref[...]
        softplus = jnp.maximum(sp_in, 0.0) + jnp.log1p(jnp.exp(-jnp.abs(sp_in)))
        g = -(jnp.exp(alog_ref[...]) * softplus)
        sh = 1
        while sh < C:
            g = g + jnp.where(row_in_chunk >= sh, pltpu.roll(g, sh, axis=0), 0.0)
            sh *= 2
        gl_rows = [g[(c + 1) * C - 1:(c + 1) * C] for c in range(nc)]
        gl_b = jnp.concatenate([jnp.broadcast_to(r, (C, LANES)) for r in gl_rows], axis=0)
        gc.append(g)
        egc.append(jnp.exp(g))
        kdsc.append(jnp.exp(gl_b - g))
        egl.append([jnp.exp(r) for r in gl_rows])
        gc_t.append(g.T)

    ri = lax.broadcasted_iota(jnp.int32, (C, C), 0)
    ci = lax.broadcasted_iota(jnp.int32, (C, C), 1)
    tril = ri >= ci
    strict = ri > ci
    eye_f = (ri == ci).astype(F32)
    lvl_masks = []
    s = 1
    while s < C:
        sh2 = s.bit_length()
        lvl_masks.append((ri >> sh2 == ci >> sh2) & ((ri & (2 * s - 1)) >= s) & ((ci & (2 * s - 1)) < s))
        s *= 2
    gn = gn_ref[...]

    seqs = [(bi, h) for bi in range(nbat) for h in range(nh)]
    ids = range(len(seqs))
    states = [s_ref[i] for i in ids]
    grp = lambda g, h: slice((g * nh + h) * LANES, (g * nh + h + 1) * LANES)
    for c in range(nc):
        rows = slice(c * C, (c + 1) * C)
        srow = lambda bi: slice(bi * tb + c * C, bi * tb + (c + 1) * C)
        gate_col = lambda arr, bi, h: arr[bi][rows, nh + h:nh + h + 1]
        qc = [qkv_s[srow(bi), grp(0, h)] for bi, h in seqs]
        kc = [qkv_s[srow(bi), grp(1, h)] for bi, h in seqs]
        vc = [qkv_s[srow(bi), grp(2, h)] for bi, h in seqs]
        bcol = [beta_all[bi][rows, h:h + 1] for bi, h in seqs]
        ecol = [gate_col(egc, bi, h) for bi, h in seqs]
        decay = [jnp.exp(jnp.where(tril, gate_col(gc, bi, h) - gc_t[bi][nh + h:nh + h + 1, rows], -jnp.inf))
                 for bi, h in seqs]
        k_beta = [kc[i] * bcol[i] for i in ids]
        st = [_dot_t(jnp.concatenate([k_beta[i], qc[i]], axis=0).astype(BF16), kc[i].astype(BF16))
              for i in ids]
        a_mat = [jnp.where(strict, st[i][:C] * decay[i], 0.0) for i in ids]
        qk = [jnp.where(tril, st[i][C:] * decay[i], 0.0) for i in ids]
        t_inv = [eye_f - jnp.where(lvl_masks[0], a_mat[i], 0.0) for i in ids]
        for msk in lvl_masks[1:]:
            x_mid = [_bdot(jnp.where(msk, a_mat[i], 0.0), t_inv[i]) for i in ids]
            y_mid = [_bdot(t_inv[i], x_mid[i]) for i in ids]
            t_inv = [t_inv[i] - y_mid[i] for i in ids]
        uw = [_bdot(t_inv[i], jnp.concatenate([vc[i] * bcol[i], k_beta[i] * ecol[i]], axis=1))
              for i in ids]
        wq = [_bdot(jnp.concatenate([uw[i][:, LANES:], qc[i] * ecol[i]], axis=0), states[i])
              for i in ids]
        v_new = [uw[i][:, :LANES] - wq[i][:C] for i in ids]
        o = [wq[i][C:] + _bdot(qk[i], v_new[i]) for i in ids]
        k_dec = [kc[i] * gate_col(kdsc, bi, h) for i, (bi, h) in enumerate(seqs)]
        states = [states[i] * egl[bi][c][:, nh + h:nh + h + 1] + lax.dot_general(
            k_dec[i].astype(BF16), v_new[i].astype(BF16), (((0,), (0,)), ((), ())),
            preferred_element_type=F32) for i, (bi, h) in enumerate(seqs)]
        for i, (bi, h) in enumerate(seqs):
            on = o[i] * lax.rsqrt(jnp.mean(o[i] * o[i], axis=-1, keepdims=True) + RMS_EPS) * gn
            zc = x_ref[bi, rows, grp(3, h)]
            o_ref[bi, rows, grp(0, h)] = (on * (zc * _sigmoid(zc))).astype(o_ref.dtype)
    for i in ids:
        s_ref[i] = states[i]


def _gdn_all(proj, ba, conv_wt, a_log, dt_bias, gnorm, tb):
    b, t, width = proj.shape
    nh = GDN_HEADS
    lane_vec = lambda v: jnp.pad(v.reshape(1, nh), ((0, 0), (nh, LANES - 2 * nh)))
    full = lambda shape: pl.BlockSpec(shape, lambda ti: (0,) * len(shape))
    return pl.pallas_call(
        functools.partial(_gdn_all_kernel, tb=tb),
        grid=(t // tb,),
        in_specs=[pl.BlockSpec((b, tb, width), lambda ti: (0, ti, 0)),
                  pl.BlockSpec((b, tb, LANES), lambda ti: (0, ti, 0)),
                  full((GDN_CONV, 3 * GDN_W)), full((1, LANES)), full((1, LANES)), full((1, GDN_DV))],
        out_specs=pl.BlockSpec((b, tb, GDN_W), lambda ti: (0, ti, 0)),
        out_shape=jax.ShapeDtypeStruct((b, t, GDN_W), BF16),
        scratch_shapes=[pltpu.VMEM((b * nh, GDN_DK, GDN_DV), F32), pltpu.VMEM((b * 8, 3 * GDN_W), F32),
                        pltpu.VMEM((b * tb, 3 * GDN_W), F32)],
        compiler_params=_params(("arbitrary",)),
        name="gdn",
    )(proj, ba, conv_wt, lane_vec(a_log), lane_vec(dt_bias), gnorm.reshape(1, GDN_DV))


MOBA_QW = 2 * MOBA_BLOCK
MOBA_ROWS_L = 16


def _moba_select_kernel(qt_ref, km_ref, qa_ref, *, nb, tq):
    nbp = -(-nb // 16) * 16
    qt = qt_ref[0]
    km = km_ref[0].astype(BF16)
    if nbp > nb:
        km = jnp.concatenate([km, jnp.zeros((nbp - nb, LANES), BF16)], axis=0)
    gate = jnp.dot(km, qt, preferred_element_type=F32)
    row = lax.broadcasted_iota(jnp.int32, (nbp, tq), 0)
    qblk = (lax.broadcasted_iota(jnp.int32, (nbp, tq), 1) + pl.program_id(2) * tq) // MOBA_BLOCK
    past = row < qblk
    g = jnp.where(past, gate, -jnp.inf)
    sel = row < 0
    for _ in range(MOBA_TOPK):
        m = jnp.max(g, axis=0, keepdims=True)
        idx = jnp.min(jnp.where(g == m, row, nbp), axis=0, keepdims=True)
        hit = row == idx
        sel = sel | (hit & past)
        g = jnp.where(hit, -jnp.inf, g)
    qa_ref[0, 0:LANES, :] = qt
    qa_ref[0, LANES:LANES + nbp, :] = jnp.where(sel, 0.0, MASK_NEG).astype(BF16)
    if nbp < LANES:
        qa_ref[0, LANES + nbp:, :] = jnp.zeros((LANES - nbp, tq), BF16)


def _moba_select(qt, kmean, tq):
    b, _, t = qt.shape
    nb = t // MOBA_BLOCK
    assert nb <= LANES
    return pl.pallas_call(
        functools.partial(_moba_select_kernel, nb=nb, tq=tq),
        grid=(b, MOBA_HEADS, t // tq),
        in_specs=[pl.BlockSpec((1, LANES, tq), lambda bi, hi, qi: (bi, hi, qi)),
                  pl.BlockSpec((1, nb, LANES), lambda bi, hi, qi: (bi, 0, hi))],
        out_specs=pl.BlockSpec((1, 2 * LANES, tq), lambda bi, hi, qi: (bi, hi, qi)),
        out_shape=jax.ShapeDtypeStruct((b, 2 * MOBA_W, t), BF16),
        compiler_params=_params(("parallel", "parallel", "parallel")),
        name="moba_select",
    )(qt, kmean)


def _moba_kernel(qa_ref, ka_ref, vt_ref, o_ref, acc_ref, m_ref, sa_ref, sb_ref, xa_ref, xb_ref, *, nb, group):
    i2 = pl.program_id(2)
    BLK = MOBA_BLOCK

    def pv(p, start, width):
        lhs = jnp.concatenate([vt_ref[0, :, pl.ds(start, width)], jnp.ones((MOBA_ROWS_L, width), BF16)], axis=0)
        return jnp.dot(lhs, p, preferred_element_type=F32)

    key_i = lax.broadcasted_iota(jnp.int32, (BLK, BLK), 0)
    qry_i = lax.broadcasted_iota(jnp.int32, (BLK, BLK), 1)
    starts = [pl.multiple_of((2 * i2 + hf) * BLK, BLK) for hf in range(2)]
    s_own = [jnp.dot(ka_ref[0, pl.ds(starts[hf], BLK), 0:LANES], qa_ref[0, 0:LANES, hf * BLK:(hf + 1) * BLK],
                     preferred_element_type=F32) for hf in range(2)]
    s_own = [jnp.where(key_i <= qry_i, s, MASK_NEG) for s in s_own]
    m_own = [jnp.max(s, axis=0, keepdims=True) for s in s_own]
    p_own = [jnp.exp2(s_own[hf] - m_own[hf]).astype(BF16) for hf in range(2)]
    for hf in range(2):
        m_ref[:, hf * BLK:(hf + 1) * BLK] = m_own[hf]
        acc_ref[:, hf * BLK:(hf + 1) * BLK] = pv(p_own[hf], starts[hf], BLK)

    span = group * BLK
    n_pairs = (2 * i2 + 2 * group) // (2 * group)
    last = nb // group - 1
    halves = [slice(hf * BLK, (hf + 1) * BLK) for hf in range(2)]

    def qk(g_idx, dst_ref, mx_dst, hs):
        start = pl.multiple_of(g_idx * span, span)
        s = jnp.dot(ka_ref[0, pl.ds(start, span), :], qa_ref[0, :, hs], preferred_element_type=F32)
        dst_ref[:, hs] = s
        mx_dst[:, hs] = jnp.max(s, axis=0, keepdims=True)

    def softmax_pv(src_ref, mx_src, g_idx, hs):
        m_old = m_ref[:, hs]
        m_new = jnp.maximum(m_old, mx_src[:, hs])
        alpha = jnp.exp2(m_old - m_new)
        m_ref[:, hs] = m_new
        p = jnp.exp2(src_ref[:, hs] - m_new).astype(BF16)
        acc_ref[:, hs] = alpha * acc_ref[:, hs] + pv(p, pl.multiple_of(g_idx * span, span), span)

    for hs in halves:
        qk(0, sa_ref, xa_ref, hs)

    def body(jj, carry):
        for hs in halves:
            qk(2 * jj + 1, sb_ref, xb_ref, hs)
            softmax_pv(sa_ref, xa_ref, 2 * jj, hs)
        for hs in halves:
            qk(jnp.minimum(2 * jj + 2, last), sa_ref, xa_ref, hs)
            softmax_pv(sb_ref, xb_ref, 2 * jj + 1, hs)
        return carry

    lax.fori_loop(0, n_pairs, body, 0)
    acc = acc_ref[...]
    o = acc[0:LANES] * (1.0 / acc[LANES:LANES + 1])
    o_ref[0] = o.T.astype(o_ref.dtype)


def _moba(qaug, kaug, vt):
    b, _, t = vt.shape
    nb = t // MOBA_BLOCK
    group = 4 if nb % 8 == 0 else 2
    assert nb <= LANES and nb % (2 * group) == 0
    span = group * MOBA_BLOCK
    return pl.pallas_call(
        functools.partial(_moba_kernel, nb=nb, group=group),
        grid=(b, MOBA_HEADS, t // MOBA_QW),
        in_specs=[pl.BlockSpec((1, 2 * LANES, MOBA_QW), lambda bi, hi, qi: (bi, hi, qi)),
                  pl.BlockSpec((1, t, 2 * LANES), lambda bi, hi, qi: (bi, 0, hi)),
                  pl.BlockSpec((1, LANES, t), lambda bi, hi, qi: (bi, hi, 0))],
        out_specs=pl.BlockSpec((1, MOBA_QW, LANES), lambda bi, hi, qi: (bi, qi, hi)),
        out_shape=jax.ShapeDtypeStruct((b, t, MOBA_W), BF16),
        scratch_shapes=[pltpu.VMEM((LANES + MOBA_ROWS_L, MOBA_QW), F32),
                        pltpu.VMEM((1, MOBA_QW), F32),
                        pltpu.VMEM((span, MOBA_QW), F32), pltpu.VMEM((span, MOBA_QW), F32),
                        pltpu.VMEM((1, MOBA_QW), F32), pltpu.VMEM((1, MOBA_QW), F32)],
        compiler_params=_params(("parallel", "parallel", "arbitrary")),
        name="moba",
    )(qaug, kaug, vt)


def _oproj_kernel(og_ref, om_ref, wa_ref, wb_ref, x_ref, nw_ref, h_ref, hn_ref):
    y = jnp.dot(og_ref[...], wa_ref[...], preferred_element_type=F32)
    y = y + jnp.dot(om_ref[...], wb_ref[...], preferred_element_type=F32)
    h = x_ref[...] + y
    h_ref[...] = h
    hn = h * lax.rsqrt(jnp.mean(h * h, axis=-1, keepdims=True) + RMS_EPS)
    hn_ref[...] = (hn * nw_ref[...]).astype(hn_ref.dtype)


def _oproj(og, om, wa, wb, x, norm_w, tm):
    m, d = x.shape
    ka, kb = og.shape[1], om.shape[1]
    row = lambda w: pl.BlockSpec((tm, w), lambda i: (i, 0))
    full = lambda r, c: pl.BlockSpec((r, c), lambda i: (0, 0))
    return pl.pallas_call(
        _oproj_kernel,
        grid=(m // tm,),
        in_specs=[row(ka), row(kb), full(ka, d), full(kb, d), row(d), full(1, d)],
        out_specs=[row(d), row(d)],
        out_shape=[jax.ShapeDtypeStruct((m, d), F32), jax.ShapeDtypeStruct((m, d), BF16)],
        compiler_params=_params(("parallel",)),
        name="oproj",
    )(og, om, wa, wb, x, norm_w.reshape(1, d))


def _ffn_kernel(hn_ref, wg_ref, wu_ref, wd_ref, h_ref, nw_ref, h2_ref, hn2_ref, acc_ref):
    f = pl.program_id(1)

    @pl.when(f == 0)
    def _():
        acc_ref[...] = jnp.zeros_like(acc_ref)

    hn = hn_ref[...]
    g = jnp.dot(hn, wg_ref[...], preferred_element_type=F32)
    u = jnp.dot(hn, wu_ref[...], preferred_element_type=F32)
    a = (g * _sigmoid(g) * u).astype(BF16)
    acc_ref[...] += jnp.dot(a, wd_ref[...], preferred_element_type=F32)

    @pl.when(f == pl.num_programs(1) - 1)
    def _():
        h2 = h_ref[...] + acc_ref[...]
        h2_ref[...] = h2
        n = h2 * lax.rsqrt(jnp.mean(h2 * h2, axis=-1, keepdims=True) + RMS_EPS)
        hn2_ref[...] = (n * nw_ref[...]).astype(hn2_ref.dtype)


def _ffn(hn, wg, wu, wd, h, norm_w, tm, tf):
    m, d = h.shape
    dff = wg.shape[1]
    row = pl.BlockSpec((tm, d), lambda i, f: (i, 0))
    return pl.pallas_call(
        _ffn_kernel,
        grid=(m // tm, dff // tf),
        in_specs=[row, pl.BlockSpec((d, tf), lambda i, f: (0, f)), pl.BlockSpec((d, tf), lambda i, f: (0, f)),
                  pl.BlockSpec((tf, d), lambda i, f: (f, 0)), row, pl.BlockSpec((1, d), lambda i, f: (0, 0))],
        out_specs=[row, row],
        out_shape=[jax.ShapeDtypeStruct((m, d), F32), jax.ShapeDtypeStruct((m, d), BF16)],
        scratch_shapes=[pltpu.VMEM((tm, d), F32)],
        compiler_params=_params(("parallel", "arbitrary")),
        name="ffn",
    )(hn, wg, wu, wd, h, norm_w.reshape(1, d))


def _ple_kernel(hn_ref, wg_ref, p_ref, wp_ref, h_ref, o_ref):
    gate = _sigmoid(jnp.dot(hn_ref[...], wg_ref[...], preferred_element_type=F32))
    proj = jnp.dot(p_ref[...].astype(BF16), wp_ref[...], preferred_element_type=F32)
    o_ref[...] = h_ref[...] + gate * proj


def _ple(hn, wg, p, wp, h, tm, tn):
    m, d = h.shape
    kp = p.shape[1]
    return pl.pallas_call(
        _ple_kernel,
        grid=(m // tm, d // tn),
        in_specs=[pl.BlockSpec((tm, d), lambda i, j: (i, 0)), pl.BlockSpec((d, tn), lambda i, j: (0, j)),
                  pl.BlockSpec((tm, kp), lambda i, j: (i, 0)), pl.BlockSpec((kp, tn), lambda i, j: (0, j)),
                  pl.BlockSpec((tm, tn), lambda i, j: (i, j))],
        out_specs=pl.BlockSpec((tm, tn), lambda i, j: (i, j)),
        out_shape=jax.ShapeDtypeStruct((m, d), F32),
        compiler_params=_params(("parallel", "arbitrary")),
        name="ple",
    )(hn, wg, p, wp, h)


def _layer(h, p, attn_norm, w_in, conv_w, a_log, dt_bias, gdn_norm, q_norm, k_norm, w_o, ffn_norm,
           w_gate, w_up, w_down, ple_norm, w_ple_gate, w_ple_proj):
    b, t, d = h.shape
    m = b * t
    x2 = h.reshape(m, d)
    tm = min(1024, m)

    o_ba = 4 * GDN_W
    o_mq = o_ba + 2 * GDN_HEADS
    w_in16 = w_in.astype(BF16)
    w_gdn = w_in16[:, :o_ba]
    w_ba = jnp.pad(w_in16[:, o_ba:o_mq], ((0, 0), (0, LANES - 2 * GDN_HEADS)))
    w_mq = w_in16[:, o_mq:o_mq + MOBA_W]
    w_mk = w_in16[:, o_mq + MOBA_W:o_mq + 2 * MOBA_W]
    w_mv = w_in16[:, o_mq + 2 * MOBA_W:o_mq + 3 * MOBA_W]

    xn = _rmsnorm(x2, attn_norm, min(512, m))
    gproj = _proj(xn, w_gdn, F32, tm, 1024, "proj_gdn")
    ba = _proj(xn, w_ba, F32, tm, LANES, "proj_ba")
    mq = _proj_moba_q(xn, w_mq, q_norm, tm, 512)
    kaug, kmean = _proj_moba_k(xn, w_mk, k_norm, tm, 512, t // MOBA_BLOCK)
    mv = _proj(xn, w_mv, BF16, tm, 512, "proj_moba_v")

    o_gdn = _gdn_all(gproj.reshape(b, t, 4 * GDN_W), ba.reshape(b, t, LANES), conv_w.T, a_log, dt_bias,
                     gdn_norm, min(256, t))
    qaug = _moba_select(mq.reshape(b, t, MOBA_W).swapaxes(1, 2), kmean.reshape(b, t // MOBA_BLOCK, MOBA_W),
                        min(2048, t))
    o_moba = _moba(qaug, kaug.reshape(b, t, 2 * MOBA_W), mv.reshape(b, t, MOBA_W).swapaxes(1, 2))

    w_o16 = w_o.astype(BF16)
    h1, hn = _oproj(o_gdn.reshape(m, GDN_W), o_moba.reshape(m, MOBA_W), w_o16[:GDN_W], w_o16[GDN_W:],
                    x2, ffn_norm, min(512, m))
    h2, hn2 = _ffn(hn, w_gate.astype(BF16), w_up.astype(BF16), w_down.astype(BF16), h1, ple_norm,
                   min(512, m), 512)
    h3 = _ple(hn2, w_ple_gate.astype(BF16), p.reshape(m, PLE_DIM), w_ple_proj.astype(BF16), h2, tm, 512)
    return h3.reshape(b, t, d)


def kernel(x, p, attn_norm, w_in, conv_w, A_log, dt_bias, gdn_norm, q_norm, k_norm, w_o, ffn_norm,
           w_gate, w_up, w_down, ple_norm, w_ple_gate, w_ple_proj):
    h = x
    for i in range(p.shape[0]):
        h = _layer(h, p[i], attn_norm[i], w_in[i], conv_w[i], A_log[i], dt_bias[i], gdn_norm[i],
                   q_norm[i], k_norm[i], w_o[i], ffn_norm[i], w_gate[i], w_up[i], w_down[i],
                   ple_norm[i], w_ple_gate[i], w_ple_proj[i])
    return h
```

```python
import functools

import jax
import jax.numpy as jnp
from jax import lax
from jax.experimental import pallas as pl
from jax.experimental.pallas import tpu as pltpu

D_MODEL = 2048
PLE_DIM = 256
GDN_HEADS = 8
GDN_DK = 128
GDN_DV = 128
GDN_CONV = 4
GDN_CHUNK = 64
MOBA_HEADS = 8
MOBA_DH = 128
MOBA_BLOCK = 256
MOBA_TOPK = 3
RMS_EPS = 1e-6
GDN_W = GDN_HEADS * GDN_DK
MOBA_W = MOBA_HEADS * MOBA_DH
LANES = 128
LOG2E = 1.4426950408889634
MASK_NEG = -1e30

F32 = jnp.float32
BF16 = jnp.bfloat16

VMEM_LIMIT = 56 * 1024 * 1024


def _params(sem):
    return pltpu.CompilerParams(dimension_semantics=sem, vmem_limit_bytes=VMEM_LIMIT)


def _sigmoid(x):
    return 1.0 / (1.0 + jnp.exp(-x))


def _dot_t(a, b):
    return lax.dot_general(a, b, (((1,), (1,)), ((), ())), preferred_element_type=F32)


def _bdot(a, b):
    return jnp.dot(a.astype(BF16), b.astype(BF16), preferred_element_type=F32)


def _rmsnorm_kernel(x_ref, w_ref, o_ref):
    x = x_ref[...]
    y = x * lax.rsqrt(jnp.mean(x * x, axis=-1, keepdims=True) + RMS_EPS)
    o_ref[...] = (y * w_ref[...]).astype(o_ref.dtype)


def _rmsnorm(x, w, tm):
    m, d = x.shape
    return pl.pallas_call(
        _rmsnorm_kernel,
        grid=(m // tm,),
        in_specs=[pl.BlockSpec((tm, d), lambda i: (i, 0)), pl.BlockSpec((1, d), lambda i: (0, 0))],
        out_specs=pl.BlockSpec((tm, d), lambda i: (i, 0)),
        out_shape=jax.ShapeDtypeStruct((m, d), BF16),
        compiler_params=_params(("parallel",)),
        name="rmsnorm_cast",
    )(x, w.reshape(1, d))


def _proj_kernel(x_ref, w_ref, o_ref):
    o_ref[...] = jnp.dot(x_ref[...], w_ref[...], preferred_element_type=F32).astype(o_ref.dtype)


def _proj(x, w, out_dtype, tm, tn, name):
    m, k = x.shape
    n = w.shape[1]
    return pl.pallas_call(
        _proj_kernel,
        grid=(m // tm, n // tn),
        in_specs=[pl.BlockSpec((tm, k), lambda i, j: (i, 0)), pl.BlockSpec((k, tn), lambda i, j: (0, j))],
        out_specs=pl.BlockSpec((tm, tn), lambda i, j: (i, j)),
        out_shape=jax.ShapeDtypeStruct((m, n), out_dtype),
        compiler_params=_params(("parallel", "arbitrary")),
        name=name,
    )(x, w)


def _head_rmsnorm(y, gain, scale):
    outs = []
    for h in range(y.shape[1] // LANES):
        yh = y[:, h * LANES:(h + 1) * LANES]
        r = lax.rsqrt(jnp.mean(yh * yh, axis=-1, keepdims=True) + RMS_EPS)
        outs.append(yh * r * gain[:, h * LANES:(h + 1) * LANES] * scale)
    return jnp.concatenate(outs, axis=1)


def _proj_qnorm_kernel(x_ref, w_ref, g_ref, o_ref, *, scale):
    y = jnp.dot(x_ref[...], w_ref[...], preferred_element_type=F32)
    o_ref[...] = _head_rmsnorm(y, g_ref[...], scale).astype(o_ref.dtype)


def _proj_knorm_kernel(x_ref, w_ref, g_ref, o_ref, km_ref, *, nb_seq):
    y = jnp.dot(x_ref[...], w_ref[...], preferred_element_type=F32)
    yn = _head_rmsnorm(y, g_ref[...], 1.0)
    tm = yn.shape[0]
    row = lax.broadcasted_iota(jnp.int32, (tm, LANES), 0) + pl.program_id(0) * tm
    lane = lax.broadcasted_iota(jnp.int32, (tm, LANES), 1)
    onehot = jnp.where((lane == lax.rem(row // MOBA_BLOCK, nb_seq)) | (lane == LANES - 1), 1.0, 0.0)
    onehot = onehot.astype(o_ref.dtype)
    yb = yn.astype(o_ref.dtype)
    parts = []
    for h in range(yn.shape[1] // LANES):
        parts += [yb[:, h * LANES:(h + 1) * LANES], onehot]
    o_ref[...] = jnp.concatenate(parts, axis=1)
    for r in range(tm // MOBA_BLOCK):
        blk = yn[r * MOBA_BLOCK:(r + 1) * MOBA_BLOCK]
        km_ref[r] = jnp.mean(blk, axis=0, keepdims=True)


def _proj_moba_q(x, w, gain, tm, tn):
    m, k = x.shape
    n = w.shape[1]
    g = jnp.tile(gain.reshape(1, MOBA_DH), (1, n // MOBA_DH))
    return pl.pallas_call(
        functools.partial(_proj_qnorm_kernel, scale=MOBA_DH ** -0.5 * LOG2E),
        grid=(m // tm, n // tn),
        in_specs=[pl.BlockSpec((tm, k), lambda i, j: (i, 0)), pl.BlockSpec((k, tn), lambda i, j: (0, j)),
                  pl.BlockSpec((1, tn), lambda i, j: (0, j))],
        out_specs=pl.BlockSpec((tm, tn), lambda i, j: (i, j)),
        out_shape=jax.ShapeDtypeStruct((m, n), BF16),
        compiler_params=_params(("parallel", "arbitrary")),
        name="proj_moba_q",
    )(x, w, g)


def _proj_moba_k(x, w, gain, tm, tn, nb_seq):
    m, k = x.shape
    n = w.shape[1]
    g = jnp.tile(gain.reshape(1, MOBA_DH), (1, n // MOBA_DH))
    nb = tm // MOBA_BLOCK
    return pl.pallas_call(
        functools.partial(_proj_knorm_kernel, nb_seq=nb_seq),
        grid=(m // tm, n // tn),
        in_specs=[pl.BlockSpec((tm, k), lambda i, j: (i, 0)), pl.BlockSpec((k, tn), lambda i, j: (0, j)),
                  pl.BlockSpec((1, tn), lambda i, j: (0, j))],
        out_specs=[pl.BlockSpec((tm, 2 * tn), lambda i, j: (i, j)),
                   pl.BlockSpec((nb, 1, tn), lambda i, j: (i, 0, j))],
        out_shape=[jax.ShapeDtypeStruct((m, 2 * n), BF16),
                   jax.ShapeDtypeStruct((m // MOBA_BLOCK, 1, n), F32)],
        compiler_params=_params(("parallel", "arbitrary")),
        name="proj_moba_k",
    )(x, w, g)


def _gdn_all_kernel(x_ref, ba_ref, cw_ref, alog_ref, dtb_ref, gn_ref, o_ref, s_ref, tail_ref, qkv_s, *, tb):
    t = pl.program_id(0)
    C = GDN_CHUNK
    nc = tb // C
    nh = GDN_HEADS
    nbat = x_ref.shape[0]

    @pl.when(t == 0)
    def _():
        s_ref[...] = jnp.zeros_like(s_ref)
        tail_ref[...] = jnp.zeros_like(tail_ref)

    def conv_group(gi, l2_scale):
        off = pl.multiple_of(gi * LANES, LANES)
        w = cw_ref[:, pl.ds(off, LANES)]
        for bi in range(nbat):
            x = x_ref[bi, :, pl.ds(off, LANES)]
            xp = jnp.concatenate([tail_ref[bi * 8:(bi + 1) * 8, pl.ds(off, LANES)], x], axis=0)
            y = x * w[GDN_CONV - 1:GDN_CONV]
            for j in range(GDN_CONV - 1):
                y = y + pltpu.roll(xp, GDN_CONV - 1 - j, axis=0)[8:8 + tb] * w[j:j + 1]
            tail_ref[bi * 8:(bi + 1) * 8, pl.ds(off, LANES)] = x[tb - 8:tb]
            y = y * _sigmoid(y)
            if l2_scale is not None:
                y = y * (lax.rsqrt(jnp.sum(y * y, axis=-1, keepdims=True) + 1e-6) * l2_scale)
            qkv_s[bi * tb:(bi + 1) * tb, pl.ds(off, LANES)] = y

    for g0, l2_scale in ((0, GDN_DK ** -0.5), (nh, 1.0), (2 * nh, None)):
        lax.fori_loop(g0, g0 + nh, lambda gi, carry, sc=l2_scale: conv_group(gi, sc), None)

    row_in_chunk = lax.broadcasted_iota(jnp.int32, (tb, LANES), 0) & (C - 1)
    beta_all, gc, egc, kdsc, egl, gc_t = [], [], [], [], [], []
    for bi in range(nbat):
        ba = ba_ref[bi]
        beta_all.append(_sigmoid(ba))
        sp_in = ba + dtb_ref[...]
        softplus = jnp.maximum(sp_in, 0.0) + jnp.log1p(jnp.exp(-jnp.abs(sp_in)))
        g = -(jnp.exp(alog_ref[...]) * softplus)
        sh = 1
        while sh < C:
            g = g + jnp.where(row_in_chunk >= sh, pltpu.roll(g, sh, axis=0), 0.0)
            sh *= 2
        gl_rows = [g[(c + 1) * C - 1:(c + 1) * C] for c in range(nc)]
        gl_b = jnp.concatenate([jnp.broadcast_to(r, (C, LANES)) for r in gl_rows], axis=0)
        gc.append(g)
        egc.append(jnp.exp(g))
        kdsc.append(jnp.exp(gl_b - g))
        egl.append([jnp.exp(r) for r in gl_rows])
        gc_t.append(g.T)

    ri = lax.broadcasted_iota(jnp.int32, (C, C), 0)
    ci = lax.broadcasted_iota(jnp.int32, (C, C), 1)
    tril = ri >= ci
    strict = ri > ci
    eye_f = (ri == ci).astype(F32)
    lvl_masks = []
    s = 1
    while s < C:
        sh2 = s.bit_length()
        lvl_masks.append((ri >> sh2 == ci >> sh2) & ((ri & (2 * s - 1)) >= s) & ((ci & (2 * s - 1)) < s))
        s *= 2
    gn = gn_ref[...]

    seqs = [(bi, h) for bi in range(nbat) for h in range(nh)]
    ids = range(len(seqs))
    states = [s_ref[i] for i in ids]
    grp = lambda g, h: slice((g * nh + h) * LANES, (g * nh + h + 1) * LANES)
    for c in range(nc):
        rows = slice(c * C, (c + 1) * C)
        srow = lambda bi: slice(bi * tb + c * C, bi * tb + (c + 1) * C)
        gate_col = lambda arr, bi, h: arr[bi][rows, nh + h:nh + h + 1]
        qc = [qkv_s[srow(bi), grp(0, h)] for bi, h in seqs]
        kc = [qkv_s[srow(bi), grp(1, h)] for bi, h in seqs]
        vc = [qkv_s[srow(bi), grp(2, h)] for bi, h in seqs]
        bcol = [beta_all[bi][rows, h:h + 1] for bi, h in seqs]
        ecol = [gate_col(egc, bi, h) for bi, h in seqs]
        decay = [jnp.exp(jnp.where(tril, gate_col(gc, bi, h) - gc_t[bi][nh + h:nh + h + 1, rows], -jnp.inf))
                 for bi, h in seqs]
        k_beta = [kc[i] * bcol[i] for i in ids]
        st = [_dot_t(jnp.concatenate([k_beta[i], qc[i]], axis=0).astype(BF16), kc[i].astype(BF16))
              for i in ids]
        a_mat = [jnp.where(strict, st[i][:C] * decay[i], 0.0) for i in ids]
        qk = [jnp.where(tril, st[i][C:] * decay[i], 0.0) for i in ids]
        t_inv = [eye_f - jnp.where(lvl_masks[0], a_mat[i], 0.0) for i in ids]
        for msk in lvl_masks[1:]:
            x_mid = [_bdot(jnp.where(msk, a_mat[i], 0.0), t_inv[i]) for i in ids]
            y_mid = [_bdot(t_inv[i], x_mid[i]) for i in ids]
            t_inv = [t_inv[i] - y_mid[i] for i in ids]
        uw = [_bdot(t_inv[i], jnp.concatenate([vc[i] * bcol[i], k_beta[i] * ecol[i]], axis=1))
              for i in ids]
        wq = [_bdot(jnp.concatenate([uw[i][:, LANES:], qc[i] * ecol[i]], axis=0), states[i])
              for i in ids]
        v_new = [uw[i][:, :LANES] - wq[i][:C] for i in ids]
        o = [wq[i][C:] + _bdot(qk[i], v_new[i]) for i in ids]
        k_dec = [kc[i] * gate_col(kdsc, bi, h) for i, (bi, h) in enumerate(seqs)]
        states = [states[i] * egl[bi][c][:, nh + h:nh + h + 1] + lax.dot_general(
            k_dec[i].astype(BF16), v_new[i].astype(BF16), (((0,), (0,)), ((), ())),
            preferred_element_type=F32) for i, (bi, h) in enumerate(seqs)]
        for i, (bi, h) in enumerate(seqs):
            on = o[i] * lax.rsqrt(jnp.mean(o[i] * o[i], axis=-1, keepdims=True) + RMS_EPS) * gn
            zc = x_ref[bi, rows, grp(3, h)]
            o_ref[bi, rows, grp(0, h)] = (on * (zc * _sigmoid(zc))).astype(o_ref.dtype)
    for i in ids:
        s_ref[i] = states[i]


def _gdn_all(proj, ba, conv_wt, a_log, dt_bias, gnorm, tb):
    b, t, width = proj.shape
    nh = GDN_HEADS
    lane_vec = lambda v: jnp.pad(v.reshape(1, nh), ((0, 0), (nh, LANES - 2 * nh)))
    full = lambda shape: pl.BlockSpec(shape, lambda ti: (0,) * len(shape))
    return pl.pallas_call(
        functools.partial(_gdn_all_kernel, tb=tb),
        grid=(t // tb,),
        in_specs=[pl.BlockSpec((b, tb, width), lambda ti: (0, ti, 0)),
                  pl.BlockSpec((b, tb, LANES), lambda ti: (0, ti, 0)),
                  full((GDN_CONV, 3 * GDN_W)), full((1, LANES)), full((1, LANES)), full((1, GDN_DV))],
        out_specs=pl.BlockSpec((b, tb, GDN_W), lambda ti: (0, ti, 0)),
        out_shape=jax.ShapeDtypeStruct((b, t, GDN_W), BF16),
        scratch_shapes=[pltpu.VMEM((b * nh, GDN_DK, GDN_DV), F32), pltpu.VMEM((b * 8, 3 * GDN_W), F32),
                        pltpu.VMEM((b * tb, 3 * GDN_W), F32)],
        compiler_params=_params(("arbitrary",)),
        name="gdn",
    )(proj, ba, conv_wt, lane_vec(a_log), lane_vec(dt_bias), gnorm.reshape(1, GDN_DV))


MOBA_QW = 2 * MOBA_BLOCK
MOBA_ROWS_L = 16
MOBA_BOUND_SLACK = 1.02
MOBA_SAFE_LOG2_RANGE = 100.0


def _moba_select_kernel(qt_ref, km_ref, kn_ref, qa_ref, *, nb, tq):
    nbp = -(-nb // 16) * 16
    qt = qt_ref[0]
    km = km_ref[0].astype(BF16)
    if nbp > nb:
        km = jnp.concatenate([km, jnp.zeros((nbp - nb, LANES), BF16)], axis=0)
    gate = jnp.dot(km, qt, preferred_element_type=F32)
    row = lax.broadcasted_iota(jnp.int32, (nbp, tq), 0)
    qblk = (lax.broadcasted_iota(jnp.int32, (nbp, tq), 1) + pl.program_id(2) * tq) // MOBA_BLOCK
    past = row < qblk
    g = jnp.where(past, gate, -jnp.inf)
    sel = row < 0
    for _ in range(MOBA_TOPK):
        m = jnp.max(g, axis=0, keepdims=True)
        idx = jnp.min(jnp.where(g == m, row, nbp), axis=0, keepdims=True)
        hit = row == idx
        sel = sel | (hit & past)
        g = jnp.where(hit, -jnp.inf, g)
    q32 = qt.astype(F32)
    bound = jnp.sqrt(jnp.sum(q32 * q32, axis=0, keepdims=True)) * kn_ref[:, 0:1] * MOBA_BOUND_SLACK
    qa_ref[0, 0:LANES, :] = qt
    qa_ref[0, LANES:LANES + nbp, :] = jnp.where(sel, 0.0, MASK_NEG).astype(BF16)
    tail_row = lax.broadcasted_iota(jnp.int32, (LANES - nbp, tq), 0)
    qa_ref[0, LANES + nbp:, :] = jnp.where(tail_row == LANES - nbp - 1, -bound, 0.0).astype(BF16)


def _moba_select(qt, kmean, k_norm_bound, tq):
    b, _, t = qt.shape
    nb = t // MOBA_BLOCK
    assert -(-nb // 16) * 16 < LANES
    return pl.pallas_call(
        functools.partial(_moba_select_kernel, nb=nb, tq=tq),
        grid=(b, MOBA_HEADS, t // tq),
        in_specs=[pl.BlockSpec((1, LANES, tq), lambda bi, hi, qi: (bi, hi, qi)),
                  pl.BlockSpec((1, nb, LANES), lambda bi, hi, qi: (bi, 0, hi)),
                  pl.BlockSpec((1, LANES), lambda bi, hi, qi: (0, 0))],
        out_specs=pl.BlockSpec((1, 2 * LANES, tq), lambda bi, hi, qi: (bi, hi, qi)),
        out_shape=jax.ShapeDtypeStruct((b, 2 * MOBA_W, t), BF16),
        compiler_params=_params(("parallel", "parallel", "parallel")),
        name="moba_select",
    )(qt, kmean, k_norm_bound)


def _moba_kernel(qa_ref, ka_ref, vt_ref, o_ref, acc_ref, m_ref, sa_ref, sb_ref, xa_ref, xb_ref, *, nb, group):
    i2 = pl.program_id(2)
    BLK = MOBA_BLOCK

    def pv(p, start, width):
        lhs = jnp.concatenate([vt_ref[0, :, pl.ds(start, width)], jnp.ones((MOBA_ROWS_L, width), BF16)], axis=0)
        return jnp.dot(lhs, p, preferred_element_type=F32)

    key_i = lax.broadcasted_iota(jnp.int32, (BLK, BLK), 0)
    qry_i = lax.broadcasted_iota(jnp.int32, (BLK, BLK), 1)
    starts = [pl.multiple_of((2 * i2 + hf) * BLK, BLK) for hf in range(2)]
    s_own = [jnp.dot(ka_ref[0, pl.ds(starts[hf], BLK), 0:LANES], qa_ref[0, 0:LANES, hf * BLK:(hf + 1) * BLK],
                     preferred_element_type=F32) for hf in range(2)]
    s_own = [s_own[hf] + qa_ref[0, 2 * LANES - 1:2 * LANES, hf * BLK:(hf + 1) * BLK].astype(F32) for hf in range(2)]
    s_own = [jnp.where(key_i <= qry_i, s, MASK_NEG) for s in s_own]
    m_own = [jnp.max(s, axis=0, keepdims=True) for s in s_own]
    p_own = [jnp.exp2(s_own[hf] - m_own[hf]).astype(BF16) for hf in range(2)]
    for hf in range(2):
        m_ref[:, hf * BLK:(hf + 1) * BLK] = m_own[hf]
        acc_ref[:, hf * BLK:(hf + 1) * BLK] = pv(p_own[hf], starts[hf], BLK)

    span = group * BLK
    n_pairs = (2 * i2 + 2 * group) // (2 * group)
    last = nb // group - 1
    halves = [slice(hf * BLK, (hf + 1) * BLK) for hf in range(2)]

    def qk(g_idx, dst_ref, mx_dst, hs):
        start = pl.multiple_of(g_idx * span, span)
        s = jnp.dot(ka_ref[0, pl.ds(start, span), :], qa_ref[0, :, hs], preferred_element_type=F32)
        dst_ref[:, hs] = s
        mx_dst[:, hs] = jnp.max(s, axis=0, keepdims=True)

    def softmax_pv(src_ref, mx_src, g_idx, hs):
        m_old = m_ref[:, hs]
        m_new = jnp.maximum(m_old, mx_src[:, hs])
        alpha = jnp.exp2(m_old - m_new)
        m_ref[:, hs] = m_new
        p = jnp.exp2(src_ref[:, hs] - m_new).astype(BF16)
        acc_ref[:, hs] = alpha * acc_ref[:, hs] + pv(p, pl.multiple_of(g_idx * span, span), span)

    for hs in halves:
        qk(0, sa_ref, xa_ref, hs)

    def body(jj, carry):
        for hs in halves:
            qk(2 * jj + 1, sb_ref, xb_ref, hs)
            softmax_pv(sa_ref, xa_ref, 2 * jj, hs)
        for hs in halves:
            qk(jnp.minimum(2 * jj + 2, last), sa_ref, xa_ref, hs)
            softmax_pv(sb_ref, xb_ref, 2 * jj + 1, hs)
        return carry

    lax.fori_loop(0, n_pairs, body, 0)
    acc = acc_ref[...]
    o = acc[0:LANES] * (1.0 / acc[LANES:LANES + 1])
    o_ref[0] = o.T.astype(o_ref.dtype)


def _moba_bounded_kernel(qa_ref, ka_ref, vt_ref, o_ref, acc_ref, pa_ref, pb_ref, *, nb, group):
    i2 = pl.program_id(2)
    BLK = MOBA_BLOCK

    def pv(p, start, width):
        lhs = jnp.concatenate([vt_ref[0, :, pl.ds(start, width)], jnp.ones((MOBA_ROWS_L, width), BF16)], axis=0)
        return jnp.dot(lhs, p, preferred_element_type=F32)

    halves = [slice(hf * BLK, (hf + 1) * BLK) for hf in range(2)]
    key_i = lax.broadcasted_iota(jnp.int32, (BLK, BLK), 0)
    qry_i = lax.broadcasted_iota(jnp.int32, (BLK, BLK), 1)
    for hf, hs in enumerate(halves):
        start = pl.multiple_of((2 * i2 + hf) * BLK, BLK)
        s = jnp.dot(ka_ref[0, pl.ds(start, BLK), 0:LANES], qa_ref[0, 0:LANES, hs], preferred_element_type=F32)
        s = s + qa_ref[0, 2 * LANES - 1:2 * LANES, hs].astype(F32)
        p = jnp.exp2(jnp.where(key_i <= qry_i, s, MASK_NEG)).astype(BF16)
        acc_ref[:, hs] = pv(p, start, BLK)

    span = group * BLK
    n_pairs = (2 * i2 + 2 * group) // (2 * group)
    last = nb // group - 1

    def qk_exp(g_idx, p_dst, hs):
        start = pl.multiple_of(g_idx * span, span)
        s = jnp.dot(ka_ref[0, pl.ds(start, span), :], qa_ref[0, :, hs], preferred_element_type=F32)
        p_dst[:, hs] = jnp.exp2(s).astype(BF16)

    def pv_acc(p_src, g_idx, hs):
        acc_ref[:, hs] += pv(p_src[:, hs], pl.multiple_of(g_idx * span, span), span)

    for hs in halves:
        qk_exp(0, pa_ref, hs)

    def body(jj, carry):
        for hs in halves:
            qk_exp(2 * jj + 1, pb_ref, hs)
            pv_acc(pa_ref, 2 * jj, hs)
        for hs in halves:
            qk_exp(jnp.minimum(2 * jj + 2, last), pa_ref, hs)
            pv_acc(pb_ref, 2 * jj + 1, hs)
        return carry

    lax.fori_loop(0, n_pairs, body, 0)
    acc = acc_ref[...]
    o = acc[0:LANES] * (1.0 / acc[LANES:LANES + 1])
    o_ref[0] = o.T.astype(o_ref.dtype)


def _moba(qaug, kaug, vt):
    b, _, t = vt.shape
    nb = t // MOBA_BLOCK
    group = 4 if nb % 8 == 0 else 2
    assert nb % (2 * group) == 0
    span = group * MOBA_BLOCK

    def call(body, scratch, name):
        return pl.pallas_call(
            functools.partial(body, nb=nb, group=group),
            grid=(b, MOBA_HEADS, t // MOBA_QW),
            in_specs=[pl.BlockSpec((1, 2 * LANES, MOBA_QW), lambda bi, hi, qi: (bi, hi, qi)),
                      pl.BlockSpec((1, t, 2 * LANES), lambda bi, hi, qi: (bi, 0, hi)),
                      pl.BlockSpec((1, LANES, t), lambda bi, hi, qi: (bi, hi, 0))],
            out_specs=pl.BlockSpec((1, MOBA_QW, LANES), lambda bi, hi, qi: (bi, qi, hi)),
            out_shape=jax.ShapeDtypeStruct((b, t, MOBA_W), BF16),
            scratch_shapes=[pltpu.VMEM((LANES + MOBA_ROWS_L, MOBA_QW), F32)] + scratch,
            compiler_params=_params(("parallel", "parallel", "arbitrary")),
            name=name,
        )

    online = call(_moba_kernel,
                  [pltpu.VMEM((1, MOBA_QW), F32),
                   pltpu.VMEM((span, MOBA_QW), F32), pltpu.VMEM((span, MOBA_QW), F32),
                   pltpu.VMEM((1, MOBA_QW), F32), pltpu.VMEM((1, MOBA_QW), F32)], "moba")
    bounded = call(_moba_bounded_kernel,
                   [pltpu.VMEM((span, MOBA_QW), BF16), pltpu.VMEM((span, MOBA_QW), BF16)], "moba_bounded")
    max_bound = -jnp.min(qaug[:, 2 * LANES - 1::2 * LANES, :].astype(F32))
    return lax.cond(2.0 * max_bound < MOBA_SAFE_LOG2_RANGE, bounded, online, qaug, kaug, vt)


def _oproj_kernel(og_ref, om_ref, wa_ref, wb_ref, x_ref, nw_ref, h_ref, hn_ref):
    y = jnp.dot(og_ref[...], wa_ref[...], preferred_element_type=F32)
    y = y + jnp.dot(om_ref[...], wb_ref[...], preferred_element_type=F32)
    h = x_ref[...] + y
    h_ref[...] = h
    hn = h * lax.rsqrt(jnp.mean(h * h, axis=-1, keepdims=True) + RMS_EPS)
    hn_ref[...] = (hn * nw_ref[...]).astype(hn_ref.dtype)


def _oproj(og, om, wa, wb, x, norm_w, tm):
    m, d = x.shape
    ka, kb = og.shape[1], om.shape[1]
    row = lambda w: pl.BlockSpec((tm, w), lambda i: (i, 0))
    full = lambda r, c: pl.BlockSpec((r, c), lambda i: (0, 0))
    return pl.pallas_call(
        _oproj_kernel,
        grid=(m // tm,),
        in_specs=[row(ka), row(kb), full(ka, d), full(kb, d), row(d), full(1, d)],
        out_specs=[row(d), row(d)],
        out_shape=[jax.ShapeDtypeStruct((m, d), F32), jax.ShapeDtypeStruct((m, d), BF16)],
        compiler_params=_params(("parallel",)),
        name="oproj",
    )(og, om, wa, wb, x, norm_w.reshape(1, d))


def _ffn_kernel(hn_ref, wg_ref, wu_ref, wd_ref, h_ref, nw_ref, h2_ref, hn2_ref, acc_ref):
    f = pl.program_id(1)

    @pl.when(f == 0)
    def _():
        acc_ref[...] = jnp.zeros_like(acc_ref)

    hn = hn_ref[...]
    g = jnp.dot(hn, wg_ref[...], preferred_element_type=F32)
    u = jnp.dot(hn, wu_ref[...], preferred_element_type=F32)
    a = (g * _sigmoid(g) * u).astype(BF16)
    acc_ref[...] += jnp.dot(a, wd_ref[...], preferred_element_type=F32)

    @pl.when(f == pl.num_programs(1) - 1)
    def _():
        h2 = h_ref[...] + acc_ref[...]
        h2_ref[...] = h2
        n = h2 * lax.rsqrt(jnp.mean(h2 * h2, axis=-1, keepdims=True) + RMS_EPS)
        hn2_ref[...] = (n * nw_ref[...]).astype(hn2_ref.dtype)


def _ffn(hn, wg, wu, wd, h, norm_w, tm, tf):
    m, d = h.shape
    dff = wg.shape[1]
    row = pl.BlockSpec((tm, d), lambda i, f: (i, 0))
    return pl.pallas_call(
        _ffn_kernel,
        grid=(m // tm, dff // tf),
        in_specs=[row, pl.BlockSpec((d, tf), lambda i, f: (0, f)), pl.BlockSpec((d, tf), lambda i, f: (0, f)),
                  pl.BlockSpec((tf, d), lambda i, f: (f, 0)), row, pl.BlockSpec((1, d), lambda i, f: (0, 0))],
        out_specs=[row, row],
        out_shape=[jax.ShapeDtypeStruct((m, d), F32), jax.ShapeDtypeStruct((m, d), BF16)],
        scratch_shapes=[pltpu.VMEM((tm, d), F32)],
        compiler_params=_params(("parallel", "arbitrary")),
        name="ffn",
    )(hn, wg, wu, wd, h, norm_w.reshape(1, d))


def _ple_kernel(hn_ref, wg_ref, p_ref, wp_ref, h_ref, o_ref):
    gate = _sigmoid(jnp.dot(hn_ref[...], wg_ref[...], preferred_element_type=F32))
    proj = jnp.dot(p_ref[...].astype(BF16), wp_ref[...], preferred_element_type=F32)
    o_ref[...] = h_ref[...] + gate * proj


def _ple(hn, wg, p, wp, h, tm, tn):
    m, d = h.shape
    kp = p.shape[1]
    return pl.pallas_call(
        _ple_kernel,
        grid=(m // tm, d // tn),
        in_specs=[pl.BlockSpec((tm, d), lambda i, j: (i, 0)), pl.BlockSpec((d, tn), lambda i, j: (0, j)),
                  pl.BlockSpec((tm, kp), lambda i, j: (i, 0)), pl.BlockSpec((kp, tn), lambda i, j: (0, j)),
                  pl.BlockSpec((tm, tn), lambda i, j: (i, j))],
        out_specs=pl.BlockSpec((tm, tn), lambda i, j: (i, j)),
        out_shape=jax.ShapeDtypeStruct((m, d), F32),
        compiler_params=_params(("parallel", "arbitrary")),
        name="ple",
    )(hn, wg, p, wp, h)


def _layer(h, p, attn_norm, w_in, conv_w, a_log, dt_bias, gdn_norm, q_norm, k_norm, w_o, ffn_norm,
           w_gate, w_up, w_down, ple_norm, w_ple_gate, w_ple_proj):
    b, t, d = h.shape
    m = b * t
    x2 = h.reshape(m, d)
    tm = min(1024, m)

    o_ba = 4 * GDN_W
    o_mq = o_ba + 2 * GDN_HEADS
    w_in16 = w_in.astype(BF16)
    w_gdn = w_in16[:, :o_ba]
    w_ba = jnp.pad(w_in16[:, o_ba:o_mq], ((0, 0), (0, LANES - 2 * GDN_HEADS)))
    w_mq = w_in16[:, o_mq:o_mq + MOBA_W]
    w_mk = w_in16[:, o_mq + MOBA_W:o_mq + 2 * MOBA_W]
    w_mv = w_in16[:, o_mq + 2 * MOBA_W:o_mq + 3 * MOBA_W]

    xn = _rmsnorm(x2, attn_norm, min(512, m))
    gproj = _proj(xn, w_gdn, F32, tm, 1024, "proj_gdn")
    ba = _proj(xn, w_ba, F32, tm, LANES, "proj_ba")
    mq = _proj_moba_q(xn, w_mq, q_norm, tm, 512)
    kaug, kmean = _proj_moba_k(xn, w_mk, k_norm, tm, 512, t // MOBA_BLOCK)
    mv = _proj(xn, w_mv, BF16, tm, 512, "proj_moba_v")

    o_gdn = _gdn_all(gproj.reshape(b, t, 4 * GDN_W), ba.reshape(b, t, LANES), conv_w.T, a_log, dt_bias,
                     gdn_norm, min(256, t))
    k_norm_bound = jnp.full((1, LANES), MOBA_DH ** 0.5, F32) * jnp.max(jnp.abs(k_norm))
    qaug = _moba_select(mq.reshape(b, t, MOBA_W).swapaxes(1, 2), kmean.reshape(b, t // MOBA_BLOCK, MOBA_W),
                        k_norm_bound, min(2048, t))
    o_moba = _moba(qaug, kaug.reshape(b, t, 2 * MOBA_W), mv.reshape(b, t, MOBA_W).swapaxes(1, 2))

    w_o16 = w_o.astype(BF16)
    h1, hn = _oproj(o_gdn.reshape(m, GDN_W), o_moba.reshape(m, MOBA_W), w_o16[:GDN_W], w_o16[GDN_W:],
                    x2, ffn_norm, min(512, m))
    h2, hn2 = _ffn(hn, w_gate.astype(BF16), w_up.astype(BF16), w_down.astype(BF16), h1, ple_norm,
                   min(512, m), 512)
    h3 = _ple(hn2, w_ple_gate.astype(BF16), p.reshape(m, PLE_DIM), w_ple_proj.astype(BF16), h2, tm, 512)
    return h3.reshape(b, t, d)


def kernel(x, p, attn_norm, w_in, conv_w, A_log, dt_bias, gdn_norm, q_norm, k_norm, w_o, ffn_norm,
           w_gate, w_up, w_down, ple_norm, w_ple_gate, w_ple_proj):
    h = x
    for i in range(p.shape[0]):
        h = _layer(h, p[i], attn_norm[i], w_in[i], conv_w[i], A_log[i], dt_bias[i], gdn_norm[i],
                   q_norm[i], k_norm[i], w_o[i], ffn_norm[i], w_gate[i], w_up[i], w_down[i],
                   ple_norm[i], w_ple_gate[i], w_ple_proj[i])
    return h
```

```python
import functools

import jax
import jax.numpy as jnp
from jax import lax
from jax.experimental import pallas as pl
from jax.experimental.pallas import tpu as pltpu

D_MODEL = 2048
PLE_DIM = 256
GDN_HEADS = 8
GDN_DK = 128
GDN_DV = 128
GDN_CONV = 4
GDN_CHUNK = 64
MOBA_HEADS = 8
MOBA_DH = 128
MOBA_BLOCK = 256
MOBA_TOPK = 3
RMS_EPS = 1e-6
GDN_W = GDN_HEADS * GDN_DK
MOBA_W = MOBA_HEADS * MOBA_DH
LANES = 128
LOG2E = 1.4426950408889634
MASK_NEG = -1e30

F32 = jnp.float32
BF16 = jnp.bfloat16

VMEM_LIMIT = 56 * 1024 * 1024


def _params(sem):
    return pltpu.CompilerParams(dimension_semantics=sem, vmem_limit_bytes=VMEM_LIMIT)


def _sigmoid(x):
    return 1.0 / (1.0 + jnp.exp(-x))


def _dot_t(a, b):
    return lax.dot_general(a, b, (((1,), (1,)), ((), ())), preferred_element_type=F32)


def _bdot(a, b):
    return jnp.dot(a.astype(BF16), b.astype(BF16), preferred_element_type=F32)


def _rmsnorm_kernel(x_ref, w_ref, o_ref):
    x = x_ref[...]
    y = x * lax.rsqrt(jnp.mean(x * x, axis=-1, keepdims=True) + RMS_EPS)
    o_ref[...] = (y * w_ref[...]).astype(o_ref.dtype)


def _rmsnorm(x, w, tm):
    m, d = x.shape
    return pl.pallas_call(
        _rmsnorm_kernel,
        grid=(m // tm,),
        in_specs=[pl.BlockSpec((tm, d), lambda i: (i, 0)), pl.BlockSpec((1, d), lambda i: (0, 0))],
        out_specs=pl.BlockSpec((tm, d), lambda i: (i, 0)),
        out_shape=jax.ShapeDtypeStruct((m, d), BF16),
        compiler_params=_params(("parallel",)),
        name="rmsnorm_cast",
    )(x, w.reshape(1, d))


def _proj_kernel(x_ref, w_ref, o_ref):
    o_ref[...] = jnp.dot(x_ref[...], w_ref[...], preferred_element_type=F32).astype(o_ref.dtype)


def _proj(x, w, out_dtype, tm, tn, name):
    m, k = x.shape
    n = w.shape[1]
    return pl.pallas_call(
        _proj_kernel,
        grid=(m // tm, n // tn),
        in_specs=[pl.BlockSpec((tm, k), lambda i, j: (i, 0)), pl.BlockSpec((k, tn), lambda i, j: (0, j))],
        out_specs=pl.BlockSpec((tm, tn), lambda i, j: (i, j)),
        out_shape=jax.ShapeDtypeStruct((m, n), out_dtype),
        compiler_params=_params(("parallel", "arbitrary")),
        name=name,
    )(x, w)


def _head_rmsnorm(y, gain, scale):
    outs = []
    for h in range(y.shape[1] // LANES):
        yh = y[:, h * LANES:(h + 1) * LANES]
        r = lax.rsqrt(jnp.mean(yh * yh, axis=-1, keepdims=True) + RMS_EPS)
        outs.append(yh * r * gain[:, h * LANES:(h + 1) * LANES] * scale)
    return jnp.concatenate(outs, axis=1)


def _proj_qnorm_kernel(x_ref, w_ref, g_ref, o_ref, *, scale):
    y = jnp.dot(x_ref[...], w_ref[...], preferred_element_type=F32)
    o_ref[...] = _head_rmsnorm(y, g_ref[...], scale).astype(o_ref.dtype)


def _proj_knorm_kernel(x_ref, w_ref, g_ref, o_ref, km_ref, *, nb_seq):
    y = jnp.dot(x_ref[...], w_ref[...], preferred_element_type=F32)
    yn = _head_rmsnorm(y, g_ref[...], 1.0)
    tm = yn.shape[0]
    row = lax.broadcasted_iota(jnp.int32, (tm, LANES), 0) + pl.program_id(0) * tm
    lane = lax.broadcasted_iota(jnp.int32, (tm, LANES), 1)
    onehot = jnp.where((lane == lax.rem(row // MOBA_BLOCK, nb_seq)) | (lane == LANES - 1), 1.0, 0.0)
    onehot = onehot.astype(o_ref.dtype)
    yb = yn.astype(o_ref.dtype)
    parts = []
    for h in range(yn.shape[1] // LANES):
        parts += [yb[:, h * LANES:(h + 1) * LANES], onehot]
    o_ref[...] = jnp.concatenate(parts, axis=1)
    for r in range(tm // MOBA_BLOCK):
        blk = yn[r * MOBA_BLOCK:(r + 1) * MOBA_BLOCK]
        km_ref[r] = jnp.mean(blk, axis=0, keepdims=True)


def _proj_moba_q(x, w, gain, tm, tn):
    m, k = x.shape
    n = w.shape[1]
    g = jnp.tile(gain.reshape(1, MOBA_DH), (1, n // MOBA_DH))
    return pl.pallas_call(
        functools.partial(_proj_qnorm_kernel, scale=MOBA_DH ** -0.5 * LOG2E),
        grid=(m // tm, n // tn),
        in_specs=[pl.BlockSpec((tm, k), lambda i, j: (i, 0)), pl.BlockSpec((k, tn), lambda i, j: (0, j)),
                  pl.BlockSpec((1, tn), lambda i, j: (0, j))],
        out_specs=pl.BlockSpec((tm, tn), lambda i, j: (i, j)),
        out_shape=jax.ShapeDtypeStruct((m, n), BF16),
        compiler_params=_params(("parallel", "arbitrary")),
        name="proj_moba_q",
    )(x, w, g)


def _proj_moba_k(x, w, gain, tm, tn, nb_seq):
    m, k = x.shape
    n = w.shape[1]
    g = jnp.tile(gain.reshape(1, MOBA_DH), (1, n // MOBA_DH))
    nb = tm // MOBA_BLOCK
    return pl.pallas_call(
        functools.partial(_proj_knorm_kernel, nb_seq=nb_seq),
        grid=(m // tm, n // tn),
        in_specs=[pl.BlockSpec((tm, k), lambda i, j: (i, 0)), pl.BlockSpec((k, tn), lambda i, j: (0, j)),
                  pl.BlockSpec((1, tn), lambda i, j: (0, j))],
        out_specs=[pl.BlockSpec((tm, 2 * tn), lambda i, j: (i, j)),
                   pl.BlockSpec((nb, 1, tn), lambda i, j: (i, 0, j))],
        out_shape=[jax.ShapeDtypeStruct((m, 2 * n), BF16),
                   jax.ShapeDtypeStruct((m // MOBA_BLOCK, 1, n), F32)],
        compiler_params=_params(("parallel", "arbitrary")),
        name="proj_moba_k",
    )(x, w, g)


def _gdn_all_kernel(x_ref, ba_ref, cw_ref, alog_ref, dtb_ref, gn_ref, o_ref, s_ref, tail_ref, qkv_s, *, tb):
    t = pl.program_id(0)
    C = GDN_CHUNK
    nc = tb // C
    nh = GDN_HEADS
    nbat = x_ref.shape[0]

    @pl.when(t == 0)
    def _():
        s_ref[...] = jnp.zeros_like(s_ref)
        tail_ref[...] = jnp.zeros_like(tail_ref)

    def conv_group(gi, l2_scale):
        off = pl.multiple_of(gi * LANES, LANES)
        w = cw_ref[:, pl.ds(off, LANES)]
        for bi in range(nbat):
            x = x_ref[bi, :, pl.ds(off, LANES)]
            xp = jnp.concatenate([tail_ref[bi * 8:(bi + 1) * 8, pl.ds(off, LANES)], x], axis=0)
            y = x * w[GDN_CONV - 1:GDN_CONV]
            for j in range(GDN_CONV - 1):
                y = y + pltpu.roll(xp, GDN_CONV - 1 - j, axis=0)[8:8 + tb] * w[j:j + 1]
            tail_ref[bi * 8:(bi + 1) * 8, pl.ds(off, LANES)] = x[tb - 8:tb]
            y = y * _sigmoid(y)
            if l2_scale is not None:
                y = y * (lax.rsqrt(jnp.sum(y * y, axis=-1, keepdims=True) + 1e-6) * l2_scale)
            qkv_s[bi * tb:(bi + 1) * tb, pl.ds(off, LANES)] = y

    for g0, l2_scale in ((0, GDN_DK ** -0.5), (nh, 1.0), (2 * nh, None)):
        lax.fori_loop(g0, g0 + nh, lambda gi, carry, sc=l2_scale: conv_group(gi, sc), None)

    row_in_chunk = lax.broadcasted_iota(jnp.int32, (tb, LANES), 0) & (C - 1)
    beta_all, gc, egc, kdsc, egl, gc_t = [], [], [], [], [], []
    for bi in range(nbat):
        ba = ba_ref[bi]
        beta_all.append(_sigmoid(ba))
        sp_in = ba + dtb_ref[...]
        softplus = jnp.maximum(sp_in, 0.0) + jnp.log1p(jnp.exp(-jnp.abs(sp_in)))
        g = -(jnp.exp(alog_ref[...]) * softplus)
        sh = 1
        while sh < C:
            g = g + jnp.where(row_in_chunk >= sh, pltpu.roll(g, sh, axis=0), 0.0)
            sh *= 2
        gl_rows = [g[(c + 1) * C - 1:(c + 1) * C] for c in range(nc)]
        gl_b = jnp.concatenate([jnp.broadcast_to(r, (C, LANES)) for r in gl_rows], axis=0)
        gc.append(g)
        egc.append(jnp.exp(g))
        kdsc.append(jnp.exp(gl_b - g))
        egl.append([jnp.exp(r) for r in gl_rows])
        gc_t.append(g.T)

    ri = lax.broadcasted_iota(jnp.int32, (C, C), 0)
    ci = lax.broadcasted_iota(jnp.int32, (C, C), 1)
    tril = ri >= ci
    strict = ri > ci
    eye_f = (ri == ci).astype(F32)
    lvl_masks = []
    s = 1
    while s < C:
        sh2 = s.bit_length()
        lvl_masks.append((ri >> sh2 == ci >> sh2) & ((ri & (2 * s - 1)) >= s) & ((ci & (2 * s - 1)) < s))
        s *= 2
    gn = gn_ref[...]

    seqs = [(bi, h) for bi in range(nbat) for h in range(nh)]
    ids = range(len(seqs))
    states = [s_ref[i] for i in ids]
    grp = lambda g, h: slice((g * nh + h) * LANES, (g * nh + h + 1) * LANES)
    for c in range(nc):
        rows = slice(c * C, (c + 1) * C)
        srow = lambda bi: slice(bi * tb + c * C, bi * tb + (c + 1) * C)
        gate_col = lambda arr, bi, h: arr[bi][rows, nh + h:nh + h + 1]
        qc = [qkv_s[srow(bi), grp(0, h)] for bi, h in seqs]
        kc = [qkv_s[srow(bi), grp(1, h)] for bi, h in seqs]
        vc = [qkv_s[srow(bi), grp(2, h)] for bi, h in seqs]
        bcol = [beta_all[bi][rows, h:h + 1] for bi, h in seqs]
        ecol = [gate_col(egc, bi, h) for bi, h in seqs]
        decay = [jnp.exp(jnp.where(tril, gate_col(gc, bi, h) - gc_t[bi][nh + h:nh + h + 1, rows], -jnp.inf))
                 for bi, h in seqs]
        k_beta = [kc[i] * bcol[i] for i in ids]
        st = [_dot_t(jnp.concatenate([k_beta[i], qc[i]], axis=0).astype(BF16), kc[i].astype(BF16))
              for i in ids]
        a_mat = [jnp.where(strict, st[i][:C] * decay[i], 0.0) for i in ids]
        qk = [jnp.where(tril, st[i][C:] * decay[i], 0.0) for i in ids]
        t_inv = [eye_f - jnp.where(lvl_masks[0], a_mat[i], 0.0) for i in ids]
        for msk in lvl_masks[1:]:
            x_mid = [_bdot(jnp.where(msk, a_mat[i], 0.0), t_inv[i]) for i in ids]
            y_mid = [_bdot(t_inv[i], x_mid[i]) for i in ids]
            t_inv = [t_inv[i] - y_mid[i] for i in ids]
        uw = [_bdot(t_inv[i], jnp.concatenate([vc[i] * bcol[i], k_beta[i] * ecol[i]], axis=1))
              for i in ids]
        wq = [_bdot(jnp.concatenate([uw[i][:, LANES:], qc[i] * ecol[i]], axis=0), states[i])
              for i in ids]
        v_new = [uw[i][:, :LANES] - wq[i][:C] for i in ids]
        o = [wq[i][C:] + _bdot(qk[i], v_new[i]) for i in ids]
        k_dec = [kc[i] * gate_col(kdsc, bi, h) for i, (bi, h) in enumerate(seqs)]
        states = [states[i] * egl[bi][c][:, nh + h:nh + h + 1] + lax.dot_general(
            k_dec[i].astype(BF16), v_new[i].astype(BF16), (((0,), (0,)), ((), ())),
            preferred_element_type=F32) for i, (bi, h) in enumerate(seqs)]
        for i, (bi, h) in enumerate(seqs):
            on = o[i] * lax.rsqrt(jnp.mean(o[i] * o[i], axis=-1, keepdims=True) + RMS_EPS) * gn
            zc = x_ref[bi, rows, grp(3, h)]
            o_ref[bi, rows, grp(0, h)] = (on * (zc * _sigmoid(zc))).astype(o_ref.dtype)
    for i in ids:
        s_ref[i] = states[i]


def _gdn_all(proj, ba, conv_wt, a_log, dt_bias, gnorm, tb):
    b, t, width = proj.shape
    nh = GDN_HEADS
    lane_vec = lambda v: jnp.pad(v.reshape(1, nh), ((0, 0), (nh, LANES - 2 * nh)))
    full = lambda shape: pl.BlockSpec(shape, lambda ti: (0,) * len(shape))
    return pl.pallas_call(
        functools.partial(_gdn_all_kernel, tb=tb),
        grid=(t // tb,),
        in_specs=[pl.BlockSpec((b, tb, width), lambda ti: (0, ti, 0)),
                  pl.BlockSpec((b, tb, LANES), lambda ti: (0, ti, 0)),
                  full((GDN_CONV, 3 * GDN_W)), full((1, LANES)), full((1, LANES)), full((1, GDN_DV))],
        out_specs=pl.BlockSpec((b, tb, GDN_W), lambda ti: (0, ti, 0)),
        out_shape=jax.ShapeDtypeStruct((b, t, GDN_W), BF16),
        scratch_shapes=[pltpu.VMEM((b * nh, GDN_DK, GDN_DV), F32), pltpu.VMEM((b * 8, 3 * GDN_W), F32),
                        pltpu.VMEM((b * tb, 3 * GDN_W), F32)],
        compiler_params=_params(("arbitrary",)),
        name="gdn",
    )(proj, ba, conv_wt, lane_vec(a_log), lane_vec(dt_bias), gnorm.reshape(1, GDN_DV))


MOBA_QW = 2 * MOBA_BLOCK
MOBA_ROWS_L = 16
MOBA_BOUND_SLACK = 1.02
MOBA_SAFE_LOG2_RANGE = 100.0


def _moba_select_kernel(qt_ref, km_ref, kn_ref, qa_ref, *, nb, tq):
    nbp = -(-nb // 16) * 16
    qt = qt_ref[0]
    km = km_ref[0].astype(BF16)
    if nbp > nb:
        km = jnp.concatenate([km, jnp.zeros((nbp - nb, LANES), BF16)], axis=0)
    gate = jnp.dot(km, qt, preferred_element_type=F32)
    row = lax.broadcasted_iota(jnp.int32, (nbp, tq), 0)
    qblk = (lax.broadcasted_iota(jnp.int32, (nbp, tq), 1) + pl.program_id(2) * tq) // MOBA_BLOCK
    past = row < qblk
    g = jnp.where(past, gate, -jnp.inf)
    sel = row < 0
    for _ in range(MOBA_TOPK):
        m = jnp.max(g, axis=0, keepdims=True)
        idx = jnp.min(jnp.where(g == m, row, nbp), axis=0, keepdims=True)
        hit = row == idx
        sel = sel | (hit & past)
        g = jnp.where(hit, -jnp.inf, g)
    q32 = qt.astype(F32)
    bound = jnp.sqrt(jnp.sum(q32 * q32, axis=0, keepdims=True)) * kn_ref[:, 0:1] * MOBA_BOUND_SLACK
    qa_ref[0, 0:LANES, :] = qt
    qa_ref[0, LANES:LANES + nbp, :] = jnp.where(sel, 0.0, MASK_NEG).astype(BF16)
    tail_row = lax.broadcasted_iota(jnp.int32, (LANES - nbp, tq), 0)
    qa_ref[0, LANES + nbp:, :] = jnp.where(tail_row == LANES - nbp - 1, -bound, 0.0).astype(BF16)


def _moba_select(qt, kmean, k_norm_bound, tq):
    b, _, t = qt.shape
    nb = t // MOBA_BLOCK
    assert -(-nb // 16) * 16 < LANES
    return pl.pallas_call(
        functools.partial(_moba_select_kernel, nb=nb, tq=tq),
        grid=(b, MOBA_HEADS, t // tq),
        in_specs=[pl.BlockSpec((1, LANES, tq), lambda bi, hi, qi: (bi, hi, qi)),
                  pl.BlockSpec((1, nb, LANES), lambda bi, hi, qi: (bi, 0, hi)),
                  pl.BlockSpec((1, LANES), lambda bi, hi, qi: (0, 0))],
        out_specs=pl.BlockSpec((1, 2 * LANES, tq), lambda bi, hi, qi: (bi, hi, qi)),
        out_shape=jax.ShapeDtypeStruct((b, 2 * MOBA_W, t), BF16),
        compiler_params=_params(("parallel", "parallel", "parallel")),
        name="moba_select",
    )(qt, kmean, k_norm_bound)


def _moba_kernel(qa_ref, ka_ref, vt_ref, o_ref, acc_ref, m_ref, sa_ref, sb_ref, xa_ref, xb_ref, *, nb, group):
    i2 = pl.program_id(2)
    BLK = MOBA_BLOCK

    def pv(p, start, width):
        lhs = jnp.concatenate([vt_ref[0, :, pl.ds(start, width)], jnp.ones((MOBA_ROWS_L, width), BF16)], axis=0)
        return jnp.dot(lhs, p, preferred_element_type=F32)

    key_i = lax.broadcasted_iota(jnp.int32, (BLK, BLK), 0)
    qry_i = lax.broadcasted_iota(jnp.int32, (BLK, BLK), 1)
    starts = [pl.multiple_of((2 * i2 + hf) * BLK, BLK) for hf in range(2)]
    s_own = [jnp.dot(ka_ref[0, pl.ds(starts[hf], BLK), 0:LANES], qa_ref[0, 0:LANES, hf * BLK:(hf + 1) * BLK],
                     preferred_element_type=F32) for hf in range(2)]
    s_own = [s_own[hf] + qa_ref[0, 2 * LANES - 1:2 * LANES, hf * BLK:(hf + 1) * BLK].astype(F32) for hf in range(2)]
    s_own = [jnp.where(key_i <= qry_i, s, MASK_NEG) for s in s_own]
    m_own = [jnp.max(s, axis=0, keepdims=True) for s in s_own]
    p_own = [jnp.exp2(s_own[hf] - m_own[hf]).astype(BF16) for hf in range(2)]
    for hf in range(2):
        m_ref[:, hf * BLK:(hf + 1) * BLK] = m_own[hf]
        acc_ref[:, hf * BLK:(hf + 1) * BLK] = pv(p_own[hf], starts[hf], BLK)

    span = group * BLK
    n_pairs = (2 * i2 + 2 * group) // (2 * group)
    last = nb // group - 1
    halves = [slice(hf * BLK, (hf + 1) * BLK) for hf in range(2)]

    def qk(g_idx, dst_ref, mx_dst, hs):
        start = pl.multiple_of(g_idx * span, span)
        s = jnp.dot(ka_ref[0, pl.ds(start, span), :], qa_ref[0, :, hs], preferred_element_type=F32)
        dst_ref[:, hs] = s
        mx_dst[:, hs] = jnp.max(s, axis=0, keepdims=True)

    def softmax_pv(src_ref, mx_src, g_idx, hs):
        m_old = m_ref[:, hs]
        m_new = jnp.maximum(m_old, mx_src[:, hs])
        alpha = jnp.exp2(m_old - m_new)
        m_ref[:, hs] = m_new
        p = jnp.exp2(src_ref[:, hs] - m_new).astype(BF16)
        acc_ref[:, hs] = alpha * acc_ref[:, hs] + pv(p, pl.multiple_of(g_idx * span, span), span)

    for hs in halves:
        qk(0, sa_ref, xa_ref, hs)

    def body(jj, carry):
        for hs in halves:
            qk(2 * jj + 1, sb_ref, xb_ref, hs)
            softmax_pv(sa_ref, xa_ref, 2 * jj, hs)
        for hs in halves:
            qk(jnp.minimum(2 * jj + 2, last), sa_ref, xa_ref, hs)
            softmax_pv(sb_ref, xb_ref, 2 * jj + 1, hs)
        return carry

    lax.fori_loop(0, n_pairs, body, 0)
    acc = acc_ref[...]
    o = acc[0:LANES] * (1.0 / acc[LANES:LANES + 1])
    o_ref[0] = o.T.astype(o_ref.dtype)


def _moba_bounded_kernel(qa_ref, ka_ref, vt_ref, o_ref, acc_ref, pa_ref, pb_ref, *, nb, group):
    i2 = pl.program_id(2)
    BLK = MOBA_BLOCK

    def pv(p, start, width):
        lhs = jnp.concatenate([vt_ref[0, :, pl.ds(start, width)], jnp.ones((MOBA_ROWS_L, width), BF16)], axis=0)
        return jnp.dot(lhs, p, preferred_element_type=F32)

    halves = [slice(hf * BLK, (hf + 1) * BLK) for hf in range(2)]
    key_i = lax.broadcasted_iota(jnp.int32, (BLK, BLK), 0)
    qry_i = lax.broadcasted_iota(jnp.int32, (BLK, BLK), 1)
    own = []
    for hf, hs in enumerate(halves):
        start = pl.multiple_of((2 * i2 + hf) * BLK, BLK)
        s = jnp.dot(ka_ref[0, pl.ds(start, BLK), 0:LANES], qa_ref[0, 0:LANES, hs], preferred_element_type=F32)
        s = s + qa_ref[0, 2 * LANES - 1:2 * LANES, hs].astype(F32)
        own.append((jnp.exp2(jnp.where(key_i <= qry_i, s, MASK_NEG)).astype(BF16), start))

    span = group * BLK
    n_pairs = (2 * i2 + 2 * group) // (2 * group)
    last = nb // group - 1

    def qk_exp(g_idx, p_dst, hs):
        start = pl.multiple_of(g_idx * span, span)
        s = jnp.dot(ka_ref[0, pl.ds(start, span), :], qa_ref[0, :, hs], preferred_element_type=F32)
        p_dst[:, hs] = jnp.exp2(s).astype(BF16)

    def pv_acc(p_src, g_idx, hs):
        acc_ref[:, hs] += pv(p_src[:, hs], pl.multiple_of(g_idx * span, span), span)

    for hs in halves:
        qk_exp(0, pa_ref, hs)
    for (p, start), hs in zip(own, halves):
        acc_ref[:, hs] = pv(p, start, BLK)

    def body(jj, carry):
        for hs in halves:
            qk_exp(2 * jj + 1, pb_ref, hs)
            pv_acc(pa_ref, 2 * jj, hs)
        for hs in halves:
            qk_exp(jnp.minimum(2 * jj + 2, last), pa_ref, hs)
            pv_acc(pb_ref, 2 * jj + 1, hs)
        return carry

    lax.fori_loop(0, n_pairs, body, 0)
    acc = acc_ref[...]
    o = acc[0:LANES] * (1.0 / acc[LANES:LANES + 1])
    o_ref[0] = o.T.astype(o_ref.dtype)


def _moba(qaug, kaug, vt):
    b, _, t = vt.shape
    nb = t // MOBA_BLOCK
    group = 4 if nb % 8 == 0 else 2
    assert nb % (2 * group) == 0
    span = group * MOBA_BLOCK

    def call(body, scratch, name):
        return pl.pallas_call(
            functools.partial(body, nb=nb, group=group),
            grid=(b, MOBA_HEADS, t // MOBA_QW),
            in_specs=[pl.BlockSpec((1, 2 * LANES, MOBA_QW), lambda bi, hi, qi: (bi, hi, qi)),
                      pl.BlockSpec((1, t, 2 * LANES), lambda bi, hi, qi: (bi, 0, hi)),
                      pl.BlockSpec((1, LANES, t), lambda bi, hi, qi: (bi, hi, 0))],
            out_specs=pl.BlockSpec((1, MOBA_QW, LANES), lambda bi, hi, qi: (bi, qi, hi)),
            out_shape=jax.ShapeDtypeStruct((b, t, MOBA_W), BF16),
            scratch_shapes=[pltpu.VMEM((LANES + MOBA_ROWS_L, MOBA_QW), F32)] + scratch,
            compiler_params=_params(("parallel", "parallel", "arbitrary")),
            name=name,
        )

    online = call(_moba_kernel,
                  [pltpu.VMEM((1, MOBA_QW), F32),
                   pltpu.VMEM((span, MOBA_QW), F32), pltpu.VMEM((span, MOBA_QW), F32),
                   pltpu.VMEM((1, MOBA_QW), F32), pltpu.VMEM((1, MOBA_QW), F32)], "moba")
    bounded = call(_moba_bounded_kernel,
                   [pltpu.VMEM((span, MOBA_QW), BF16), pltpu.VMEM((span, MOBA_QW), BF16)], "moba_bounded")
    max_bound = -jnp.min(qaug[:, 2 * LANES - 1::2 * LANES, :].astype(F32))
    return lax.cond(2.0 * max_bound < MOBA_SAFE_LOG2_RANGE, bounded, online, qaug, kaug, vt)


def _oproj_kernel(og_ref, om_ref, wa_ref, wb_ref, x_ref, nw_ref, h_ref, hn_ref):
    y = jnp.dot(og_ref[...], wa_ref[...], preferred_element_type=F32)
    y = y + jnp.dot(om_ref[...], wb_ref[...], preferred_element_type=F32)
    h = x_ref[...] + y
    h_ref[...] = h
    hn = h * lax.rsqrt(jnp.mean(h * h, axis=-1, keepdims=True) + RMS_EPS)
    hn_ref[...] = (hn * nw_ref[...]).astype(hn_ref.dtype)


def _oproj(og, om, wa, wb, x, norm_w, tm):
    m, d = x.shape
    ka, kb = og.shape[1], om.shape[1]
    row = lambda w: pl.BlockSpec((tm, w), lambda i: (i, 0))
    full = lambda r, c: pl.BlockSpec((r, c), lambda i: (0, 0))
    return pl.pallas_call(
        _oproj_kernel,
        grid=(m // tm,),
        in_specs=[row(ka), row(kb), full(ka, d), full(kb, d), row(d), full(1, d)],
        out_specs=[row(d), row(d)],
        out_shape=[jax.ShapeDtypeStruct((m, d), F32), jax.ShapeDtypeStruct((m, d), BF16)],
        compiler_params=_params(("parallel",)),
        name="oproj",
    )(og, om, wa, wb, x, norm_w.reshape(1, d))


def _ffn_kernel(hn_ref, wg_ref, wu_ref, wd_ref, h_ref, nw_ref, h2_ref, hn2_ref, acc_ref):
    f = pl.program_id(1)

    @pl.when(f == 0)
    def _():
        acc_ref[...] = jnp.zeros_like(acc_ref)

    hn = hn_ref[...]
    g = jnp.dot(hn, wg_ref[...], preferred_element_type=F32)
    u = jnp.dot(hn, wu_ref[...], preferred_element_type=F32)
    a = (g * _sigmoid(g) * u).astype(BF16)
    acc_ref[...] += jnp.dot(a, wd_ref[...], preferred_element_type=F32)

    @pl.when(f == pl.num_programs(1) - 1)
    def _():
        h2 = h_ref[...] + acc_ref[...]
        h2_ref[...] = h2
        n = h2 * lax.rsqrt(jnp.mean(h2 * h2, axis=-1, keepdims=True) + RMS_EPS)
        hn2_ref[...] = (n * nw_ref[...]).astype(hn2_ref.dtype)


def _ffn(hn, wg, wu, wd, h, norm_w, tm, tf):
    m, d = h.shape
    dff = wg.shape[1]
    row = pl.BlockSpec((tm, d), lambda i, f: (i, 0))
    return pl.pallas_call(
        _ffn_kernel,
        grid=(m // tm, dff // tf),
        in_specs=[row, pl.BlockSpec((d, tf), lambda i, f: (0, f)), pl.BlockSpec((d, tf), lambda i, f: (0, f)),
                  pl.BlockSpec((tf, d), lambda i, f: (f, 0)), row, pl.BlockSpec((1, d), lambda i, f: (0, 0))],
        out_specs=[row, row],
        out_shape=[jax.ShapeDtypeStruct((m, d), F32), jax.ShapeDtypeStruct((m, d), BF16)],
        scratch_shapes=[pltpu.VMEM((tm, d), F32)],
        compiler_params=_params(("parallel", "arbitrary")),
        name="ffn",
    )(hn, wg, wu, wd, h, norm_w.reshape(1, d))


def _ple_kernel(hn_ref, wg_ref, p_ref, wp_ref, h_ref, o_ref):
    gate = _sigmoid(jnp.dot(hn_ref[...], wg_ref[...], preferred_element_type=F32))
    proj = jnp.dot(p_ref[...].astype(BF16), wp_ref[...], preferred_element_type=F32)
    o_ref[...] = h_ref[...] + gate * proj


def _ple(hn, wg, p, wp, h, tm, tn):
    m, d = h.shape
    kp = p.shape[1]
    return pl.pallas_call(
        _ple_kernel,
        grid=(m // tm, d // tn),
        in_specs=[pl.BlockSpec((tm, d), lambda i, j: (i, 0)), pl.BlockSpec((d, tn), lambda i, j: (0, j)),
                  pl.BlockSpec((tm, kp), lambda i, j: (i, 0)), pl.BlockSpec((kp, tn), lambda i, j: (0, j)),
                  pl.BlockSpec((tm, tn), lambda i, j: (i, j))],
        out_specs=pl.BlockSpec((tm, tn), lambda i, j: (i, j)),
        out_shape=jax.ShapeDtypeStruct((m, d), F32),
        compiler_params=_params(("parallel", "arbitrary")),
        name="ple",
    )(hn, wg, p, wp, h)


def _layer(h, p, attn_norm, w_in, conv_w, a_log, dt_bias, gdn_norm, q_norm, k_norm, w_o, ffn_norm,
           w_gate, w_up, w_down, ple_norm, w_ple_gate, w_ple_proj):
    b, t, d = h.shape
    m = b * t
    x2 = h.reshape(m, d)
    tm = min(1024, m)

    o_ba = 4 * GDN_W
    o_mq = o_ba + 2 * GDN_HEADS
    w_in16 = w_in.astype(BF16)
    w_gdn = w_in16[:, :o_ba]
    w_ba = jnp.pad(w_in16[:, o_ba:o_mq], ((0, 0), (0, LANES - 2 * GDN_HEADS)))
    w_mq = w_in16[:, o_mq:o_mq + MOBA_W]
    w_mk = w_in16[:, o_mq + MOBA_W:o_mq + 2 * MOBA_W]
    w_mv = w_in16[:, o_mq + 2 * MOBA_W:o_mq + 3 * MOBA_W]

    xn = _rmsnorm(x2, attn_norm, min(512, m))
    gproj = _proj(xn, w_gdn, F32, tm, 1024, "proj_gdn")
    ba = _proj(xn, w_ba, F32, tm, LANES, "proj_ba")
    mq = _proj_moba_q(xn, w_mq, q_norm, tm, 1024)
    kaug, kmean = _proj_moba_k(xn, w_mk, k_norm, tm, 1024, t // MOBA_BLOCK)
    mv = _proj(xn, w_mv, BF16, tm, 1024, "proj_moba_v")

    o_gdn = _gdn_all(gproj.reshape(b, t, 4 * GDN_W), ba.reshape(b, t, LANES), conv_w.T, a_log, dt_bias,
                     gdn_norm, min(256, t))
    k_norm_bound = jnp.full((1, LANES), MOBA_DH ** 0.5, F32) * jnp.max(jnp.abs(k_norm))
    qaug = _moba_select(mq.reshape(b, t, MOBA_W).swapaxes(1, 2), kmean.reshape(b, t // MOBA_BLOCK, MOBA_W),
                        k_norm_bound, min(2048, t))
    o_moba = _moba(qaug, kaug.reshape(b, t, 2 * MOBA_W), mv.reshape(b, t, MOBA_W).swapaxes(1, 2))

    w_o16 = w_o.astype(BF16)
    h1, hn = _oproj(o_gdn.reshape(m, GDN_W), o_moba.reshape(m, MOBA_W), w_o16[:GDN_W], w_o16[GDN_W:],
                    x2, ffn_norm, min(512, m))
    h2, hn2 = _ffn(hn, w_gate.astype(BF16), w_up.astype(BF16), w_down.astype(BF16), h1, ple_norm,
                   min(512, m), 512)
    h3 = _ple(hn2, w_ple_gate.astype(BF16), p.reshape(m, PLE_DIM), w_ple_proj.astype(BF16), h2, tm, 1024)
    return h3.reshape(b, t, d)


def kernel(x, p, attn_norm, w_in, conv_w, A_log, dt_bias, gdn_norm, q_norm, k_norm, w_o, ffn_norm,
           w_gate, w_up, w_down, ple_norm, w_ple_gate, w_ple_proj):
    h = x
    for i in range(p.shape[0]):
        h = _layer(h, p[i], attn_norm[i], w_in[i], conv_w[i], A_log[i], dt_bias[i], gdn_norm[i],
                   q_norm[i], k_norm[i], w_o[i], ffn_norm[i], w_gate[i], w_up[i], w_down[i],
                   ple_norm[i], w_ple_gate[i], w_ple_proj[i])
    return h
```

```python
import functools

import jax
import jax.numpy as jnp
from jax import lax
from jax.experimental import pallas as pl
from jax.experimental.pallas import tpu as pltpu

D_MODEL = 2048
PLE_DIM = 256
GDN_HEADS = 8
GDN_DK = 128
GDN_DV = 128
GDN_CONV = 4
GDN_CHUNK = 64
MOBA_HEADS = 8
MOBA_DH = 128
MOBA_BLOCK = 256
MOBA_TOPK = 3
RMS_EPS = 1e-6
GDN_W = GDN_HEADS * GDN_DK
MOBA_W = MOBA_HEADS * MOBA_DH
LANES = 128
LOG2E = 1.4426950408889634
MASK_NEG = -1e30

F32 = jnp.float32
BF16 = jnp.bfloat16

VMEM_LIMIT = 56 * 1024 * 1024


def _params(sem):
    return pltpu.CompilerParams(dimension_semantics=sem, vmem_limit_bytes=VMEM_LIMIT)


def _sigmoid(x):
    return 1.0 / (1.0 + jnp.exp(-x))


def _dot_t(a, b):
    return lax.dot_general(a, b, (((1,), (1,)), ((), ())), preferred_element_type=F32)


def _bdot(a, b):
    return jnp.dot(a.astype(BF16), b.astype(BF16), preferred_element_type=F32)


def _rmsnorm_kernel(x_ref, w_ref, o_ref):
    x = x_ref[...]
    y = x * lax.rsqrt(jnp.mean(x * x, axis=-1, keepdims=True) + RMS_EPS)
    o_ref[...] = (y * w_ref[...]).astype(o_ref.dtype)


def _rmsnorm(x, w, tm):
    m, d = x.shape
    return pl.pallas_call(
        _rmsnorm_kernel,
        grid=(m // tm,),
        in_specs=[pl.BlockSpec((tm, d), lambda i: (i, 0)), pl.BlockSpec((1, d), lambda i: (0, 0))],
        out_specs=pl.BlockSpec((tm, d), lambda i: (i, 0)),
        out_shape=jax.ShapeDtypeStruct((m, d), BF16),
        compiler_params=_params(("parallel",)),
        name="rmsnorm_cast",
    )(x, w.reshape(1, d))


def _proj_kernel(x_ref, w_ref, o_ref):
    o_ref[...] = jnp.dot(x_ref[...], w_ref[...], preferred_element_type=F32).astype(o_ref.dtype)


def _proj(x, w, out_dtype, tm, tn, name):
    m, k = x.shape
    n = w.shape[1]
    return pl.pallas_call(
        _proj_kernel,
        grid=(m // tm, n // tn),
        in_specs=[pl.BlockSpec((tm, k), lambda i, j: (i, 0)), pl.BlockSpec((k, tn), lambda i, j: (0, j))],
        out_specs=pl.BlockSpec((tm, tn), lambda i, j: (i, j)),
        out_shape=jax.ShapeDtypeStruct((m, n), out_dtype),
        compiler_params=_params(("parallel", "arbitrary")),
        name=name,
    )(x, w)


def _head_rmsnorm(y, gain, scale):
    outs = []
    for h in range(y.shape[1] // LANES):
        yh = y[:, h * LANES:(h + 1) * LANES]
        r = lax.rsqrt(jnp.mean(yh * yh, axis=-1, keepdims=True) + RMS_EPS)
        outs.append(yh * r * gain[:, h * LANES:(h + 1) * LANES] * scale)
    return jnp.concatenate(outs, axis=1)


def _proj_qnorm_kernel(x_ref, w_ref, g_ref, o_ref, *, scale):
    y = jnp.dot(x_ref[...], w_ref[...], preferred_element_type=F32)
    o_ref[...] = _head_rmsnorm(y, g_ref[...], scale).astype(o_ref.dtype)


def _proj_knorm_kernel(x_ref, w_ref, g_ref, o_ref, km_ref, *, nb_seq):
    y = jnp.dot(x_ref[...], w_ref[...], preferred_element_type=F32)
    yn = _head_rmsnorm(y, g_ref[...], 1.0)
    tm = yn.shape[0]
    row = lax.broadcasted_iota(jnp.int32, (tm, LANES), 0) + pl.program_id(0) * tm
    lane = lax.broadcasted_iota(jnp.int32, (tm, LANES), 1)
    onehot = jnp.where((lane == lax.rem(row // MOBA_BLOCK, nb_seq)) | (lane == LANES - 1), 1.0, 0.0)
    onehot = onehot.astype(o_ref.dtype)
    yb = yn.astype(o_ref.dtype)
    parts = []
    for h in range(yn.shape[1] // LANES):
        parts += [yb[:, h * LANES:(h + 1) * LANES], onehot]
    o_ref[...] = jnp.concatenate(parts, axis=1)
    for r in range(tm // MOBA_BLOCK):
        blk = yn[r * MOBA_BLOCK:(r + 1) * MOBA_BLOCK]
        km_ref[r] = jnp.mean(blk, axis=0, keepdims=True)


def _proj_moba_q(x, w, gain, tm, tn):
    m, k = x.shape
    n = w.shape[1]
    g = jnp.tile(gain.reshape(1, MOBA_DH), (1, n // MOBA_DH))
    return pl.pallas_call(
        functools.partial(_proj_qnorm_kernel, scale=MOBA_DH ** -0.5 * LOG2E),
        grid=(m // tm, n // tn),
        in_specs=[pl.BlockSpec((tm, k), lambda i, j: (i, 0)), pl.BlockSpec((k, tn), lambda i, j: (0, j)),
                  pl.BlockSpec((1, tn), lambda i, j: (0, j))],
        out_specs=pl.BlockSpec((tm, tn), lambda i, j: (i, j)),
        out_shape=jax.ShapeDtypeStruct((m, n), BF16),
        compiler_params=_params(("parallel", "arbitrary")),
        name="proj_moba_q",
    )(x, w, g)


def _proj_moba_k(x, w, gain, tm, tn, nb_seq):
    m, k = x.shape
    n = w.shape[1]
    g = jnp.tile(gain.reshape(1, MOBA_DH), (1, n // MOBA_DH))
    nb = tm // MOBA_BLOCK
    return pl.pallas_call(
        functools.partial(_proj_knorm_kernel, nb_seq=nb_seq),
        grid=(m // tm, n // tn),
        in_specs=[pl.BlockSpec((tm, k), lambda i, j: (i, 0)), pl.BlockSpec((k, tn), lambda i, j: (0, j)),
                  pl.BlockSpec((1, tn), lambda i, j: (0, j))],
        out_specs=[pl.BlockSpec((tm, 2 * tn), lambda i, j: (i, j)),
                   pl.BlockSpec((nb, 1, tn), lambda i, j: (i, 0, j))],
        out_shape=[jax.ShapeDtypeStruct((m, 2 * n), BF16),
                   jax.ShapeDtypeStruct((m // MOBA_BLOCK, 1, n), F32)],
        compiler_params=_params(("parallel", "arbitrary")),
        name="proj_moba_k",
    )(x, w, g)


def _gdn_all_kernel(x_ref, ba_ref, cw_ref, alog_ref, dtb_ref, gn_ref, o_ref, s_ref, tail_ref, qkv_s, xp_s, *, tb):
    t = pl.program_id(0)
    C = GDN_CHUNK
    nc = tb // C
    nh = GDN_HEADS
    nbat = x_ref.shape[0]

    @pl.when(t == 0)
    def _():
        s_ref[...] = jnp.zeros_like(s_ref)
        tail_ref[...] = jnp.zeros_like(tail_ref)

    def conv_group(gi, l2_scale):
        off = pl.multiple_of(gi * LANES, LANES)
        w = cw_ref[:, pl.ds(off, LANES)]
        for bi in range(nbat):
            x = x_ref[bi, :, pl.ds(off, LANES)]
            xp_s[bi, 0:8, :] = tail_ref[bi * 8:(bi + 1) * 8, pl.ds(off, LANES)]
            xp_s[bi, 8:8 + tb, :] = x
            y = x * w[GDN_CONV - 1:GDN_CONV]
            for j in range(GDN_CONV - 1):
                lag = GDN_CONV - 1 - j
                y = y + xp_s[bi, 8 - lag:8 - lag + tb, :] * w[j:j + 1]
            tail_ref[bi * 8:(bi + 1) * 8, pl.ds(off, LANES)] = x[tb - 8:tb]
            y = y * _sigmoid(y)
            if l2_scale is not None:
                y = y * (lax.rsqrt(jnp.sum(y * y, axis=-1, keepdims=True) + 1e-6) * l2_scale)
            qkv_s[bi * tb:(bi + 1) * tb, pl.ds(off, LANES)] = y

    for g0, l2_scale in ((0, GDN_DK ** -0.5), (nh, 1.0), (2 * nh, None)):
        lax.fori_loop(g0, g0 + nh, lambda gi, carry, sc=l2_scale: conv_group(gi, sc), None)

    row_in_chunk = lax.broadcasted_iota(jnp.int32, (tb, LANES), 0) & (C - 1)
    beta_all, gc, egc, kdsc, egl, gc_t = [], [], [], [], [], []
    for bi in range(nbat):
        ba = ba_ref[bi]
        beta_all.append(_sigmoid(ba))
        sp_in = ba + dtb_ref[...]
        softplus = jnp.maximum(sp_in, 0.0) + jnp.log1p(jnp.exp(-jnp.abs(sp_in)))
        g = -(jnp.exp(alog_ref[...]) * softplus)
        sh = 1
        while sh < C:
            g = g + jnp.where(row_in_chunk >= sh, pltpu.roll(g, sh, axis=0), 0.0)
            sh *= 2
        gl_rows = [g[(c + 1) * C - 1:(c + 1) * C] for c in range(nc)]
        gl_b = jnp.concatenate([jnp.broadcast_to(r, (C, LANES)) for r in gl_rows], axis=0)
        gc.append(g)
        egc.append(jnp.exp(g))
        kdsc.append(jnp.exp(gl_b - g))
        egl.append([jnp.exp(r) for r in gl_rows])
        gc_t.append(g.T)

    ri = lax.broadcasted_iota(jnp.int32, (C, C), 0)
    ci = lax.broadcasted_iota(jnp.int32, (C, C), 1)
    tril = ri >= ci
    strict = ri > ci
    eye_f = (ri == ci).astype(F32)
    lvl_masks = []
    s = 1
    while s < C:
        sh2 = s.bit_length()
        lvl_masks.append((ri >> sh2 == ci >> sh2) & ((ri & (2 * s - 1)) >= s) & ((ci & (2 * s - 1)) < s))
        s *= 2
    gn = gn_ref[...]

    seqs = [(bi, h) for bi in range(nbat) for h in range(nh)]
    ids = range(len(seqs))
    states = [s_ref[i] for i in ids]
    grp = lambda g, h: slice((g * nh + h) * LANES, (g * nh + h + 1) * LANES)
    for c in range(nc):
        rows = slice(c * C, (c + 1) * C)
        srow = lambda bi: slice(bi * tb + c * C, bi * tb + (c + 1) * C)
        gate_col = lambda arr, bi, h: arr[bi][rows, nh + h:nh + h + 1]
        qc = [qkv_s[srow(bi), grp(0, h)] for bi, h in seqs]
        kc = [qkv_s[srow(bi), grp(1, h)] for bi, h in seqs]
        vc = [qkv_s[srow(bi), grp(2, h)] for bi, h in seqs]
        bcol = [beta_all[bi][rows, h:h + 1] for bi, h in seqs]
        ecol = [gate_col(egc, bi, h) for bi, h in seqs]
        decay = [jnp.exp(jnp.where(tril, gate_col(gc, bi, h) - gc_t[bi][nh + h:nh + h + 1, rows], -jnp.inf))
                 for bi, h in seqs]
        k_beta = [kc[i] * bcol[i] for i in ids]
        st = [_dot_t(jnp.concatenate([k_beta[i], qc[i]], axis=0).astype(BF16), kc[i].astype(BF16))
              for i in ids]
        a_mat = [jnp.where(strict, st[i][:C] * decay[i], 0.0) for i in ids]
        qk = [jnp.where(tril, st[i][C:] * decay[i], 0.0) for i in ids]
        t_inv = [eye_f - jnp.where(lvl_masks[0], a_mat[i], 0.0) for i in ids]
        for msk in lvl_masks[1:]:
            x_mid = [_bdot(jnp.where(msk, a_mat[i], 0.0), t_inv[i]) for i in ids]
            y_mid = [_bdot(t_inv[i], x_mid[i]) for i in ids]
            t_inv = [t_inv[i] - y_mid[i] for i in ids]
        uw = [_bdot(t_inv[i], jnp.concatenate([vc[i] * bcol[i], k_beta[i] * ecol[i]], axis=1))
              for i in ids]
        wq = [_bdot(jnp.concatenate([uw[i][:, LANES:], qc[i] * ecol[i]], axis=0), states[i])
              for i in ids]
        v_new = [uw[i][:, :LANES] - wq[i][:C] for i in ids]
        o = [wq[i][C:] + _bdot(qk[i], v_new[i]) for i in ids]
        k_dec = [kc[i] * gate_col(kdsc, bi, h) for i, (bi, h) in enumerate(seqs)]
        states = [states[i] * egl[bi][c][:, nh + h:nh + h + 1] + lax.dot_general(
            k_dec[i].astype(BF16), v_new[i].astype(BF16), (((0,), (0,)), ((), ())),
            preferred_element_type=F32) for i, (bi, h) in enumerate(seqs)]
        for i, (bi, h) in enumerate(seqs):
            on = o[i] * lax.rsqrt(jnp.mean(o[i] * o[i], axis=-1, keepdims=True) + RMS_EPS) * gn
            zc = x_ref[bi, rows, grp(3, h)]
            o_ref[bi, rows, grp(0, h)] = (on * (zc * _sigmoid(zc))).astype(o_ref.dtype)
    for i in ids:
        s_ref[i] = states[i]


def _gdn_all(proj, ba, conv_wt, a_log, dt_bias, gnorm, tb):
    b, t, width = proj.shape
    nh = GDN_HEADS
    lane_vec = lambda v: jnp.pad(v.reshape(1, nh), ((0, 0), (nh, LANES - 2 * nh)))
    full = lambda shape: pl.BlockSpec(shape, lambda ti: (0,) * len(shape))
    return pl.pallas_call(
        functools.partial(_gdn_all_kernel, tb=tb),
        grid=(t // tb,),
        in_specs=[pl.BlockSpec((b, tb, width), lambda ti: (0, ti, 0)),
                  pl.BlockSpec((b, tb, LANES), lambda ti: (0, ti, 0)),
                  full((GDN_CONV, 3 * GDN_W)), full((1, LANES)), full((1, LANES)), full((1, GDN_DV))],
        out_specs=pl.BlockSpec((b, tb, GDN_W), lambda ti: (0, ti, 0)),
        out_shape=jax.ShapeDtypeStruct((b, t, GDN_W), BF16),
        scratch_shapes=[pltpu.VMEM((b * nh, GDN_DK, GDN_DV), F32), pltpu.VMEM((b * 8, 3 * GDN_W), F32),
                        pltpu.VMEM((b * tb, 3 * GDN_W), F32), pltpu.VMEM((b, tb + 8, LANES), F32)],
        compiler_params=_params(("arbitrary",)),
        name="gdn",
    )(proj, ba, conv_wt, lane_vec(a_log), lane_vec(dt_bias), gnorm.reshape(1, GDN_DV))


MOBA_QW = 2 * MOBA_BLOCK
MOBA_ROWS_L = 16
MOBA_BOUND_SLACK = 1.02
MOBA_SAFE_LOG2_RANGE = 100.0


def _moba_select_kernel(qt_ref, km_ref, kn_ref, qa_ref, *, nb, tq):
    nbp = -(-nb // 16) * 16
    qt = qt_ref[0]
    km = km_ref[0].astype(BF16)
    if nbp > nb:
        km = jnp.concatenate([km, jnp.zeros((nbp - nb, LANES), BF16)], axis=0)
    gate = jnp.dot(km, qt, preferred_element_type=F32)
    row = lax.broadcasted_iota(jnp.int32, (nbp, tq), 0)
    qblk = (lax.broadcasted_iota(jnp.int32, (nbp, tq), 1) + pl.program_id(2) * tq) // MOBA_BLOCK
    past = row < qblk
    g = jnp.where(past, gate, -jnp.inf)
    sel = row < 0
    for _ in range(MOBA_TOPK):
        m = jnp.max(g, axis=0, keepdims=True)
        idx = jnp.min(jnp.where(g == m, row, nbp), axis=0, keepdims=True)
        hit = row == idx
        sel = sel | (hit & past)
        g = jnp.where(hit, -jnp.inf, g)
    q32 = qt.astype(F32)
    bound = jnp.sqrt(jnp.sum(q32 * q32, axis=0, keepdims=True)) * kn_ref[:, 0:1] * MOBA_BOUND_SLACK
    qa_ref[0, 0:LANES, :] = qt
    qa_ref[0, LANES:LANES + nbp, :] = jnp.where(sel, 0.0, MASK_NEG).astype(BF16)
    tail_row = lax.broadcasted_iota(jnp.int32, (LANES - nbp, tq), 0)
    qa_ref[0, LANES + nbp:, :] = jnp.where(tail_row == LANES - nbp - 1, -bound, 0.0).astype(BF16)


def _moba_select(qt, kmean, k_norm_bound, tq):
    b, _, t = qt.shape
    nb = t // MOBA_BLOCK
    assert -(-nb // 16) * 16 < LANES
    return pl.pallas_call(
        functools.partial(_moba_select_kernel, nb=nb, tq=tq),
        grid=(b, MOBA_HEADS, t // tq),
        in_specs=[pl.BlockSpec((1, LANES, tq), lambda bi, hi, qi: (bi, hi, qi)),
                  pl.BlockSpec((1, nb, LANES), lambda bi, hi, qi: (bi, 0, hi)),
                  pl.BlockSpec((1, LANES), lambda bi, hi, qi: (0, 0))],
        out_specs=pl.BlockSpec((1, 2 * LANES, tq), lambda bi, hi, qi: (bi, hi, qi)),
        out_shape=jax.ShapeDtypeStruct((b, 2 * MOBA_W, t), BF16),
        compiler_params=_params(("parallel", "parallel", "parallel")),
        name="moba_select",
    )(qt, kmean, k_norm_bound)


def _moba_kernel(qa_ref, ka_ref, vt_ref, o_ref, acc_ref, m_ref, sa_ref, sb_ref, xa_ref, xb_ref, *, nb, group):
    i2 = pl.program_id(2)
    BLK = MOBA_BLOCK

    def pv(p, start, width):
        lhs = jnp.concatenate([vt_ref[0, :, pl.ds(start, width)], jnp.ones((MOBA_ROWS_L, width), BF16)], axis=0)
        return jnp.dot(lhs, p, preferred_element_type=F32)

    key_i = lax.broadcasted_iota(jnp.int32, (BLK, BLK), 0)
    qry_i = lax.broadcasted_iota(jnp.int32, (BLK, BLK), 1)
    starts = [pl.multiple_of((2 * i2 + hf) * BLK, BLK) for hf in range(2)]
    s_own = [jnp.dot(ka_ref[0, pl.ds(starts[hf], BLK), 0:LANES], qa_ref[0, 0:LANES, hf * BLK:(hf + 1) * BLK],
                     preferred_element_type=F32) for hf in range(2)]
    s_own = [s_own[hf] + qa_ref[0, 2 * LANES - 1:2 * LANES, hf * BLK:(hf + 1) * BLK].astype(F32) for hf in range(2)]
    s_own = [jnp.where(key_i <= qry_i, s, MASK_NEG) for s in s_own]
    m_own = [jnp.max(s, axis=0, keepdims=True) for s in s_own]
    p_own = [jnp.exp2(s_own[hf] - m_own[hf]).astype(BF16) for hf in range(2)]
    for hf in range(2):
        m_ref[:, hf * BLK:(hf + 1) * BLK] = m_own[hf]
        acc_ref[:, hf * BLK:(hf + 1) * BLK] = pv(p_own[hf], starts[hf], BLK)

    span = group * BLK
    n_pairs = (2 * i2 + 2 * group) // (2 * group)
    last = nb // group - 1
    halves = [slice(hf * BLK, (hf + 1) * BLK) for hf in range(2)]

    def qk(g_idx, dst_ref, mx_dst, hs):
        start = pl.multiple_of(g_idx * span, span)
        s = jnp.dot(ka_ref[0, pl.ds(start, span), :], qa_ref[0, :, hs], preferred_element_type=F32)
        dst_ref[:, hs] = s
        mx_dst[:, hs] = jnp.max(s, axis=0, keepdims=True)

    def softmax_pv(src_ref, mx_src, g_idx, hs):
        m_old = m_ref[:, hs]
        m_new = jnp.maximum(m_old, mx_src[:, hs])
        alpha = jnp.exp2(m_old - m_new)
        m_ref[:, hs] = m_new
        p = jnp.exp2(src_ref[:, hs] - m_new).astype(BF16)
        acc_ref[:, hs] = alpha * acc_ref[:, hs] + pv(p, pl.multiple_of(g_idx * span, span), span)

    for hs in halves:
        qk(0, sa_ref, xa_ref, hs)

    def body(jj, carry):
        for hs in halves:
            qk(2 * jj + 1, sb_ref, xb_ref, hs)
            softmax_pv(sa_ref, xa_ref, 2 * jj, hs)
        for hs in halves:
            qk(jnp.minimum(2 * jj + 2, last), sa_ref, xa_ref, hs)
            softmax_pv(sb_ref, xb_ref, 2 * jj + 1, hs)
        return carry

    lax.fori_loop(0, n_pairs, body, 0)
    acc = acc_ref[...]
    o = acc[0:LANES] * (1.0 / acc[LANES:LANES + 1])
    o_ref[0] = o.T.astype(o_ref.dtype)


def _moba_bounded_kernel(qa_ref, ka_ref, vt_ref, o_ref, acc_ref, pa_ref, pb_ref, *, nb, group):
    i2 = pl.program_id(2)
    BLK = MOBA_BLOCK

    def pv(p, start, width):
        lhs = jnp.concatenate([vt_ref[0, :, pl.ds(start, width)], jnp.ones((MOBA_ROWS_L, width), BF16)], axis=0)
        return jnp.dot(lhs, p, preferred_element_type=F32)

    halves = [slice(hf * BLK, (hf + 1) * BLK) for hf in range(2)]
    key_i = lax.broadcasted_iota(jnp.int32, (BLK, BLK), 0)
    qry_i = lax.broadcasted_iota(jnp.int32, (BLK, BLK), 1)
    own = []
    for hf, hs in enumerate(halves):
        start = pl.multiple_of((2 * i2 + hf) * BLK, BLK)
        s = jnp.dot(ka_ref[0, pl.ds(start, BLK), 0:LANES], qa_ref[0, 0:LANES, hs], preferred_element_type=F32)
        s = s + qa_ref[0, 2 * LANES - 1:2 * LANES, hs].astype(F32)
        own.append((jnp.exp2(jnp.where(key_i <= qry_i, s, MASK_NEG)).astype(BF16), start))

    span = group * BLK
    n_groups = (2 * i2 + group) // group
    n_pairs = n_groups // 2
    last = nb // group - 1

    def qk_exp(g_idx, p_dst, hs):
        start = pl.multiple_of(g_idx * span, span)
        s = jnp.dot(ka_ref[0, pl.ds(start, span), :], qa_ref[0, :, hs], preferred_element_type=F32)
        p_dst[:, hs] = jnp.exp2(s).astype(BF16)

    def pv_acc(p_src, g_idx, hs):
        acc_ref[:, hs] += pv(p_src[:, hs], pl.multiple_of(g_idx * span, span), span)

    for hs in halves:
        qk_exp(0, pa_ref, hs)
    for (p, start), hs in zip(own, halves):
        acc_ref[:, hs] = pv(p, start, BLK)

    def body(jj, carry):
        for hs in halves:
            qk_exp(2 * jj + 1, pb_ref, hs)
            pv_acc(pa_ref, 2 * jj, hs)
        for hs in halves:
            qk_exp(jnp.minimum(2 * jj + 2, last), pa_ref, hs)
            pv_acc(pb_ref, 2 * jj + 1, hs)
        return carry

    lax.fori_loop(0, n_pairs, body, 0)

    @pl.when(n_groups % 2 == 1)
    def _():
        for hs in halves:
            pv_acc(pa_ref, 2 * n_pairs, hs)

    acc = acc_ref[...]
    o = acc[0:LANES] * (1.0 / acc[LANES:LANES + 1])
    o_ref[0] = o.T.astype(o_ref.dtype)


def _moba(qaug, kaug, vt):
    b, _, t = vt.shape
    nb = t // MOBA_BLOCK
    group = 4 if nb % 8 == 0 else 2
    assert nb % (2 * group) == 0
    span = group * MOBA_BLOCK

    def call(body, scratch, name):
        return pl.pallas_call(
            functools.partial(body, nb=nb, group=group),
            grid=(b, MOBA_HEADS, t // MOBA_QW),
            in_specs=[pl.BlockSpec((1, 2 * LANES, MOBA_QW), lambda bi, hi, qi: (bi, hi, qi)),
                      pl.BlockSpec((1, t, 2 * LANES), lambda bi, hi, qi: (bi, 0, hi)),
                      pl.BlockSpec((1, LANES, t), lambda bi, hi, qi: (bi, hi, 0))],
            out_specs=pl.BlockSpec((1, MOBA_QW, LANES), lambda bi, hi, qi: (bi, qi, hi)),
            out_shape=jax.ShapeDtypeStruct((b, t, MOBA_W), BF16),
            scratch_shapes=[pltpu.VMEM((LANES + MOBA_ROWS_L, MOBA_QW), F32)] + scratch,
            compiler_params=_params(("parallel", "parallel", "arbitrary")),
            name=name,
        )

    online = call(_moba_kernel,
                  [pltpu.VMEM((1, MOBA_QW), F32),
                   pltpu.VMEM((span, MOBA_QW), F32), pltpu.VMEM((span, MOBA_QW), F32),
                   pltpu.VMEM((1, MOBA_QW), F32), pltpu.VMEM((1, MOBA_QW), F32)], "moba")
    bounded = call(_moba_bounded_kernel,
                   [pltpu.VMEM((span, MOBA_QW), BF16), pltpu.VMEM((span, MOBA_QW), BF16)], "moba_bounded")
    max_bound = -jnp.min(qaug[:, 2 * LANES - 1::2 * LANES, :].astype(F32))
    return lax.cond(2.0 * max_bound < MOBA_SAFE_LOG2_RANGE, bounded, online, qaug, kaug, vt)


def _oproj_kernel(og_ref, om_ref, wa_ref, wb_ref, x_ref, nw_ref, h_ref, hn_ref):
    y = jnp.dot(og_ref[...], wa_ref[...], preferred_element_type=F32)
    y = y + jnp.dot(om_ref[...], wb_ref[...], preferred_element_type=F32)
    h = x_ref[...] + y
    h_ref[...] = h
    hn = h * lax.rsqrt(jnp.mean(h * h, axis=-1, keepdims=True) + RMS_EPS)
    hn_ref[...] = (hn * nw_ref[...]).astype(hn_ref.dtype)


def _oproj(og, om, wa, wb, x, norm_w, tm):
    m, d = x.shape
    ka, kb = og.shape[1], om.shape[1]
    row = lambda w: pl.BlockSpec((tm, w), lambda i: (i, 0))
    full = lambda r, c: pl.BlockSpec((r, c), lambda i: (0, 0))
    return pl.pallas_call(
        _oproj_kernel,
        grid=(m // tm,),
        in_specs=[row(ka), row(kb), full(ka, d), full(kb, d), row(d), full(1, d)],
        out_specs=[row(d), row(d)],
        out_shape=[jax.ShapeDtypeStruct((m, d), F32), jax.ShapeDtypeStruct((m, d), BF16)],
        compiler_params=_params(("parallel",)),
        name="oproj",
    )(og, om, wa, wb, x, norm_w.reshape(1, d))


def _ffn_kernel(hn_ref, wg_ref, wu_ref, wd_ref, h_ref, nw_ref, h2_ref, hn2_ref, acc_ref):
    f = pl.program_id(1)

    @pl.when(f == 0)
    def _():
        acc_ref[...] = jnp.zeros_like(acc_ref)

    hn = hn_ref[...]
    g = jnp.dot(hn, wg_ref[...], preferred_element_type=F32)
    u = jnp.dot(hn, wu_ref[...], preferred_element_type=F32)
    a = (g * _sigmoid(g) * u).astype(BF16)
    acc_ref[...] += jnp.dot(a, wd_ref[...], preferred_element_type=F32)

    @pl.when(f == pl.num_programs(1) - 1)
    def _():
        h2 = h_ref[...] + acc_ref[...]
        h2_ref[...] = h2
        n = h2 * lax.rsqrt(jnp.mean(h2 * h2, axis=-1, keepdims=True) + RMS_EPS)
        hn2_ref[...] = (n * nw_ref[...]).astype(hn2_ref.dtype)


def _ffn(hn, wg, wu, wd, h, norm_w, tm, tf):
    m, d = h.shape
    dff = wg.shape[1]
    row = pl.BlockSpec((tm, d), lambda i, f: (i, 0))
    return pl.pallas_call(
        _ffn_kernel,
        grid=(m // tm, dff // tf),
        in_specs=[row, pl.BlockSpec((d, tf), lambda i, f: (0, f)), pl.BlockSpec((d, tf), lambda i, f: (0, f)),
                  pl.BlockSpec((tf, d), lambda i, f: (f, 0)), row, pl.BlockSpec((1, d), lambda i, f: (0, 0))],
        out_specs=[row, row],
        out_shape=[jax.ShapeDtypeStruct((m, d), F32), jax.ShapeDtypeStruct((m, d), BF16)],
        scratch_shapes=[pltpu.VMEM((tm, d), F32)],
        compiler_params=_params(("parallel", "arbitrary")),
        name="ffn",
    )(hn, wg, wu, wd, h, norm_w.reshape(1, d))


def _ple_kernel(hn_ref, wg_ref, p_ref, wp_ref, h_ref, o_ref):
    gate = _sigmoid(jnp.dot(hn_ref[...], wg_ref[...], preferred_element_type=F32))
    proj = jnp.dot(p_ref[...].astype(BF16), wp_ref[...], preferred_element_type=F32)
    o_ref[...] = h_ref[...] + gate * proj


def _ple(hn, wg, p, wp, h, tm, tn):
    m, d = h.shape
    kp = p.shape[1]
    return pl.pallas_call(
        _ple_kernel,
        grid=(m // tm, d // tn),
        in_specs=[pl.BlockSpec((tm, d), lambda i, j: (i, 0)), pl.BlockSpec((d, tn), lambda i, j: (0, j)),
                  pl.BlockSpec((tm, kp), lambda i, j: (i, 0)), pl.BlockSpec((kp, tn), lambda i, j: (0, j)),
                  pl.BlockSpec((tm, tn), lambda i, j: (i, j))],
        out_specs=pl.BlockSpec((tm, tn), lambda i, j: (i, j)),
        out_shape=jax.ShapeDtypeStruct((m, d), F32),
        compiler_params=_params(("parallel", "arbitrary")),
        name="ple",
    )(hn, wg, p, wp, h)


def _layer(h, p, attn_norm, w_in, conv_w, a_log, dt_bias, gdn_norm, q_norm, k_norm, w_o, ffn_norm,
           w_gate, w_up, w_down, ple_norm, w_ple_gate, w_ple_proj):
    b, t, d = h.shape
    m = b * t
    x2 = h.reshape(m, d)
    tm = min(1024, m)

    o_ba = 4 * GDN_W
    o_mq = o_ba + 2 * GDN_HEADS
    w_in16 = w_in.astype(BF16)
    w_gdn = w_in16[:, :o_ba]
    w_ba = jnp.pad(w_in16[:, o_ba:o_mq], ((0, 0), (0, LANES - 2 * GDN_HEADS)))
    w_mq = w_in16[:, o_mq:o_mq + MOBA_W]
    w_mk = w_in16[:, o_mq + MOBA_W:o_mq + 2 * MOBA_W]
    w_mv = w_in16[:, o_mq + 2 * MOBA_W:o_mq + 3 * MOBA_W]

    xn = _rmsnorm(x2, attn_norm, min(512, m))
    gproj = _proj(xn, w_gdn, F32, tm, 1024, "proj_gdn")
    ba = _proj(xn, w_ba, F32, tm, LANES, "proj_ba")
    mq = _proj_moba_q(xn, w_mq, q_norm, tm, 1024)
    kaug, kmean = _proj_moba_k(xn, w_mk, k_norm, tm, 1024, t // MOBA_BLOCK)
    mv = _proj(xn, w_mv, BF16, tm, 1024, "proj_moba_v")

    o_gdn = _gdn_all(gproj.reshape(b, t, 4 * GDN_W), ba.reshape(b, t, LANES), conv_w.T, a_log, dt_bias,
                     gdn_norm, min(256, t))
    k_norm_bound = jnp.full((1, LANES), MOBA_DH ** 0.5, F32) * jnp.max(jnp.abs(k_norm))
    qaug = _moba_select(mq.reshape(b, t, MOBA_W).swapaxes(1, 2), kmean.reshape(b, t // MOBA_BLOCK, MOBA_W),
                        k_norm_bound, min(2048, t))
    o_moba = _moba(qaug, kaug.reshape(b, t, 2 * MOBA_W), mv.reshape(b, t, MOBA_W).swapaxes(1, 2))

    w_o16 = w_o.astype(BF16)
    h1, hn = _oproj(o_gdn.reshape(m, GDN_W), o_moba.reshape(m, MOBA_W), w_o16[:GDN_W], w_o16[GDN_W:],
                    x2, ffn_norm, min(512, m))
    h2, hn2 = _ffn(hn, w_gate.astype(BF16), w_up.astype(BF16), w_down.astype(BF16), h1, ple_norm,
                   min(512, m), 512)
    h3 = _ple(hn2, w_ple_gate.astype(BF16), p.reshape(m, PLE_DIM), w_ple_proj.astype(BF16), h2, tm, 1024)
    return h3.reshape(b, t, d)


def kernel(x, p, attn_norm, w_in, conv_w, A_log, dt_bias, gdn_norm, q_norm, k_norm, w_o, ffn_norm,
           w_gate, w_up, w_down, ple_norm, w_ple_gate, w_ple_proj):
    h = x
    for i in range(p.shape[0]):
        h = _layer(h, p[i], attn_norm[i], w_in[i], conv_w[i], A_log[i], dt_bias[i], gdn_norm[i],
                   q_norm[i], k_norm[i], w_o[i], ffn_norm[i], w_gate[i], w_up[i], w_down[i],
                   ple_norm[i], w_ple_gate[i], w_ple_proj[i])
    return h
```

```python
import functools

import jax
import jax.numpy as jnp
from jax import lax
from jax.experimental import pallas as pl
from jax.experimental.pallas import tpu as pltpu

D_MODEL = 2048
PLE_DIM = 256
GDN_HEADS = 8
GDN_DK = 128
GDN_DV = 128
GDN_CONV = 4
GDN_CHUNK = 64
MOBA_HEADS = 8
MOBA_DH = 128
MOBA_BLOCK = 256
MOBA_TOPK = 3
RMS_EPS = 1e-6
GDN_W = GDN_HEADS * GDN_DK
MOBA_W = MOBA_HEADS * MOBA_DH
LANES = 128
LOG2E = 1.4426950408889634
MASK_NEG = -1e30

F32 = jnp.float32
BF16 = jnp.bfloat16

VMEM_LIMIT = 56 * 1024 * 1024


def _params(sem):
    return pltpu.CompilerParams(dimension_semantics=sem, vmem_limit_bytes=VMEM_LIMIT)


def _sigmoid(x):
    return 1.0 / (1.0 + jnp.exp(-x))


def _dot_t(a, b):
    return lax.dot_general(a, b, (((1,), (1,)), ((), ())), preferred_element_type=F32)


def _bdot(a, b):
    return jnp.dot(a.astype(BF16), b.astype(BF16), preferred_element_type=F32)


def _rmsnorm_kernel(x_ref, w_ref, o_ref):
    x = x_ref[...]
    y = x * lax.rsqrt(jnp.mean(x * x, axis=-1, keepdims=True) + RMS_EPS)
    o_ref[...] = (y * w_ref[...]).astype(o_ref.dtype)


def _rmsnorm(x, w, tm):
    m, d = x.shape
    return pl.pallas_call(
        _rmsnorm_kernel,
        grid=(m // tm,),
        in_specs=[pl.BlockSpec((tm, d), lambda i: (i, 0)), pl.BlockSpec((1, d), lambda i: (0, 0))],
        out_specs=pl.BlockSpec((tm, d), lambda i: (i, 0)),
        out_shape=jax.ShapeDtypeStruct((m, d), BF16),
        compiler_params=_params(("parallel",)),
        name="rmsnorm_cast",
    )(x, w.reshape(1, d))


def _proj_kernel(x_ref, w_ref, o_ref):
    o_ref[...] = jnp.dot(x_ref[...], w_ref[...], preferred_element_type=F32).astype(o_ref.dtype)


def _proj(x, w, out_dtype, tm, tn, name):
    m, k = x.shape
    n = w.shape[1]
    return pl.pallas_call(
        _proj_kernel,
        grid=(m // tm, n // tn),
        in_specs=[pl.BlockSpec((tm, k), lambda i, j: (i, 0)), pl.BlockSpec((k, tn), lambda i, j: (0, j))],
        out_specs=pl.BlockSpec((tm, tn), lambda i, j: (i, j)),
        out_shape=jax.ShapeDtypeStruct((m, n), out_dtype),
        compiler_params=_params(("parallel", "arbitrary")),
        name=name,
    )(x, w)


def _head_rmsnorm(y, gain, scale):
    outs = []
    for h in range(y.shape[1] // LANES):
        yh = y[:, h * LANES:(h + 1) * LANES]
        r = lax.rsqrt(jnp.mean(yh * yh, axis=-1, keepdims=True) + RMS_EPS)
        outs.append(yh * r * gain[:, h * LANES:(h + 1) * LANES] * scale)
    return jnp.concatenate(outs, axis=1)


def _proj_qnorm_kernel(x_ref, w_ref, g_ref, o_ref, *, scale):
    y = jnp.dot(x_ref[...], w_ref[...], preferred_element_type=F32)
    o_ref[...] = _head_rmsnorm(y, g_ref[...], scale).astype(o_ref.dtype)


def _proj_knorm_kernel(x_ref, w_ref, g_ref, o_ref, km_ref, *, nb_seq):
    y = jnp.dot(x_ref[...], w_ref[...], preferred_element_type=F32)
    yn = _head_rmsnorm(y, g_ref[...], 1.0)
    tm = yn.shape[0]
    row = lax.broadcasted_iota(jnp.int32, (tm, LANES), 0) + pl.program_id(0) * tm
    lane = lax.broadcasted_iota(jnp.int32, (tm, LANES), 1)
    onehot = jnp.where((lane == lax.rem(row // MOBA_BLOCK, nb_seq)) | (lane == LANES - 1), 1.0, 0.0)
    onehot = onehot.astype(o_ref.dtype)
    yb = yn.astype(o_ref.dtype)
    parts = []
    for h in range(yn.shape[1] // LANES):
        parts += [yb[:, h * LANES:(h + 1) * LANES], onehot]
    o_ref[...] = jnp.concatenate(parts, axis=1)
    for r in range(tm // MOBA_BLOCK):
        blk = yn[r * MOBA_BLOCK:(r + 1) * MOBA_BLOCK]
        km_ref[r] = jnp.mean(blk, axis=0, keepdims=True)


def _proj_moba_q(x, w, gain, tm, tn):
    m, k = x.shape
    n = w.shape[1]
    g = jnp.tile(gain.reshape(1, MOBA_DH), (1, n // MOBA_DH))
    return pl.pallas_call(
        functools.partial(_proj_qnorm_kernel, scale=MOBA_DH ** -0.5 * LOG2E),
        grid=(m // tm, n // tn),
        in_specs=[pl.BlockSpec((tm, k), lambda i, j: (i, 0)), pl.BlockSpec((k, tn), lambda i, j: (0, j)),
                  pl.BlockSpec((1, tn), lambda i, j: (0, j))],
        out_specs=pl.BlockSpec((tm, tn), lambda i, j: (i, j)),
        out_shape=jax.ShapeDtypeStruct((m, n), BF16),
        compiler_params=_params(("parallel", "arbitrary")),
        name="proj_moba_q",
    )(x, w, g)


def _proj_moba_k(x, w, gain, tm, tn, nb_seq):
    m, k = x.shape
    n = w.shape[1]
    g = jnp.tile(gain.reshape(1, MOBA_DH), (1, n // MOBA_DH))
    nb = tm // MOBA_BLOCK
    return pl.pallas_call(
        functools.partial(_proj_knorm_kernel, nb_seq=nb_seq),
        grid=(m // tm, n // tn),
        in_specs=[pl.BlockSpec((tm, k), lambda i, j: (i, 0)), pl.BlockSpec((k, tn), lambda i, j: (0, j)),
                  pl.BlockSpec((1, tn), lambda i, j: (0, j))],
        out_specs=[pl.BlockSpec((tm, 2 * tn), lambda i, j: (i, j)),
                   pl.BlockSpec((nb, 1, tn), lambda i, j: (i, 0, j))],
        out_shape=[jax.ShapeDtypeStruct((m, 2 * n), BF16),
                   jax.ShapeDtypeStruct((m // MOBA_BLOCK, 1, n), F32)],
        compiler_params=_params(("parallel", "arbitrary")),
        name="proj_moba_k",
    )(x, w, g)


def _gdn_all_kernel(x_ref, ba_ref, cw_ref, alog_ref, dtb_ref, gn_ref, o_ref, s_ref, tail_ref, qkv_s, xp_s, *, tb):
    t = pl.program_id(0)
    C = GDN_CHUNK
    nc = tb // C
    nh = GDN_HEADS
    nbat = x_ref.shape[0]

    @pl.when(t == 0)
    def _():
        s_ref[...] = jnp.zeros_like(s_ref)
        tail_ref[...] = jnp.zeros_like(tail_ref)

    def conv_group(gi, l2_scale):
        off = pl.multiple_of(gi * LANES, LANES)
        w = cw_ref[:, pl.ds(off, LANES)]
        for bi in range(nbat):
            x = x_ref[bi, :, pl.ds(off, LANES)]
            xp_s[bi, 0:8, :] = tail_ref[bi * 8:(bi + 1) * 8, pl.ds(off, LANES)]
            xp_s[bi, 8:8 + tb, :] = x
            y = x * w[GDN_CONV - 1:GDN_CONV]
            for j in range(GDN_CONV - 1):
                lag = GDN_CONV - 1 - j
                y = y + xp_s[bi, 8 - lag:8 - lag + tb, :] * w[j:j + 1]
            tail_ref[bi * 8:(bi + 1) * 8, pl.ds(off, LANES)] = x[tb - 8:tb]
            y = y * _sigmoid(y)
            if l2_scale is not None:
                y = y * (lax.rsqrt(jnp.sum(y * y, axis=-1, keepdims=True) + 1e-6) * l2_scale)
            qkv_s[bi * tb:(bi + 1) * tb, pl.ds(off, LANES)] = y

    for g0, l2_scale in ((0, GDN_DK ** -0.5), (nh, 1.0), (2 * nh, None)):
        lax.fori_loop(g0, g0 + nh, lambda gi, carry, sc=l2_scale: conv_group(gi, sc), None)

    row_in_chunk = lax.broadcasted_iota(jnp.int32, (tb, LANES), 0) & (C - 1)
    beta_all, gc, egc, kdsc, egl, gc_t = [], [], [], [], [], []
    for bi in range(nbat):
        ba = ba_ref[bi]
        beta_all.append(_sigmoid(ba))
        sp_in = ba + dtb_ref[...]
        softplus = jnp.maximum(sp_in, 0.0) + jnp.log1p(jnp.exp(-jnp.abs(sp_in)))
        g = -(jnp.exp(alog_ref[...]) * softplus)
        sh = 1
        while sh < C:
            g = g + jnp.where(row_in_chunk >= sh, pltpu.roll(g, sh, axis=0), 0.0)
            sh *= 2
        gl_rows = [g[(c + 1) * C - 1:(c + 1) * C] for c in range(nc)]
        gl_b = jnp.concatenate([jnp.broadcast_to(r, (C, LANES)) for r in gl_rows], axis=0)
        gc.append(g)
        egc.append(jnp.exp(g))
        kdsc.append(jnp.exp(gl_b - g))
        egl.append([jnp.exp(r) for r in gl_rows])
        gc_t.append(g.T)

    ri = lax.broadcasted_iota(jnp.int32, (C, C), 0)
    ci = lax.broadcasted_iota(jnp.int32, (C, C), 1)
    tril = ri >= ci
    strict = ri > ci
    eye_f = (ri == ci).astype(F32)
    lvl_masks = []
    s = 1
    while s < C:
        sh2 = s.bit_length()
        lvl_masks.append((ri >> sh2 == ci >> sh2) & ((ri & (2 * s - 1)) >= s) & ((ci & (2 * s - 1)) < s))
        s *= 2
    gn = gn_ref[...]

    seqs = [(bi, h) for bi in range(nbat) for h in range(nh)]
    ids = range(len(seqs))
    states = [s_ref[i] for i in ids]
    grp = lambda g, h: slice((g * nh + h) * LANES, (g * nh + h + 1) * LANES)
    for c in range(nc):
        rows = slice(c * C, (c + 1) * C)
        srow = lambda bi: slice(bi * tb + c * C, bi * tb + (c + 1) * C)
        gate_col = lambda arr, bi, h: arr[bi][rows, nh + h:nh + h + 1]
        qc = [qkv_s[srow(bi), grp(0, h)] for bi, h in seqs]
        kc = [qkv_s[srow(bi), grp(1, h)] for bi, h in seqs]
        vc = [qkv_s[srow(bi), grp(2, h)] for bi, h in seqs]
        bcol = [beta_all[bi][rows, h:h + 1] for bi, h in seqs]
        ecol = [gate_col(egc, bi, h) for bi, h in seqs]
        decay = [jnp.exp(jnp.where(tril, gate_col(gc, bi, h) - gc_t[bi][nh + h:nh + h + 1, rows], -jnp.inf))
                 for bi, h in seqs]
        k_beta = [kc[i] * bcol[i] for i in ids]
        st = [_dot_t(jnp.concatenate([k_beta[i], qc[i]], axis=0).astype(BF16), kc[i].astype(BF16))
              for i in ids]
        a_mat = [jnp.where(strict, st[i][:C] * decay[i], 0.0) for i in ids]
        qk = [jnp.where(tril, st[i][C:] * decay[i], 0.0) for i in ids]
        t_inv = [eye_f - jnp.where(lvl_masks[0], a_mat[i], 0.0) for i in ids]
        for msk in lvl_masks[1:]:
            x_mid = [_bdot(jnp.where(msk, a_mat[i], 0.0), t_inv[i]) for i in ids]
            y_mid = [_bdot(t_inv[i], x_mid[i]) for i in ids]
            t_inv = [t_inv[i] - y_mid[i] for i in ids]
        uw = [_bdot(t_inv[i], jnp.concatenate([vc[i] * bcol[i], k_beta[i] * ecol[i]], axis=1))
              for i in ids]
        wq = [_bdot(jnp.concatenate([uw[i][:, LANES:], qc[i] * ecol[i]], axis=0), states[i])
              for i in ids]
        v_new = [uw[i][:, :LANES] - wq[i][:C] for i in ids]
        o = [wq[i][C:] + _bdot(qk[i], v_new[i]) for i in ids]
        k_dec = [kc[i] * gate_col(kdsc, bi, h) for i, (bi, h) in enumerate(seqs)]
        states = [states[i] * egl[bi][c][:, nh + h:nh + h + 1] + lax.dot_general(
            k_dec[i].astype(BF16), v_new[i].astype(BF16), (((0,), (0,)), ((), ())),
            preferred_element_type=F32) for i, (bi, h) in enumerate(seqs)]
        for i, (bi, h) in enumerate(seqs):
            on = o[i] * lax.rsqrt(jnp.mean(o[i] * o[i], axis=-1, keepdims=True) + RMS_EPS) * gn
            zc = x_ref[bi, rows, grp(3, h)]
            o_ref[bi, rows, grp(0, h)] = (on * (zc * _sigmoid(zc))).astype(o_ref.dtype)
    for i in ids:
        s_ref[i] = states[i]


def _gdn_all(proj, ba, conv_wt, a_log, dt_bias, gnorm, tb):
    b, t, width = proj.shape
    nh = GDN_HEADS
    lane_vec = lambda v: jnp.pad(v.reshape(1, nh), ((0, 0), (nh, LANES - 2 * nh)))
    full = lambda shape: pl.BlockSpec(shape, lambda ti: (0,) * len(shape))
    return pl.pallas_call(
        functools.partial(_gdn_all_kernel, tb=tb),
        grid=(t // tb,),
        in_specs=[pl.BlockSpec((b, tb, width), lambda ti: (0, ti, 0)),
                  pl.BlockSpec((b, tb, LANES), lambda ti: (0, ti, 0)),
                  full((GDN_CONV, 3 * GDN_W)), full((1, LANES)), full((1, LANES)), full((1, GDN_DV))],
        out_specs=pl.BlockSpec((b, tb, GDN_W), lambda ti: (0, ti, 0)),
        out_shape=jax.ShapeDtypeStruct((b, t, GDN_W), BF16),
        scratch_shapes=[pltpu.VMEM((b * nh, GDN_DK, GDN_DV), F32), pltpu.VMEM((b * 8, 3 * GDN_W), F32),
                        pltpu.VMEM((b * tb, 3 * GDN_W), F32), pltpu.VMEM((b, tb + 8, LANES), F32)],
        compiler_params=_params(("arbitrary",)),
        name="gdn",
    )(proj, ba, conv_wt, lane_vec(a_log), lane_vec(dt_bias), gnorm.reshape(1, GDN_DV))


MOBA_QW = 2 * MOBA_BLOCK
MOBA_ROWS_L = 16
MOBA_BOUND_SLACK = 1.02
MOBA_SAFE_LOG2_RANGE = 100.0


def _moba_select_kernel(qt_ref, km_ref, kn_ref, qa_ref, *, nb, tq):
    nbp = -(-nb // 16) * 16
    qt = qt_ref[0]
    km = km_ref[0].astype(BF16)
    if nbp > nb:
        km = jnp.concatenate([km, jnp.zeros((nbp - nb, LANES), BF16)], axis=0)
    gate = jnp.dot(km, qt, preferred_element_type=F32)
    row = lax.broadcasted_iota(jnp.int32, (nbp, tq), 0)
    qblk = (lax.broadcasted_iota(jnp.int32, (nbp, tq), 1) + pl.program_id(2) * tq) // MOBA_BLOCK
    past = row < qblk
    g = jnp.where(past, gate, -jnp.inf)
    sel = row < 0
    for _ in range(MOBA_TOPK):
        m = jnp.max(g, axis=0, keepdims=True)
        idx = jnp.min(jnp.where(g == m, row, nbp), axis=0, keepdims=True)
        hit = row == idx
        sel = sel | (hit & past)
        g = jnp.where(hit, -jnp.inf, g)
    q32 = qt.astype(F32)
    bound = jnp.sqrt(jnp.sum(q32 * q32, axis=0, keepdims=True)) * kn_ref[:, 0:1] * MOBA_BOUND_SLACK
    qa_ref[0, 0:LANES, :] = qt
    qa_ref[0, LANES:LANES + nbp, :] = jnp.where(sel, 0.0, MASK_NEG).astype(BF16)
    tail_row = lax.broadcasted_iota(jnp.int32, (LANES - nbp, tq), 0)
    qa_ref[0, LANES + nbp:, :] = jnp.where(tail_row == LANES - nbp - 1, -bound, 0.0).astype(BF16)


def _moba_select(qt, kmean, k_norm_bound, tq):
    b, _, t = qt.shape
    nb = t // MOBA_BLOCK
    assert -(-nb // 16) * 16 < LANES
    return pl.pallas_call(
        functools.partial(_moba_select_kernel, nb=nb, tq=tq),
        grid=(b, MOBA_HEADS, t // tq),
        in_specs=[pl.BlockSpec((1, LANES, tq), lambda bi, hi, qi: (bi, hi, qi)),
                  pl.BlockSpec((1, nb, LANES), lambda bi, hi, qi: (bi, 0, hi)),
                  pl.BlockSpec((1, LANES), lambda bi, hi, qi: (0, 0))],
        out_specs=pl.BlockSpec((1, 2 * LANES, tq), lambda bi, hi, qi: (bi, hi, qi)),
        out_shape=jax.ShapeDtypeStruct((b, 2 * MOBA_W, t), BF16),
        compiler_params=_params(("parallel", "parallel", "parallel")),
        name="moba_select",
    )(qt, kmean, k_norm_bound)


def _moba_kernel(qa_ref, ka_ref, vt_ref, o_ref, acc_ref, m_ref, sa_ref, sb_ref, xa_ref, xb_ref, *, nb, group):
    i2 = pl.program_id(2)
    BLK = MOBA_BLOCK

    def pv(p, start, width):
        lhs = jnp.concatenate([vt_ref[0, :, pl.ds(start, width)], jnp.ones((MOBA_ROWS_L, width), BF16)], axis=0)
        return jnp.dot(lhs, p, preferred_element_type=F32)

    key_i = lax.broadcasted_iota(jnp.int32, (BLK, BLK), 0)
    qry_i = lax.broadcasted_iota(jnp.int32, (BLK, BLK), 1)
    starts = [pl.multiple_of((2 * i2 + hf) * BLK, BLK) for hf in range(2)]
    s_own = [jnp.dot(ka_ref[0, pl.ds(starts[hf], BLK), 0:LANES], qa_ref[0, 0:LANES, hf * BLK:(hf + 1) * BLK],
                     preferred_element_type=F32) for hf in range(2)]
    s_own = [s_own[hf] + qa_ref[0, 2 * LANES - 1:2 * LANES, hf * BLK:(hf + 1) * BLK].astype(F32) for hf in range(2)]
    s_own = [jnp.where(key_i <= qry_i, s, MASK_NEG) for s in s_own]
    m_own = [jnp.max(s, axis=0, keepdims=True) for s in s_own]
    p_own = [jnp.exp2(s_own[hf] - m_own[hf]).astype(BF16) for hf in range(2)]
    for hf in range(2):
        m_ref[:, hf * BLK:(hf + 1) * BLK] = m_own[hf]
        acc_ref[:, hf * BLK:(hf + 1) * BLK] = pv(p_own[hf], starts[hf], BLK)

    span = group * BLK
    n_pairs = (2 * i2 + 2 * group) // (2 * group)
    last = nb // group - 1
    halves = [slice(hf * BLK, (hf + 1) * BLK) for hf in range(2)]

    def qk(g_idx, dst_ref, mx_dst, hs):
        start = pl.multiple_of(g_idx * span, span)
        s = jnp.dot(ka_ref[0, pl.ds(start, span), :], qa_ref[0, :, hs], preferred_element_type=F32)
        dst_ref[:, hs] = s
        mx_dst[:, hs] = jnp.max(s, axis=0, keepdims=True)

    def softmax_pv(src_ref, mx_src, g_idx, hs):
        m_old = m_ref[:, hs]
        m_new = jnp.maximum(m_old, mx_src[:, hs])
        alpha = jnp.exp2(m_old - m_new)
        m_ref[:, hs] = m_new
        p = jnp.exp2(src_ref[:, hs] - m_new).astype(BF16)
        acc_ref[:, hs] = alpha * acc_ref[:, hs] + pv(p, pl.multiple_of(g_idx * span, span), span)

    for hs in halves:
        qk(0, sa_ref, xa_ref, hs)

    def body(jj, carry):
        for hs in halves:
            qk(2 * jj + 1, sb_ref, xb_ref, hs)
            softmax_pv(sa_ref, xa_ref, 2 * jj, hs)
        for hs in halves:
            qk(jnp.minimum(2 * jj + 2, last), sa_ref, xa_ref, hs)
            softmax_pv(sb_ref, xb_ref, 2 * jj + 1, hs)
        return carry

    lax.fori_loop(0, n_pairs, body, 0)
    acc = acc_ref[...]
    o = acc[0:LANES] * (1.0 / acc[LANES:LANES + 1])
    o_ref[0] = o.T.astype(o_ref.dtype)


def _moba_bounded_kernel(qa_ref, ka_ref, vt_ref, o_ref, acc_ref, pa_ref, pb_ref, *, nb, group):
    i2 = pl.program_id(2)
    BLK = MOBA_BLOCK

    def pv(p, start, width):
        lhs = jnp.concatenate([vt_ref[0, :, pl.ds(start, width)], jnp.ones((MOBA_ROWS_L, width), BF16)], axis=0)
        return jnp.dot(lhs, p, preferred_element_type=F32)

    halves = [slice(hf * BLK, (hf + 1) * BLK) for hf in range(2)]
    key_i = lax.broadcasted_iota(jnp.int32, (BLK, BLK), 0)
    qry_i = lax.broadcasted_iota(jnp.int32, (BLK, BLK), 1)
    own = []
    for hf, hs in enumerate(halves):
        start = pl.multiple_of((2 * i2 + hf) * BLK, BLK)
        s = jnp.dot(ka_ref[0, pl.ds(start, BLK), 0:LANES], qa_ref[0, 0:LANES, hs], preferred_element_type=F32)
        s = s + qa_ref[0, 2 * LANES - 1:2 * LANES, hs].astype(F32)
        own.append((jnp.exp2(jnp.where(key_i <= qry_i, s, MASK_NEG)).astype(BF16), start))

    span = group * BLK
    n_groups = (2 * i2 + group) // group
    n_pairs = n_groups // 2
    last = nb // group - 1

    def qk_exp(g_idx, p_dst, hs):
        start = pl.multiple_of(g_idx * span, span)
        s = jnp.dot(ka_ref[0, pl.ds(start, span), :], qa_ref[0, :, hs], preferred_element_type=F32)
        p_dst[:, hs] = jnp.exp2(s).astype(BF16)

    def pv_acc(p_src, g_idx, hs):
        acc_ref[:, hs] += pv(p_src[:, hs], pl.multiple_of(g_idx * span, span), span)

    for hs in halves:
        qk_exp(0, pa_ref, hs)
    for (p, start), hs in zip(own, halves):
        acc_ref[:, hs] = pv(p, start, BLK)

    def body(jj, carry):
        for hs in halves:
            qk_exp(2 * jj + 1, pb_ref, hs)
            pv_acc(pa_ref, 2 * jj, hs)
        for hs in halves:
            qk_exp(jnp.minimum(2 * jj + 2, last), pa_ref, hs)
            pv_acc(pb_ref, 2 * jj + 1, hs)
        return carry

    lax.fori_loop(0, n_pairs, body, 0)

    @pl.when(n_groups % 2 == 1)
    def _():
        for hs in halves:
            pv_acc(pa_ref, 2 * n_pairs, hs)

    acc = acc_ref[...]
    o = acc[0:LANES] * (1.0 / acc[LANES:LANES + 1])
    o_ref[0] = o.T.astype(o_ref.dtype)


def _moba(qaug, kaug, vt):
    b, _, t = vt.shape
    nb = t // MOBA_BLOCK
    group = 4 if nb % 8 == 0 else 2
    assert nb % (2 * group) == 0
    span = group * MOBA_BLOCK

    def call(body, scratch, name):
        return pl.pallas_call(
            functools.partial(body, nb=nb, group=group),
            grid=(b, MOBA_HEADS, t // MOBA_QW),
            in_specs=[pl.BlockSpec((1, 2 * LANES, MOBA_QW), lambda bi, hi, qi: (bi, hi, qi)),
                      pl.BlockSpec((1, t, 2 * LANES), lambda bi, hi, qi: (bi, 0, hi)),
                      pl.BlockSpec((1, LANES, t), lambda bi, hi, qi: (bi, hi, 0))],
            out_specs=pl.BlockSpec((1, MOBA_QW, LANES), lambda bi, hi, qi: (bi, qi, hi)),
            out_shape=jax.ShapeDtypeStruct((b, t, MOBA_W), BF16),
            scratch_shapes=[pltpu.VMEM((LANES + MOBA_ROWS_L, MOBA_QW), F32)] + scratch,
            compiler_params=_params(("parallel", "parallel", "arbitrary")),
            name=name,
        )

    online = call(_moba_kernel,
                  [pltpu.VMEM((1, MOBA_QW), F32),
                   pltpu.VMEM((span, MOBA_QW), F32), pltpu.VMEM((span, MOBA_QW), F32),
                   pltpu.VMEM((1, MOBA_QW), F32), pltpu.VMEM((1, MOBA_QW), F32)], "moba")
    bounded = call(_moba_bounded_kernel,
                   [pltpu.VMEM((span, MOBA_QW), BF16), pltpu.VMEM((span, MOBA_QW), BF16)], "moba_bounded")
    max_bound = -jnp.min(qaug[:, 2 * LANES - 1::2 * LANES, :].astype(F32))
    return lax.cond(2.0 * max_bound < MOBA_SAFE_LOG2_RANGE, bounded, online, qaug, kaug, vt)


def _oproj_kernel(og_ref, om_ref, wa_ref, wb_ref, x_ref, nw_ref, h_ref, hn_ref):
    y = jnp.dot(og_ref[...], wa_ref[...], preferred_element_type=F32)
    y = y + jnp.dot(om_ref[...], wb_ref[...], preferred_element_type=F32)
    h = x_ref[...] + y
    h_ref[...] = h
    hn = h * lax.rsqrt(jnp.mean(h * h, axis=-1, keepdims=True) + RMS_EPS)
    hn_ref[...] = (hn * nw_ref[...]).astype(hn_ref.dtype)


def _oproj(og, om, wa, wb, x, norm_w, tm):
    m, d = x.shape
    ka, kb = og.shape[1], om.shape[1]
    row = lambda w: pl.BlockSpec((tm, w), lambda i: (i, 0))
    full = lambda r, c: pl.BlockSpec((r, c), lambda i: (0, 0))
    return pl.pallas_call(
        _oproj_kernel,
        grid=(m // tm,),
        in_specs=[row(ka), row(kb), full(ka, d), full(kb, d), row(d), full(1, d)],
        out_specs=[row(d), row(d)],
        out_shape=[jax.ShapeDtypeStruct((m, d), F32), jax.ShapeDtypeStruct((m, d), BF16)],
        compiler_params=_params(("parallel",)),
        name="oproj",
    )(og, om, wa, wb, x, norm_w.reshape(1, d))


def _ffn_kernel(hn_ref, wg_ref, wu_ref, wd_ref, h_ref, nw_ref, h2_ref, hn2_ref, acc_ref):
    f = pl.program_id(1)

    @pl.when(f == 0)
    def _():
        acc_ref[...] = jnp.zeros_like(acc_ref)

    hn = hn_ref[...]
    g = jnp.dot(hn, wg_ref[...], preferred_element_type=F32)
    u = jnp.dot(hn, wu_ref[...], preferred_element_type=F32)
    a = (g * _sigmoid(g) * u).astype(BF16)
    qw = acc_ref.shape[1] // 4
    for c in range(4):
        cols = slice(c * qw, (c + 1) * qw)
        acc_ref[:, cols] += jnp.dot(a, wd_ref[:, cols], preferred_element_type=F32)

    @pl.when(f == pl.num_programs(1) - 1)
    def _():
        h2 = h_ref[...] + acc_ref[...]
        h2_ref[...] = h2
        n = h2 * lax.rsqrt(jnp.mean(h2 * h2, axis=-1, keepdims=True) + RMS_EPS)
        hn2_ref[...] = (n * nw_ref[...]).astype(hn2_ref.dtype)


def _ffn(hn, wg, wu, wd, h, norm_w, tm, tf):
    m, d = h.shape
    dff = wg.shape[1]
    row = pl.BlockSpec((tm, d), lambda i, f: (i, 0))
    return pl.pallas_call(
        _ffn_kernel,
        grid=(m // tm, dff // tf),
        in_specs=[row, pl.BlockSpec((d, tf), lambda i, f: (0, f)), pl.BlockSpec((d, tf), lambda i, f: (0, f)),
                  pl.BlockSpec((tf, d), lambda i, f: (f, 0)), row, pl.BlockSpec((1, d), lambda i, f: (0, 0))],
        out_specs=[row, row],
        out_shape=[jax.ShapeDtypeStruct((m, d), F32), jax.ShapeDtypeStruct((m, d), BF16)],
        scratch_shapes=[pltpu.VMEM((tm, d), F32)],
        compiler_params=_params(("parallel", "arbitrary")),
        name="ffn",
    )(hn, wg, wu, wd, h, norm_w.reshape(1, d))


def _ple_kernel(hn_ref, wg_ref, p_ref, wp_ref, h_ref, o_ref):
    gate = _sigmoid(jnp.dot(hn_ref[...], wg_ref[...], preferred_element_type=F32))
    proj = jnp.dot(p_ref[...].astype(BF16), wp_ref[...], preferred_element_type=F32)
    o_ref[...] = h_ref[...] + gate * proj


def _ple(hn, wg, p, wp, h, tm, tn):
    m, d = h.shape
    kp = p.shape[1]
    return pl.pallas_call(
        _ple_kernel,
        grid=(m // tm, d // tn),
        in_specs=[pl.BlockSpec((tm, d), lambda i, j: (i, 0)), pl.BlockSpec((d, tn), lambda i, j: (0, j)),
                  pl.BlockSpec((tm, kp), lambda i, j: (i, 0)), pl.BlockSpec((kp, tn), lambda i, j: (0, j)),
                  pl.BlockSpec((tm, tn), lambda i, j: (i, j))],
        out_specs=pl.BlockSpec((tm, tn), lambda i, j: (i, j)),
        out_shape=jax.ShapeDtypeStruct((m, d), F32),
        compiler_params=_params(("parallel", "arbitrary")),
        name="ple",
    )(hn, wg, p, wp, h)


def _layer(h, p, attn_norm, w_in, conv_w, a_log, dt_bias, gdn_norm, q_norm, k_norm, w_o, ffn_norm,
           w_gate, w_up, w_down, ple_norm, w_ple_gate, w_ple_proj):
    b, t, d = h.shape
    m = b * t
    x2 = h.reshape(m, d)
    tm = min(1024, m)

    o_ba = 4 * GDN_W
    o_mq = o_ba + 2 * GDN_HEADS
    w_in16 = w_in.astype(BF16)
    w_gdn = w_in16[:, :o_ba]
    w_ba = jnp.pad(w_in16[:, o_ba:o_mq], ((0, 0), (0, LANES - 2 * GDN_HEADS)))
    w_mq = w_in16[:, o_mq:o_mq + MOBA_W]
    w_mk = w_in16[:, o_mq + MOBA_W:o_mq + 2 * MOBA_W]
    w_mv = w_in16[:, o_mq + 2 * MOBA_W:o_mq + 3 * MOBA_W]

    xn = _rmsnorm(x2, attn_norm, min(512, m))
    gproj = _proj(xn, w_gdn, F32, tm, 1024, "proj_gdn")
    ba = _proj(xn, w_ba, F32, tm, LANES, "proj_ba")
    mq = _proj_moba_q(xn, w_mq, q_norm, tm, 1024)
    kaug, kmean = _proj_moba_k(xn, w_mk, k_norm, tm, 1024, t // MOBA_BLOCK)
    mv = _proj(xn, w_mv, BF16, tm, 1024, "proj_moba_v")

    o_gdn = _gdn_all(gproj.reshape(b, t, 4 * GDN_W), ba.reshape(b, t, LANES), conv_w.T, a_log, dt_bias,
                     gdn_norm, min(256, t))
    k_norm_bound = jnp.full((1, LANES), MOBA_DH ** 0.5, F32) * jnp.max(jnp.abs(k_norm))
    qaug = _moba_select(mq.reshape(b, t, MOBA_W).swapaxes(1, 2), kmean.reshape(b, t // MOBA_BLOCK, MOBA_W),
                        k_norm_bound, min(2048, t))
    o_moba = _moba(qaug, kaug.reshape(b, t, 2 * MOBA_W), mv.reshape(b, t, MOBA_W).swapaxes(1, 2))

    w_o16 = w_o.astype(BF16)
    h1, hn = _oproj(o_gdn.reshape(m, GDN_W), o_moba.reshape(m, MOBA_W), w_o16[:GDN_W], w_o16[GDN_W:],
                    x2, ffn_norm, min(512, m))
    h2, hn2 = _ffn(hn, w_gate.astype(BF16), w_up.astype(BF16), w_down.astype(BF16), h1, ple_norm,
                   min(512, m), 512)
    h3 = _ple(hn2, w_ple_gate.astype(BF16), p.reshape(m, PLE_DIM), w_ple_proj.astype(BF16), h2, tm, 1024)
    return h3.reshape(b, t, d)


def kernel(x, p, attn_norm, w_in, conv_w, A_log, dt_bias, gdn_norm, q_norm, k_norm, w_o, ffn_norm,
           w_gate, w_up, w_down, ple_norm, w_ple_gate, w_ple_proj):
    h = x
    for i in range(p.shape[0]):
        h = _layer(h, p[i], attn_norm[i], w_in[i], conv_w[i], A_log[i], dt_bias[i], gdn_norm[i],
                   q_norm[i], k_norm[i], w_o[i], ffn_norm[i], w_gate[i], w_up[i], w_down[i],
                   ple_norm[i], w_ple_gate[i], w_ple_proj[i])
    return h
```

```python
import functools

import jax
import jax.numpy as jnp
from jax import lax
from jax.experimental import pallas as pl
from jax.experimental.pallas import tpu as pltpu

D_MODEL = 2048
PLE_DIM = 256
GDN_HEADS = 8
GDN_DK = 128
GDN_DV = 128
GDN_CONV = 4
GDN_CHUNK = 64
GDN_CHUNKS_WIDE = 2
MOBA_HEADS = 8
MOBA_DH = 128
MOBA_BLOCK = 256
MOBA_TOPK = 3
RMS_EPS = 1e-6
GDN_W = GDN_HEADS * GDN_DK
MOBA_W = MOBA_HEADS * MOBA_DH
LANES = 128
LOG2E = 1.4426950408889634
MASK_NEG = -1e30

F32 = jnp.float32
BF16 = jnp.bfloat16

VMEM_LIMIT = 56 * 1024 * 1024


def _params(sem):
    return pltpu.CompilerParams(dimension_semantics=sem, vmem_limit_bytes=VMEM_LIMIT)


def _sigmoid(x):
    return 1.0 / (1.0 + jnp.exp(-x))


def _dot_t(a, b):
    return lax.dot_general(a, b, (((1,), (1,)), ((), ())), preferred_element_type=F32)


def _bdot(a, b):
    return jnp.dot(a.astype(BF16), b.astype(BF16), preferred_element_type=F32)


def _rmsnorm_kernel(x_ref, w_ref, o_ref):
    x = x_ref[...]
    y = x * lax.rsqrt(jnp.mean(x * x, axis=-1, keepdims=True) + RMS_EPS)
    o_ref[...] = (y * w_ref[...]).astype(o_ref.dtype)


def _rmsnorm(x, w, tm):
    m, d = x.shape
    return pl.pallas_call(
        _rmsnorm_kernel,
        grid=(m // tm,),
        in_specs=[pl.BlockSpec((tm, d), lambda i: (i, 0)), pl.BlockSpec((1, d), lambda i: (0, 0))],
        out_specs=pl.BlockSpec((tm, d), lambda i: (i, 0)),
        out_shape=jax.ShapeDtypeStruct((m, d), BF16),
        compiler_params=_params(("parallel",)),
        name="rmsnorm_cast",
    )(x, w.reshape(1, d))


def _proj_kernel(x_ref, w_ref, o_ref):
    o_ref[...] = jnp.dot(x_ref[...], w_ref[...], preferred_element_type=F32).astype(o_ref.dtype)


def _proj(x, w, out_dtype, tm, tn, name):
    m, k = x.shape
    n = w.shape[1]
    return pl.pallas_call(
        _proj_kernel,
        grid=(m // tm, n // tn),
        in_specs=[pl.BlockSpec((tm, k), lambda i, j: (i, 0)), pl.BlockSpec((k, tn), lambda i, j: (0, j))],
        out_specs=pl.BlockSpec((tm, tn), lambda i, j: (i, j)),
        out_shape=jax.ShapeDtypeStruct((m, n), out_dtype),
        compiler_params=_params(("parallel", "arbitrary")),
        name=name,
    )(x, w)


def _head_rmsnorm(y, gain, scale):
    outs = []
    for h in range(y.shape[1] // LANES):
        yh = y[:, h * LANES:(h + 1) * LANES]
        r = lax.rsqrt(jnp.mean(yh * yh, axis=-1, keepdims=True) + RMS_EPS)
        outs.append(yh * r * gain[:, h * LANES:(h + 1) * LANES] * scale)
    return jnp.concatenate(outs, axis=1)


def _proj_qnorm_kernel(x_ref, w_ref, g_ref, o_ref, *, scale):
    y = jnp.dot(x_ref[...], w_ref[...], preferred_element_type=F32)
    o_ref[...] = _head_rmsnorm(y, g_ref[...], scale).astype(o_ref.dtype)


def _proj_knorm_kernel(x_ref, w_ref, g_ref, o_ref, km_ref, *, nb_seq):
    y = jnp.dot(x_ref[...], w_ref[...], preferred_element_type=F32)
    yn = _head_rmsnorm(y, g_ref[...], 1.0)
    tm = yn.shape[0]
    row = lax.broadcasted_iota(jnp.int32, (tm, LANES), 0) + pl.program_id(0) * tm
    lane = lax.broadcasted_iota(jnp.int32, (tm, LANES), 1)
    onehot = jnp.where((lane == lax.rem(row // MOBA_BLOCK, nb_seq)) | (lane == LANES - 1), 1.0, 0.0)
    onehot = onehot.astype(o_ref.dtype)
    yb = yn.astype(o_ref.dtype)
    parts = []
    for h in range(yn.shape[1] // LANES):
        parts += [yb[:, h * LANES:(h + 1) * LANES], onehot]
    o_ref[...] = jnp.concatenate(parts, axis=1)
    for r in range(tm // MOBA_BLOCK):
        blk = yn[r * MOBA_BLOCK:(r + 1) * MOBA_BLOCK]
        km_ref[r] = jnp.mean(blk, axis=0, keepdims=True)


def _proj_moba_q(x, w, gain, tm, tn):
    m, k = x.shape
    n = w.shape[1]
    g = jnp.tile(gain.reshape(1, MOBA_DH), (1, n // MOBA_DH))
    return pl.pallas_call(
        functools.partial(_proj_qnorm_kernel, scale=MOBA_DH ** -0.5 * LOG2E),
        grid=(m // tm, n // tn),
        in_specs=[pl.BlockSpec((tm, k), lambda i, j: (i, 0)), pl.BlockSpec((k, tn), lambda i, j: (0, j)),
                  pl.BlockSpec((1, tn), lambda i, j: (0, j))],
        out_specs=pl.BlockSpec((tm, tn), lambda i, j: (i, j)),
        out_shape=jax.ShapeDtypeStruct((m, n), BF16),
        compiler_params=_params(("parallel", "arbitrary")),
        name="proj_moba_q",
    )(x, w, g)


def _proj_moba_k(x, w, gain, tm, tn, nb_seq):
    m, k = x.shape
    n = w.shape[1]
    g = jnp.tile(gain.reshape(1, MOBA_DH), (1, n // MOBA_DH))
    nb = tm // MOBA_BLOCK
    return pl.pallas_call(
        functools.partial(_proj_knorm_kernel, nb_seq=nb_seq),
        grid=(m // tm, n // tn),
        in_specs=[pl.BlockSpec((tm, k), lambda i, j: (i, 0)), pl.BlockSpec((k, tn), lambda i, j: (0, j)),
                  pl.BlockSpec((1, tn), lambda i, j: (0, j))],
        out_specs=[pl.BlockSpec((tm, 2 * tn), lambda i, j: (i, j)),
                   pl.BlockSpec((nb, 1, tn), lambda i, j: (i, 0, j))],
        out_shape=[jax.ShapeDtypeStruct((m, 2 * n), BF16),
                   jax.ShapeDtypeStruct((m // MOBA_BLOCK, 1, n), F32)],
        compiler_params=_params(("parallel", "arbitrary")),
        name="proj_moba_k",
    )(x, w, g)


def _gdn_all_kernel(x_ref, ba_ref, cw_ref, alog_ref, dtb_ref, gn_ref, o_ref, s_ref, tail_ref, qkv_s, xp_s, *, tb):
    t = pl.program_id(0)
    C = GDN_CHUNK
    nc = tb // C
    nh = GDN_HEADS
    nbat = x_ref.shape[0]

    @pl.when(t == 0)
    def _():
        s_ref[...] = jnp.zeros_like(s_ref)
        tail_ref[...] = jnp.zeros_like(tail_ref)

    def conv_group(gi, l2_scale):
        off = pl.multiple_of(gi * LANES, LANES)
        w = cw_ref[:, pl.ds(off, LANES)]
        for bi in range(nbat):
            x = x_ref[bi, :, pl.ds(off, LANES)]
            xp_s[bi, 0:8, :] = tail_ref[bi * 8:(bi + 1) * 8, pl.ds(off, LANES)]
            xp_s[bi, 8:8 + tb, :] = x
            y = x * w[GDN_CONV - 1:GDN_CONV]
            for j in range(GDN_CONV - 1):
                lag = GDN_CONV - 1 - j
                y = y + xp_s[bi, 8 - lag:8 - lag + tb, :] * w[j:j + 1]
            tail_ref[bi * 8:(bi + 1) * 8, pl.ds(off, LANES)] = x[tb - 8:tb]
            y = y * _sigmoid(y)
            if l2_scale is not None:
                y = y * (lax.rsqrt(jnp.sum(y * y, axis=-1, keepdims=True) + 1e-6) * l2_scale)
            qkv_s[bi * tb:(bi + 1) * tb, pl.ds(off, LANES)] = y

    for g0, l2_scale in ((0, GDN_DK ** -0.5), (nh, 1.0), (2 * nh, None)):
        lax.fori_loop(g0, g0 + nh, lambda gi, carry, sc=l2_scale: conv_group(gi, sc), None)

    row_in_chunk = lax.broadcasted_iota(jnp.int32, (tb, LANES), 0) & (C - 1)
    beta_all, gc, egc, kdsc, egl, gc_t = [], [], [], [], [], []
    for bi in range(nbat):
        ba = ba_ref[bi]
        beta_all.append(_sigmoid(ba))
        sp_in = ba + dtb_ref[...]
        softplus = jnp.maximum(sp_in, 0.0) + jnp.log1p(jnp.exp(-jnp.abs(sp_in)))
        g = -(jnp.exp(alog_ref[...]) * softplus)
        sh = 1
        while sh < C:
            g = g + jnp.where(row_in_chunk >= sh, pltpu.roll(g, sh, axis=0), 0.0)
            sh *= 2
        gl_rows = [g[(c + 1) * C - 1:(c + 1) * C] for c in range(nc)]
        gl_b = jnp.concatenate([jnp.broadcast_to(r, (C, LANES)) for r in gl_rows], axis=0)
        gc.append(g)
        egc.append(jnp.exp(g))
        kdsc.append(jnp.exp(gl_b - g))
        egl.append([jnp.exp(r) for r in gl_rows])
        gc_t.append(g.T)

    ri = lax.broadcasted_iota(jnp.int32, (C, 2 * C), 0)
    lane = lax.broadcasted_iota(jnp.int32, (C, 2 * C), 1)
    ci = lane & (C - 1)
    left = lane < C
    tril = ri >= ci
    strict = ri > ci
    eye_f = (ri == ci).astype(F32)
    lvl_masks = []
    s = 1
    while s < C:
        sh2 = s.bit_length()
        lvl_masks.append((ri >> sh2 == ci >> sh2) & ((ri & (2 * s - 1)) >= s) & ((ci & (2 * s - 1)) < s))
        s *= 2
    gn = gn_ref[...]

    def block_diag(m_l, m_r):
        z = jnp.zeros_like(m_l)
        return jnp.concatenate([jnp.concatenate([m_l, z], axis=1), jnp.concatenate([z, m_r], axis=1)], axis=0)

    def unpack_diag(m):
        return jnp.concatenate([jnp.where(left, m, 0.0), jnp.where(left, 0.0, m)], axis=0)

    seqs = [(bi, h) for bi in range(nbat) for h in range(nh)]
    ids = range(len(seqs))
    pairs = [(2 * p, 2 * p + 1) for p in range(len(seqs) // 2)]
    pids = range(len(pairs))
    states = [s_ref[i] for i in ids]
    grp = lambda g, h: slice((g * nh + h) * LANES, (g * nh + h + 1) * LANES)
    def operands(c):
        rows = slice(c * C, (c + 1) * C)
        srow = lambda bi: slice(bi * tb + c * C, bi * tb + (c + 1) * C)
        col = lambda arr, bi, h: arr[bi][rows, nh + h:nh + h + 1]
        d = dict(rows=rows,
                 qc=[qkv_s[srow(bi), grp(0, h)] for bi, h in seqs],
                 kc=[qkv_s[srow(bi), grp(1, h)] for bi, h in seqs],
                 vc=[qkv_s[srow(bi), grp(2, h)] for bi, h in seqs],
                 bcol=[beta_all[bi][rows, h:h + 1] for bi, h in seqs],
                 ecol=[col(egc, bi, h) for bi, h in seqs],
                 gcol=[col(gc, bi, h) for bi, h in seqs],
                 kdcol=[col(kdsc, bi, h) for bi, h in seqs],
                 grow=[gc_t[bi][nh + h:nh + h + 1, rows] for bi, h in seqs])
        d["k_beta"] = [d["kc"][i] * d["bcol"][i] for i in ids]
        return d

    def stateless(chunks):
        ops = {c: operands(c) for c in chunks}
        items = [(c, p) for c in chunks for p in pids]
        nit = range(len(items))
        decay, st = [], []
        for c, p in items:
            d, (l, r) = ops[c], pairs[p]
            decay.append(jnp.exp(jnp.where(
                tril, jnp.where(left, d["gcol"][l], d["gcol"][r])
                - jnp.concatenate([d["grow"][l], d["grow"][r]], axis=1), -jnp.inf)))
        for c, p in items:
            d, (l, r) = ops[c], pairs[p]
            st.append(_dot_t(jnp.concatenate([jnp.concatenate([d["k_beta"][l], d["k_beta"][r]], axis=1),
                                              jnp.concatenate([d["qc"][l], d["qc"][r]], axis=1)],
                                             axis=0).astype(BF16),
                             block_diag(d["kc"][l], d["kc"][r]).astype(BF16)))
        a_mat = [jnp.where(strict, st[n][:C] * decay[n], 0.0) for n in nit]
        qk = [jnp.where(tril, st[n][C:] * decay[n], 0.0) for n in nit]
        t_inv = [eye_f - jnp.where(lvl_masks[0], a_mat[n], 0.0) for n in nit]
        for msk in lvl_masks[1:]:
            x_mid = [_bdot(jnp.where(msk, a_mat[n], 0.0), unpack_diag(t_inv[n])) for n in nit]
            y_mid = [_bdot(t_inv[n], unpack_diag(x_mid[n])) for n in nit]
            t_inv = [t_inv[n] - y_mid[n] for n in nit]
        uw_p = []
        for n, (c, p) in enumerate(items):
            d, (l, r) = ops[c], pairs[p]
            rhs = [jnp.concatenate([d["vc"][i] * d["bcol"][i], d["k_beta"][i] * d["ecol"][i]], axis=1)
                   for i in (l, r)]
            uw_p.append(_bdot(t_inv[n], block_diag(rhs[0], rhs[1])))
        out = {}
        for k, c in enumerate(chunks):
            base = k * len(pairs)
            uw = [uw_p[base + i // 2][:, (i % 2) * 2 * LANES:(i % 2 + 1) * 2 * LANES] for i in ids]
            out[c] = (ops[c], uw, qk[base:base + len(pairs)])
        return out

    def recurrent(c, d, uw, qk, states):
        wq = [_bdot(jnp.concatenate([uw[i][:, LANES:], d["qc"][i] * d["ecol"][i]], axis=0), states[i])
              for i in ids]
        v_new = [uw[i][:, :LANES] - wq[i][:C] for i in ids]
        o_p = [_bdot(qk[p], block_diag(v_new[l], v_new[r])) for p, (l, r) in enumerate(pairs)]
        o = [wq[i][C:] + o_p[i // 2][:, (i % 2) * LANES:(i % 2 + 1) * LANES] for i in ids]
        new_states = [states[i] * egl[bi][c][:, nh + h:nh + h + 1] + lax.dot_general(
            (d["kc"][i] * d["kdcol"][i]).astype(BF16), v_new[i].astype(BF16), (((0,), (0,)), ((), ())),
            preferred_element_type=F32) for i, (bi, h) in enumerate(seqs)]
        for i, (bi, h) in enumerate(seqs):
            on = o[i] * lax.rsqrt(jnp.mean(o[i] * o[i], axis=-1, keepdims=True) + RMS_EPS) * gn
            zc = x_ref[bi, d["rows"], grp(3, h)]
            o_ref[bi, d["rows"], grp(0, h)] = (on * (zc * _sigmoid(zc))).astype(o_ref.dtype)
        return new_states

    for c0 in range(0, nc, GDN_CHUNKS_WIDE):
        chunks = list(range(c0, min(c0 + GDN_CHUNKS_WIDE, nc)))
        ready = stateless(chunks)
        for c in chunks:
            states = recurrent(c, *ready[c], states)
    for i in ids:
        s_ref[i] = states[i]


def _gdn_all(proj, ba, conv_wt, a_log, dt_bias, gnorm, tb):
    b, t, width = proj.shape
    nh = GDN_HEADS
    lane_vec = lambda v: jnp.pad(v.reshape(1, nh), ((0, 0), (nh, LANES - 2 * nh)))
    full = lambda shape: pl.BlockSpec(shape, lambda ti: (0,) * len(shape))
    return pl.pallas_call(
        functools.partial(_gdn_all_kernel, tb=tb),
        grid=(t // tb,),
        in_specs=[pl.BlockSpec((b, tb, width), lambda ti: (0, ti, 0)),
                  pl.BlockSpec((b, tb, LANES), lambda ti: (0, ti, 0)),
                  full((GDN_CONV, 3 * GDN_W)), full((1, LANES)), full((1, LANES)), full((1, GDN_DV))],
        out_specs=pl.BlockSpec((b, tb, GDN_W), lambda ti: (0, ti, 0)),
        out_shape=jax.ShapeDtypeStruct((b, t, GDN_W), BF16),
        scratch_shapes=[pltpu.VMEM((b * nh, GDN_DK, GDN_DV), F32), pltpu.VMEM((b * 8, 3 * GDN_W), F32),
                        pltpu.VMEM((b * tb, 3 * GDN_W), F32), pltpu.VMEM((b, tb + 8, LANES), F32)],
        compiler_params=_params(("arbitrary",)),
        name="gdn",
    )(proj, ba, conv_wt, lane_vec(a_log), lane_vec(dt_bias), gnorm.reshape(1, GDN_DV))


MOBA_QW = 2 * MOBA_BLOCK
MOBA_ROWS_L = 16
MOBA_BOUND_SLACK = 1.02
MOBA_SAFE_LOG2_RANGE = 100.0


def _moba_select_kernel(qt_ref, km_ref, kn_ref, qa_ref, *, nb, tq):
    nbp = -(-nb // 16) * 16
    qt = qt_ref[0]
    km = km_ref[0].astype(BF16)
    if nbp > nb:
        km = jnp.concatenate([km, jnp.zeros((nbp - nb, LANES), BF16)], axis=0)
    gate = jnp.dot(km, qt, preferred_element_type=F32)
    row = lax.broadcasted_iota(jnp.int32, (nbp, tq), 0)
    qblk = (lax.broadcasted_iota(jnp.int32, (nbp, tq), 1) + pl.program_id(2) * tq) // MOBA_BLOCK
    past = row < qblk
    g = jnp.where(past, gate, -jnp.inf)
    sel = row < 0
    for _ in range(MOBA_TOPK):
        m = jnp.max(g, axis=0, keepdims=True)
        idx = jnp.min(jnp.where(g == m, row, nbp), axis=0, keepdims=True)
        hit = row == idx
        sel = sel | (hit & past)
        g = jnp.where(hit, -jnp.inf, g)
    q32 = qt.astype(F32)
    bound = jnp.sqrt(jnp.sum(q32 * q32, axis=0, keepdims=True)) * kn_ref[:, 0:1] * MOBA_BOUND_SLACK
    qa_ref[0, 0:LANES, :] = qt
    qa_ref[0, LANES:LANES + nbp, :] = jnp.where(sel, 0.0, MASK_NEG).astype(BF16)
    tail_row = lax.broadcasted_iota(jnp.int32, (LANES - nbp, tq), 0)
    qa_ref[0, LANES + nbp:, :] = jnp.where(tail_row == LANES - nbp - 1, -bound, 0.0).astype(BF16)


def _moba_select(qt, kmean, k_norm_bound, tq):
    b, _, t = qt.shape
    nb = t // MOBA_BLOCK
    assert -(-nb // 16) * 16 < LANES
    return pl.pallas_call(
        functools.partial(_moba_select_kernel, nb=nb, tq=tq),
        grid=(b, MOBA_HEADS, t // tq),
        in_specs=[pl.BlockSpec((1, LANES, tq), lambda bi, hi, qi: (bi, hi, qi)),
                  pl.BlockSpec((1, nb, LANES), lambda bi, hi, qi: (bi, 0, hi)),
                  pl.BlockSpec((1, LANES), lambda bi, hi, qi: (0, 0))],
        out_specs=pl.BlockSpec((1, 2 * LANES, tq), lambda bi, hi, qi: (bi, hi, qi)),
        out_shape=jax.ShapeDtypeStruct((b, 2 * MOBA_W, t), BF16),
        compiler_params=_params(("parallel", "parallel", "parallel")),
        name="moba_select",
    )(qt, kmean, k_norm_bound)


def _moba_kernel(qa_ref, ka_ref, vt_ref, o_ref, acc_ref, m_ref, sa_ref, sb_ref, xa_ref, xb_ref, *, nb, group):
    i2 = pl.program_id(2)
    BLK = MOBA_BLOCK

    def pv(p, start, width):
        lhs = jnp.concatenate([vt_ref[0, :, pl.ds(start, width)], jnp.ones((MOBA_ROWS_L, width), BF16)], axis=0)
        return jnp.dot(lhs, p, preferred_element_type=F32)

    key_i = lax.broadcasted_iota(jnp.int32, (BLK, BLK), 0)
    qry_i = lax.broadcasted_iota(jnp.int32, (BLK, BLK), 1)
    starts = [pl.multiple_of((2 * i2 + hf) * BLK, BLK) for hf in range(2)]
    s_own = [jnp.dot(ka_ref[0, pl.ds(starts[hf], BLK), 0:LANES], qa_ref[0, 0:LANES, hf * BLK:(hf + 1) * BLK],
                     preferred_element_type=F32) for hf in range(2)]
    s_own = [s_own[hf] + qa_ref[0, 2 * LANES - 1:2 * LANES, hf * BLK:(hf + 1) * BLK].astype(F32) for hf in range(2)]
    s_own = [jnp.where(key_i <= qry_i, s, MASK_NEG) for s in s_own]
    m_own = [jnp.max(s, axis=0, keepdims=True) for s in s_own]
    p_own = [jnp.exp2(s_own[hf] - m_own[hf]).astype(BF16) for hf in range(2)]
    for hf in range(2):
        m_ref[:, hf * BLK:(hf + 1) * BLK] = m_own[hf]
        acc_ref[:, hf * BLK:(hf + 1) * BLK] = pv(p_own[hf], starts[hf], BLK)

    span = group * BLK
    n_pairs = (2 * i2 + 2 * group) // (2 * group)
    last = nb // group - 1
    halves = [slice(hf * BLK, (hf + 1) * BLK) for hf in range(2)]

    def qk(g_idx, dst_ref, mx_dst, hs):
        start = pl.multiple_of(g_idx * span, span)
        s = jnp.dot(ka_ref[0, pl.ds(start, span), :], qa_ref[0, :, hs], preferred_element_type=F32)
        dst_ref[:, hs] = s
        mx_dst[:, hs] = jnp.max(s, axis=0, keepdims=True)

    def softmax_pv(src_ref, mx_src, g_idx, hs):
        m_old = m_ref[:, hs]
        m_new = jnp.maximum(m_old, mx_src[:, hs])
        alpha = jnp.exp2(m_old - m_new)
        m_ref[:, hs] = m_new
        p = jnp.exp2(src_ref[:, hs] - m_new).astype(BF16)
        acc_ref[:, hs] = alpha * acc_ref[:, hs] + pv(p, pl.multiple_of(g_idx * span, span), span)

    for hs in halves:
        qk(0, sa_ref, xa_ref, hs)

    def body(jj, carry):
        for hs in halves:
            qk(2 * jj + 1, sb_ref, xb_ref, hs)
            softmax_pv(sa_ref, xa_ref, 2 * jj, hs)
        for hs in halves:
            qk(jnp.minimum(2 * jj + 2, last), sa_ref, xa_ref, hs)
            softmax_pv(sb_ref, xb_ref, 2 * jj + 1, hs)
        return carry

    lax.fori_loop(0, n_pairs, body, 0)
    acc = acc_ref[...]
    o = acc[0:LANES] * (1.0 / acc[LANES:LANES + 1])
    o_ref[0] = o.T.astype(o_ref.dtype)


def _moba_bounded_kernel(qa_ref, ka_ref, vt_ref, o_ref, acc_ref, pa_ref, pb_ref, *, nb, group):
    i2 = pl.program_id(2)
    BLK = MOBA_BLOCK

    def pv(p, start, width):
        lhs = jnp.concatenate([vt_ref[0, :, pl.ds(start, width)], jnp.ones((MOBA_ROWS_L, width), BF16)], axis=0)
        return jnp.dot(lhs, p, preferred_element_type=F32)

    halves = [slice(hf * BLK, (hf + 1) * BLK) for hf in range(2)]
    key_i = lax.broadcasted_iota(jnp.int32, (BLK, BLK), 0)
    qry_i = lax.broadcasted_iota(jnp.int32, (BLK, BLK), 1)
    own = []
    for hf, hs in enumerate(halves):
        start = pl.multiple_of((2 * i2 + hf) * BLK, BLK)
        s = jnp.dot(ka_ref[0, pl.ds(start, BLK), 0:LANES], qa_ref[0, 0:LANES, hs], preferred_element_type=F32)
        s = s + qa_ref[0, 2 * LANES - 1:2 * LANES, hs].astype(F32)
        own.append((jnp.exp2(jnp.where(key_i <= qry_i, s, MASK_NEG)).astype(BF16), start))

    span = group * BLK
    n_groups = (2 * i2 + group) // group
    n_pairs = n_groups // 2
    last = nb // group - 1

    def qk_exp(g_idx, p_dst, hs):
        start = pl.multiple_of(g_idx * span, span)
        s = jnp.dot(ka_ref[0, pl.ds(start, span), :], qa_ref[0, :, hs], preferred_element_type=F32)
        p_dst[:, hs] = jnp.exp2(s).astype(BF16)

    def pv_acc(p_src, g_idx, hs):
        acc_ref[:, hs] += pv(p_src[:, hs], pl.multiple_of(g_idx * span, span), span)

    for hs in halves:
        qk_exp(0, pa_ref, hs)
    for (p, start), hs in zip(own, halves):
        acc_ref[:, hs] = pv(p, start, BLK)

    def body(jj, carry):
        for hs in halves:
            qk_exp(2 * jj + 1, pb_ref, hs)
            pv_acc(pa_ref, 2 * jj, hs)
        for hs in halves:
            qk_exp(jnp.minimum(2 * jj + 2, last), pa_ref, hs)
            pv_acc(pb_ref, 2 * jj + 1, hs)
        return carry

    lax.fori_loop(0, n_pairs, body, 0)

    @pl.when(n_groups % 2 == 1)
    def _():
        for hs in halves:
            pv_acc(pa_ref, 2 * n_pairs, hs)

    acc = acc_ref[...]
    o = acc[0:LANES] * (1.0 / acc[LANES:LANES + 1])
    o_ref[0] = o.T.astype(o_ref.dtype)


def _moba(qaug, kaug, vt):
    b, _, t = vt.shape
    nb = t // MOBA_BLOCK
    group = 4 if nb % 8 == 0 else 2
    assert nb % (2 * group) == 0
    span = group * MOBA_BLOCK

    def call(body, scratch, name):
        return pl.pallas_call(
            functools.partial(body, nb=nb, group=group),
            grid=(b, MOBA_HEADS, t // MOBA_QW),
            in_specs=[pl.BlockSpec((1, 2 * LANES, MOBA_QW), lambda bi, hi, qi: (bi, hi, qi)),
                      pl.BlockSpec((1, t, 2 * LANES), lambda bi, hi, qi: (bi, 0, hi)),
                      pl.BlockSpec((1, LANES, t), lambda bi, hi, qi: (bi, hi, 0))],
            out_specs=pl.BlockSpec((1, MOBA_QW, LANES), lambda bi, hi, qi: (bi, qi, hi)),
            out_shape=jax.ShapeDtypeStruct((b, t, MOBA_W), BF16),
            scratch_shapes=[pltpu.VMEM((LANES + MOBA_ROWS_L, MOBA_QW), F32)] + scratch,
            compiler_params=_params(("parallel", "parallel", "arbitrary")),
            name=name,
        )

    online = call(_moba_kernel,
                  [pltpu.VMEM((1, MOBA_QW), F32),
                   pltpu.VMEM((span, MOBA_QW), F32), pltpu.VMEM((span, MOBA_QW), F32),
                   pltpu.VMEM((1, MOBA_QW), F32), pltpu.VMEM((1, MOBA_QW), F32)], "moba")
    bounded = call(_moba_bounded_kernel,
                   [pltpu.VMEM((span, MOBA_QW), BF16), pltpu.VMEM((span, MOBA_QW), BF16)], "moba_bounded")
    max_bound = -jnp.min(qaug[:, 2 * LANES - 1::2 * LANES, :].astype(F32))
    return lax.cond(2.0 * max_bound < MOBA_SAFE_LOG2_RANGE, bounded, online, qaug, kaug, vt)


def _oproj_kernel(og_ref, om_ref, wa_ref, wb_ref, x_ref, nw_ref, h_ref, hn_ref):
    y = jnp.dot(og_ref[...], wa_ref[...], preferred_element_type=F32)
    y = y + jnp.dot(om_ref[...], wb_ref[...], preferred_element_type=F32)
    h = x_ref[...] + y
    h_ref[...] = h
    hn = h * lax.rsqrt(jnp.mean(h * h, axis=-1, keepdims=True) + RMS_EPS)
    hn_ref[...] = (hn * nw_ref[...]).astype(hn_ref.dtype)


def _oproj(og, om, wa, wb, x, norm_w, tm):
    m, d = x.shape
    ka, kb = og.shape[1], om.shape[1]
    row = lambda w: pl.BlockSpec((tm, w), lambda i: (i, 0))
    full = lambda r, c: pl.BlockSpec((r, c), lambda i: (0, 0))
    return pl.pallas_call(
        _oproj_kernel,
        grid=(m // tm,),
        in_specs=[row(ka), row(kb), full(ka, d), full(kb, d), row(d), full(1, d)],
        out_specs=[row(d), row(d)],
        out_shape=[jax.ShapeDtypeStruct((m, d), F32), jax.ShapeDtypeStruct((m, d), BF16)],
        compiler_params=_params(("parallel",)),
        name="oproj",
    )(og, om, wa, wb, x, norm_w.reshape(1, d))


def _ffn_kernel(hn_ref, wg_ref, wu_ref, wd_ref, h_ref, nw_ref, h2_ref, hn2_ref, acc_ref):
    f = pl.program_id(1)

    @pl.when(f == 0)
    def _():
        acc_ref[...] = jnp.zeros_like(acc_ref)

    hn = hn_ref[...]
    g = jnp.dot(hn, wg_ref[...], preferred_element_type=F32)
    u = jnp.dot(hn, wu_ref[...], preferred_element_type=F32)
    a = (g * _sigmoid(g) * u).astype(BF16)
    acc_ref[...] += jnp.dot(a, wd_ref[...], preferred_element_type=F32)

    @pl.when(f == pl.num_programs(1) - 1)
    def _():
        h2 = h_ref[...] + acc_ref[...]
        h2_ref[...] = h2
        n = h2 * lax.rsqrt(jnp.mean(h2 * h2, axis=-1, keepdims=True) + RMS_EPS)
        hn2_ref[...] = (n * nw_ref[...]).astype(hn2_ref.dtype)


def _ffn(hn, wg, wu, wd, h, norm_w, tm, tf):
    m, d = h.shape
    dff = wg.shape[1]
    row = pl.BlockSpec((tm, d), lambda i, f: (i, 0))
    return pl.pallas_call(
        _ffn_kernel,
        grid=(m // tm, dff // tf),
        in_specs=[row, pl.BlockSpec((d, tf), lambda i, f: (0, f)), pl.BlockSpec((d, tf), lambda i, f: (0, f)),
                  pl.BlockSpec((tf, d), lambda i, f: (f, 0)), row, pl.BlockSpec((1, d), lambda i, f: (0, 0))],
        out_specs=[row, row],
        out_shape=[jax.ShapeDtypeStruct((m, d), F32), jax.ShapeDtypeStruct((m, d), BF16)],
        scratch_shapes=[pltpu.VMEM((tm, d), F32)],
        compiler_params=_params(("parallel", "arbitrary")),
        name="ffn",
    )(hn, wg, wu, wd, h, norm_w.reshape(1, d))


def _ple_kernel(hn_ref, wg_ref, p_ref, wp_ref, h_ref, o_ref):
    gate = _sigmoid(jnp.dot(hn_ref[...], wg_ref[...], preferred_element_type=F32))
    proj = jnp.dot(p_ref[...].astype(BF16), wp_ref[...], preferred_element_type=F32)
    o_ref[...] = h_ref[...] + gate * proj


def _ple(hn, wg, p, wp, h, tm, tn):
    m, d = h.shape
    kp = p.shape[1]
    return pl.pallas_call(
        _ple_kernel,
        grid=(m // tm, d // tn),
        in_specs=[pl.BlockSpec((tm, d), lambda i, j: (i, 0)), pl.BlockSpec((d, tn), lambda i, j: (0, j)),
                  pl.BlockSpec((tm, kp), lambda i, j: (i, 0)), pl.BlockSpec((kp, tn), lambda i, j: (0, j)),
                  pl.BlockSpec((tm, tn), lambda i, j: (i, j))],
        out_specs=pl.BlockSpec((tm, tn), lambda i, j: (i, j)),
        out_shape=jax.ShapeDtypeStruct((m, d), F32),
        compiler_params=_params(("parallel", "arbitrary")),
        name="ple",
    )(hn, wg, p, wp, h)


def _layer(h, p, attn_norm, w_in, conv_w, a_log, dt_bias, gdn_norm, q_norm, k_norm, w_o, ffn_norm,
           w_gate, w_up, w_down, ple_norm, w_ple_gate, w_ple_proj):
    b, t, d = h.shape
    m = b * t
    x2 = h.reshape(m, d)
    tm = min(1024, m)

    o_ba = 4 * GDN_W
    o_mq = o_ba + 2 * GDN_HEADS
    w_in16 = w_in.astype(BF16)
    w_gdn = w_in16[:, :o_ba]
    w_ba = jnp.pad(w_in16[:, o_ba:o_mq], ((0, 0), (0, LANES - 2 * GDN_HEADS)))
    w_mq = w_in16[:, o_mq:o_mq + MOBA_W]
    w_mk = w_in16[:, o_mq + MOBA_W:o_mq + 2 * MOBA_W]
    w_mv = w_in16[:, o_mq + 2 * MOBA_W:o_mq + 3 * MOBA_W]

    xn = _rmsnorm(x2, attn_norm, min(512, m))
    gproj = _proj(xn, w_gdn, F32, tm, 1024, "proj_gdn")
    ba = _proj(xn, w_ba, F32, tm, LANES, "proj_ba")
    mq = _proj_moba_q(xn, w_mq, q_norm, tm, 1024)
    kaug, kmean = _proj_moba_k(xn, w_mk, k_norm, tm, 1024, t // MOBA_BLOCK)
    mv = _proj(xn, w_mv, BF16, tm, 1024, "proj_moba_v")

    o_gdn = _gdn_all(gproj.reshape(b, t, 4 * GDN_W), ba.reshape(b, t, LANES), conv_w.T, a_log, dt_bias,
                     gdn_norm, min(256, t))
    k_norm_bound = jnp.full((1, LANES), MOBA_DH ** 0.5, F32) * jnp.max(jnp.abs(k_norm))
    qaug = _moba_select(mq.reshape(b, t, MOBA_W).swapaxes(1, 2), kmean.reshape(b, t // MOBA_BLOCK, MOBA_W),
                        k_norm_bound, min(2048, t))
    o_moba = _moba(qaug, kaug.reshape(b, t, 2 * MOBA_W), mv.reshape(b, t, MOBA_W).swapaxes(1, 2))

    w_o16 = w_o.astype(BF16)
    h1, hn = _oproj(o_gdn.reshape(m, GDN_W), o_moba.reshape(m, MOBA_W), w_o16[:GDN_W], w_o16[GDN_W:],
                    x2, ffn_norm, min(512, m))
    h2, hn2 = _ffn(hn, w_gate.astype(BF16), w_up.astype(BF16), w_down.astype(BF16), h1, ple_norm,
                   min(512, m), 512)
    h3 = _ple(hn2, w_ple_gate.astype(BF16), p.reshape(m, PLE_DIM), w_ple_proj.astype(BF16), h2, tm, 1024)
    return h3.reshape(b, t, d)


def kernel(x, p, attn_norm, w_in, conv_w, A_log, dt_bias, gdn_norm, q_norm, k_norm, w_o, ffn_norm,
           w_gate, w_up, w_down, ple_norm, w_ple_gate, w_ple_proj):
    h = x
    for i in range(p.shape[0]):
        h = _layer(h, p[i], attn_norm[i], w_in[i], conv_w[i], A_log[i], dt_bias[i], gdn_norm[i],
                   q_norm[i], k_norm[i], w_o[i], ffn_norm[i], w_gate[i], w_up[i], w_down[i],
                   ple_norm[i], w_ple_gate[i], w_ple_proj[i])
    return h
```

```python
import functools

import jax
import jax.numpy as jnp
from jax import lax
from jax.experimental import pallas as pl
from jax.experimental.pallas import tpu as pltpu

D_MODEL = 2048
PLE_DIM = 256
GDN_HEADS = 8
GDN_DK = 128
GDN_DV = 128
GDN_CONV = 4
GDN_CHUNK = 64
GDN_CHUNKS_WIDE = 2
MOBA_HEADS = 8
MOBA_DH = 128
MOBA_BLOCK = 256
MOBA_TOPK = 3
RMS_EPS = 1e-6
GDN_W = GDN_HEADS * GDN_DK
MOBA_W = MOBA_HEADS * MOBA_DH
LANES = 128
LOG2E = 1.4426950408889634
MASK_NEG = -1e30

F32 = jnp.float32
BF16 = jnp.bfloat16

VMEM_LIMIT = 56 * 1024 * 1024


def _params(sem):
    return pltpu.CompilerParams(dimension_semantics=sem, vmem_limit_bytes=VMEM_LIMIT)


def _sigmoid(x):
    return 1.0 / (1.0 + jnp.exp(-x))


def _dot_t(a, b):
    return lax.dot_general(a, b, (((1,), (1,)), ((), ())), preferred_element_type=F32)


def _bdot(a, b):
    return jnp.dot(a.astype(BF16), b.astype(BF16), preferred_element_type=F32)


def _rmsnorm_kernel(x_ref, w_ref, o_ref):
    x = x_ref[...]
    y = x * lax.rsqrt(jnp.mean(x * x, axis=-1, keepdims=True) + RMS_EPS)
    o_ref[...] = (y * w_ref[...]).astype(o_ref.dtype)


def _rmsnorm(x, w, tm):
    m, d = x.shape
    return pl.pallas_call(
        _rmsnorm_kernel,
        grid=(m // tm,),
        in_specs=[pl.BlockSpec((tm, d), lambda i: (i, 0)), pl.BlockSpec((1, d), lambda i: (0, 0))],
        out_specs=pl.BlockSpec((tm, d), lambda i: (i, 0)),
        out_shape=jax.ShapeDtypeStruct((m, d), BF16),
        compiler_params=_params(("parallel",)),
        name="rmsnorm_cast",
    )(x, w.reshape(1, d))


def _proj_kernel(x_ref, w_ref, o_ref):
    o_ref[...] = jnp.dot(x_ref[...], w_ref[...], preferred_element_type=F32).astype(o_ref.dtype)


def _proj(x, w, out_dtype, tm, tn, name):
    m, k = x.shape
    n = w.shape[1]
    return pl.pallas_call(
        _proj_kernel,
        grid=(m // tm, n // tn),
        in_specs=[pl.BlockSpec((tm, k), lambda i, j: (i, 0)), pl.BlockSpec((k, tn), lambda i, j: (0, j))],
        out_specs=pl.BlockSpec((tm, tn), lambda i, j: (i, j)),
        out_shape=jax.ShapeDtypeStruct((m, n), out_dtype),
        compiler_params=_params(("parallel", "arbitrary")),
        name=name,
    )(x, w)


def _head_rmsnorm(y, gain, scale):
    outs = []
    for h in range(y.shape[1] // LANES):
        yh = y[:, h * LANES:(h + 1) * LANES]
        r = lax.rsqrt(jnp.mean(yh * yh, axis=-1, keepdims=True) + RMS_EPS)
        outs.append(yh * r * gain[:, h * LANES:(h + 1) * LANES] * scale)
    return jnp.concatenate(outs, axis=1)


def _proj_qnorm_kernel(x_ref, w_ref, g_ref, o_ref, *, scale):
    y = jnp.dot(x_ref[...], w_ref[...], preferred_element_type=F32)
    o_ref[...] = _head_rmsnorm(y, g_ref[...], scale).astype(o_ref.dtype)


def _proj_knorm_kernel(x_ref, w_ref, g_ref, o_ref, km_ref, *, nb_seq):
    y = jnp.dot(x_ref[...], w_ref[...], preferred_element_type=F32)
    yn = _head_rmsnorm(y, g_ref[...], 1.0)
    tm = yn.shape[0]
    row = lax.broadcasted_iota(jnp.int32, (tm, LANES), 0) + pl.program_id(0) * tm
    lane = lax.broadcasted_iota(jnp.int32, (tm, LANES), 1)
    onehot = jnp.where((lane == lax.rem(row // MOBA_BLOCK, nb_seq)) | (lane == LANES - 1), 1.0, 0.0)
    onehot = onehot.astype(o_ref.dtype)
    yb = yn.astype(o_ref.dtype)
    parts = []
    for h in range(yn.shape[1] // LANES):
        parts += [yb[:, h * LANES:(h + 1) * LANES], onehot]
    o_ref[...] = jnp.concatenate(parts, axis=1)
    for r in range(tm // MOBA_BLOCK):
        blk = yn[r * MOBA_BLOCK:(r + 1) * MOBA_BLOCK]
        km_ref[r] = jnp.mean(blk, axis=0, keepdims=True)


def _proj_moba_q(x, w, gain, tm, tn):
    m, k = x.shape
    n = w.shape[1]
    g = jnp.tile(gain.reshape(1, MOBA_DH), (1, n // MOBA_DH))
    return pl.pallas_call(
        functools.partial(_proj_qnorm_kernel, scale=MOBA_DH ** -0.5 * LOG2E),
        grid=(m // tm, n // tn),
        in_specs=[pl.BlockSpec((tm, k), lambda i, j: (i, 0)), pl.BlockSpec((k, tn), lambda i, j: (0, j)),
                  pl.BlockSpec((1, tn), lambda i, j: (0, j))],
        out_specs=pl.BlockSpec((tm, tn), lambda i, j: (i, j)),
        out_shape=jax.ShapeDtypeStruct((m, n), BF16),
        compiler_params=_params(("parallel", "arbitrary")),
        name="proj_moba_q",
    )(x, w, g)


def _proj_moba_k(x, w, gain, tm, tn, nb_seq):
    m, k = x.shape
    n = w.shape[1]
    g = jnp.tile(gain.reshape(1, MOBA_DH), (1, n // MOBA_DH))
    nb = tm // MOBA_BLOCK
    return pl.pallas_call(
        functools.partial(_proj_knorm_kernel, nb_seq=nb_seq),
        grid=(m // tm, n // tn),
        in_specs=[pl.BlockSpec((tm, k), lambda i, j: (i, 0)), pl.BlockSpec((k, tn), lambda i, j: (0, j)),
                  pl.BlockSpec((1, tn), lambda i, j: (0, j))],
        out_specs=[pl.BlockSpec((tm, 2 * tn), lambda i, j: (i, j)),
                   pl.BlockSpec((nb, 1, tn), lambda i, j: (i, 0, j))],
        out_shape=[jax.ShapeDtypeStruct((m, 2 * n), BF16),
                   jax.ShapeDtypeStruct((m // MOBA_BLOCK, 1, n), F32)],
        compiler_params=_params(("parallel", "arbitrary")),
        name="proj_moba_k",
    )(x, w, g)


def _gdn_all_kernel(x_ref, cw_ref, alog_ref, dtb_ref, gn_ref, o_ref, s_ref, tail_ref, qkv_s, xp_s, *, tb):
    t = pl.program_id(0)
    C = GDN_CHUNK
    nc = tb // C
    nh = GDN_HEADS
    nbat = x_ref.shape[0]

    @pl.when(t == 0)
    def _():
        s_ref[...] = jnp.zeros_like(s_ref)
        tail_ref[...] = jnp.zeros_like(tail_ref)

    def conv_group(gi, l2_scale):
        off = pl.multiple_of(gi * LANES, LANES)
        w = cw_ref[:, pl.ds(off, LANES)]
        for bi in range(nbat):
            x = x_ref[bi, :, pl.ds(off, LANES)]
            xp_s[bi, 0:8, :] = tail_ref[bi * 8:(bi + 1) * 8, pl.ds(off, LANES)]
            xp_s[bi, 8:8 + tb, :] = x
            y = x * w[GDN_CONV - 1:GDN_CONV]
            for j in range(GDN_CONV - 1):
                lag = GDN_CONV - 1 - j
                y = y + xp_s[bi, 8 - lag:8 - lag + tb, :] * w[j:j + 1]
            tail_ref[bi * 8:(bi + 1) * 8, pl.ds(off, LANES)] = x[tb - 8:tb]
            y = y * _sigmoid(y)
            if l2_scale is not None:
                y = y * (lax.rsqrt(jnp.sum(y * y, axis=-1, keepdims=True) + 1e-6) * l2_scale)
            qkv_s[bi * tb:(bi + 1) * tb, pl.ds(off, LANES)] = y

    for g0, l2_scale in ((0, GDN_DK ** -0.5), (nh, 1.0), (2 * nh, None)):
        lax.fori_loop(g0, g0 + nh, lambda gi, carry, sc=l2_scale: conv_group(gi, sc), None)

    row_in_chunk = lax.broadcasted_iota(jnp.int32, (tb, LANES), 0) & (C - 1)
    beta_all, gc, egc, kdsc, egl, gc_t = [], [], [], [], [], []
    for bi in range(nbat):
        ba = x_ref[bi, :, 4 * GDN_W:4 * GDN_W + LANES]
        beta_all.append(_sigmoid(ba))
        sp_in = ba + dtb_ref[...]
        softplus = jnp.maximum(sp_in, 0.0) + jnp.log1p(jnp.exp(-jnp.abs(sp_in)))
        g = -(jnp.exp(alog_ref[...]) * softplus)
        sh = 1
        while sh < C:
            g = g + jnp.where(row_in_chunk >= sh, pltpu.roll(g, sh, axis=0), 0.0)
            sh *= 2
        gl_rows = [g[(c + 1) * C - 1:(c + 1) * C] for c in range(nc)]
        gl_b = jnp.concatenate([jnp.broadcast_to(r, (C, LANES)) for r in gl_rows], axis=0)
        gc.append(g)
        egc.append(jnp.exp(g))
        kdsc.append(jnp.exp(gl_b - g))
        egl.append([jnp.exp(r) for r in gl_rows])
        gc_t.append(g.T)

    ri = lax.broadcasted_iota(jnp.int32, (C, 2 * C), 0)
    lane = lax.broadcasted_iota(jnp.int32, (C, 2 * C), 1)
    ci = lane & (C - 1)
    left = lane < C
    tril = ri >= ci
    strict = ri > ci
    eye_f = (ri == ci).astype(F32)
    lvl_masks = []
    s = 1
    while s < C:
        sh2 = s.bit_length()
        lvl_masks.append((ri >> sh2 == ci >> sh2) & ((ri & (2 * s - 1)) >= s) & ((ci & (2 * s - 1)) < s))
        s *= 2
    gn = gn_ref[...]

    def block_diag(m_l, m_r):
        z = jnp.zeros_like(m_l)
        return jnp.concatenate([jnp.concatenate([m_l, z], axis=1), jnp.concatenate([z, m_r], axis=1)], axis=0)

    def unpack_diag(m):
        return jnp.concatenate([jnp.where(left, m, 0.0), jnp.where(left, 0.0, m)], axis=0)

    seqs = [(bi, h) for bi in range(nbat) for h in range(nh)]
    ids = range(len(seqs))
    pairs = [(2 * p, 2 * p + 1) for p in range(len(seqs) // 2)]
    pids = range(len(pairs))
    states = [s_ref[i] for i in ids]
    grp = lambda g, h: slice((g * nh + h) * LANES, (g * nh + h + 1) * LANES)
    def operands(c):
        rows = slice(c * C, (c + 1) * C)
        srow = lambda bi: slice(bi * tb + c * C, bi * tb + (c + 1) * C)
        col = lambda arr, bi, h: arr[bi][rows, nh + h:nh + h + 1]
        d = dict(rows=rows,
                 qc=[qkv_s[srow(bi), grp(0, h)] for bi, h in seqs],
                 kc=[qkv_s[srow(bi), grp(1, h)] for bi, h in seqs],
                 vc=[qkv_s[srow(bi), grp(2, h)] for bi, h in seqs],
                 bcol=[beta_all[bi][rows, h:h + 1] for bi, h in seqs],
                 ecol=[col(egc, bi, h) for bi, h in seqs],
                 gcol=[col(gc, bi, h) for bi, h in seqs],
                 kdcol=[col(kdsc, bi, h) for bi, h in seqs],
                 grow=[gc_t[bi][nh + h:nh + h + 1, rows] for bi, h in seqs])
        d["k_beta"] = [d["kc"][i] * d["bcol"][i] for i in ids]
        return d

    def stateless(chunks):
        ops = {c: operands(c) for c in chunks}
        items = [(c, p) for c in chunks for p in pids]
        nit = range(len(items))
        decay, st = [], []
        for c, p in items:
            d, (l, r) = ops[c], pairs[p]
            decay.append(jnp.exp(jnp.where(
                tril, jnp.where(left, d["gcol"][l], d["gcol"][r])
                - jnp.concatenate([d["grow"][l], d["grow"][r]], axis=1), -jnp.inf)))
        for c, p in items:
            d, (l, r) = ops[c], pairs[p]
            st.append(_dot_t(jnp.concatenate([jnp.concatenate([d["k_beta"][l], d["k_beta"][r]], axis=1),
                                              jnp.concatenate([d["qc"][l], d["qc"][r]], axis=1)],
                                             axis=0).astype(BF16),
                             block_diag(d["kc"][l], d["kc"][r]).astype(BF16)))
        a_mat = [jnp.where(strict, st[n][:C] * decay[n], 0.0) for n in nit]
        qk = [jnp.where(tril, st[n][C:] * decay[n], 0.0) for n in nit]
        t_inv = [eye_f - jnp.where(lvl_masks[0], a_mat[n], 0.0) for n in nit]
        for msk in lvl_masks[1:]:
            x_mid = [_bdot(jnp.where(msk, a_mat[n], 0.0), unpack_diag(t_inv[n])) for n in nit]
            y_mid = [_bdot(t_inv[n], unpack_diag(x_mid[n])) for n in nit]
            t_inv = [t_inv[n] - y_mid[n] for n in nit]
        uw_p = []
        for n, (c, p) in enumerate(items):
            d, (l, r) = ops[c], pairs[p]
            rhs = [jnp.concatenate([d["vc"][i] * d["bcol"][i], d["k_beta"][i] * d["ecol"][i]], axis=1)
                   for i in (l, r)]
            uw_p.append(_bdot(t_inv[n], block_diag(rhs[0], rhs[1])))
        out = {}
        for k, c in enumerate(chunks):
            base = k * len(pairs)
            uw = [uw_p[base + i // 2][:, (i % 2) * 2 * LANES:(i % 2 + 1) * 2 * LANES] for i in ids]
            out[c] = (ops[c], uw, qk[base:base + len(pairs)])
        return out

    def recurrent(c, d, uw, qk, states):
        wq = [_bdot(jnp.concatenate([uw[i][:, LANES:], d["qc"][i] * d["ecol"][i]], axis=0), states[i])
              for i in ids]
        v_new = [uw[i][:, :LANES] - wq[i][:C] for i in ids]
        o_p = [_bdot(qk[p], block_diag(v_new[l], v_new[r])) for p, (l, r) in enumerate(pairs)]
        o = [wq[i][C:] + o_p[i // 2][:, (i % 2) * LANES:(i % 2 + 1) * LANES] for i in ids]
        new_states = [states[i] * egl[bi][c][:, nh + h:nh + h + 1] + lax.dot_general(
            (d["kc"][i] * d["kdcol"][i]).astype(BF16), v_new[i].astype(BF16), (((0,), (0,)), ((), ())),
            preferred_element_type=F32) for i, (bi, h) in enumerate(seqs)]
        for i, (bi, h) in enumerate(seqs):
            on = o[i] * lax.rsqrt(jnp.mean(o[i] * o[i], axis=-1, keepdims=True) + RMS_EPS) * gn
            zc = x_ref[bi, d["rows"], grp(3, h)]
            o_ref[bi, d["rows"], grp(0, h)] = (on * (zc * _sigmoid(zc))).astype(o_ref.dtype)
        return new_states

    for c0 in range(0, nc, GDN_CHUNKS_WIDE):
        chunks = list(range(c0, min(c0 + GDN_CHUNKS_WIDE, nc)))
        ready = stateless(chunks)
        for c in chunks:
            states = recurrent(c, *ready[c], states)
    for i in ids:
        s_ref[i] = states[i]


def _gdn_all(proj, conv_wt, a_log, dt_bias, gnorm, tb):
    b, t, width = proj.shape
    nh = GDN_HEADS
    lane_vec = lambda v: jnp.pad(v.reshape(1, nh), ((0, 0), (nh, LANES - 2 * nh)))
    full = lambda shape: pl.BlockSpec(shape, lambda ti: (0,) * len(shape))
    return pl.pallas_call(
        functools.partial(_gdn_all_kernel, tb=tb),
        grid=(t // tb,),
        in_specs=[pl.BlockSpec((b, tb, width), lambda ti: (0, ti, 0)),
                  full((GDN_CONV, 3 * GDN_W)), full((1, LANES)), full((1, LANES)), full((1, GDN_DV))],
        out_specs=pl.BlockSpec((b, tb, GDN_W), lambda ti: (0, ti, 0)),
        out_shape=jax.ShapeDtypeStruct((b, t, GDN_W), BF16),
        scratch_shapes=[pltpu.VMEM((b * nh, GDN_DK, GDN_DV), F32), pltpu.VMEM((b * 8, 3 * GDN_W), F32),
                        pltpu.VMEM((b * tb, 3 * GDN_W), F32), pltpu.VMEM((b, tb + 8, LANES), F32)],
        compiler_params=_params(("arbitrary",)),
        name="gdn",
    )(proj, conv_wt, lane_vec(a_log), lane_vec(dt_bias), gnorm.reshape(1, GDN_DV))


MOBA_QW = 2 * MOBA_BLOCK
MOBA_ROWS_L = 16
MOBA_BOUND_SLACK = 1.02
MOBA_SAFE_LOG2_RANGE = 100.0


def _moba_select_kernel(qt_ref, km_ref, kn_ref, qa_ref, bm_ref, *, nb, tq):
    nbp = -(-nb // 16) * 16
    qt = qt_ref[0]
    km = km_ref[0].astype(BF16)
    if nbp > nb:
        km = jnp.concatenate([km, jnp.zeros((nbp - nb, LANES), BF16)], axis=0)
    gate = jnp.dot(km, qt, preferred_element_type=F32)
    row = lax.broadcasted_iota(jnp.int32, (nbp, tq), 0)
    qblk = (lax.broadcasted_iota(jnp.int32, (nbp, tq), 1) + pl.program_id(2) * tq) // MOBA_BLOCK
    past = row < qblk
    g = jnp.where(past, gate, -jnp.inf)
    sel = row < 0
    for _ in range(MOBA_TOPK):
        m = jnp.max(g, axis=0, keepdims=True)
        idx = jnp.min(jnp.where(g == m, row, nbp), axis=0, keepdims=True)
        hit = row == idx
        sel = sel | (hit & past)
        g = jnp.where(hit, -jnp.inf, g)
    q32 = qt.astype(F32)
    bound = jnp.sqrt(jnp.sum(q32 * q32, axis=0, keepdims=True)) * kn_ref[:, 0:1] * MOBA_BOUND_SLACK
    qa_ref[0, 0:LANES, :] = qt
    qa_ref[0, LANES:LANES + nbp, :] = jnp.where(sel, 0.0, MASK_NEG).astype(BF16)
    tail_row = lax.broadcasted_iota(jnp.int32, (LANES - nbp, tq), 0)
    qa_ref[0, LANES + nbp:, :] = jnp.where(tail_row == LANES - nbp - 1, -bound, 0.0).astype(BF16)
    stored = bound.astype(BF16).astype(F32)
    bm_ref[0, 0] = jnp.broadcast_to(jnp.max(stored, axis=1, keepdims=True), (1, LANES))


def _moba_select(qt, kmean, k_norm_bound, tq):
    b, _, t = qt.shape
    nb = t // MOBA_BLOCK
    assert -(-nb // 16) * 16 < LANES
    qaug, bmax = pl.pallas_call(
        functools.partial(_moba_select_kernel, nb=nb, tq=tq),
        grid=(b, MOBA_HEADS, t // tq),
        in_specs=[pl.BlockSpec((1, LANES, tq), lambda bi, hi, qi: (bi, hi, qi)),
                  pl.BlockSpec((1, nb, LANES), lambda bi, hi, qi: (bi, 0, hi)),
                  pl.BlockSpec((1, LANES), lambda bi, hi, qi: (0, 0))],
        out_specs=[pl.BlockSpec((1, 2 * LANES, tq), lambda bi, hi, qi: (bi, hi, qi)),
                   pl.BlockSpec((1, 1, 1, LANES), lambda bi, hi, qi: (bi, hi * (t // tq) + qi, 0, 0))],
        out_shape=[jax.ShapeDtypeStruct((b, 2 * MOBA_W, t), BF16),
                   jax.ShapeDtypeStruct((b, MOBA_HEADS * (t // tq), 1, LANES), F32)],
        compiler_params=_params(("parallel", "parallel", "parallel")),
        name="moba_select",
    )(qt, kmean, k_norm_bound)
    return qaug, jnp.max(bmax)


def _moba_kernel(qa_ref, ka_ref, vt_ref, o_ref, acc_ref, m_ref, sa_ref, sb_ref, xa_ref, xb_ref, *, nb, group):
    i2 = pl.program_id(2)
    BLK = MOBA_BLOCK

    def pv(p, start, width):
        lhs = jnp.concatenate([vt_ref[0, :, pl.ds(start, width)], jnp.ones((MOBA_ROWS_L, width), BF16)], axis=0)
        return jnp.dot(lhs, p, preferred_element_type=F32)

    key_i = lax.broadcasted_iota(jnp.int32, (BLK, BLK), 0)
    qry_i = lax.broadcasted_iota(jnp.int32, (BLK, BLK), 1)
    starts = [pl.multiple_of((2 * i2 + hf) * BLK, BLK) for hf in range(2)]
    s_own = [jnp.dot(ka_ref[0, pl.ds(starts[hf], BLK), 0:LANES], qa_ref[0, 0:LANES, hf * BLK:(hf + 1) * BLK],
                     preferred_element_type=F32) for hf in range(2)]
    s_own = [s_own[hf] + qa_ref[0, 2 * LANES - 1:2 * LANES, hf * BLK:(hf + 1) * BLK].astype(F32) for hf in range(2)]
    s_own = [jnp.where(key_i <= qry_i, s, MASK_NEG) for s in s_own]
    m_own = [jnp.max(s, axis=0, keepdims=True) for s in s_own]
    p_own = [jnp.exp2(s_own[hf] - m_own[hf]).astype(BF16) for hf in range(2)]
    for hf in range(2):
        m_ref[:, hf * BLK:(hf + 1) * BLK] = m_own[hf]
        acc_ref[:, hf * BLK:(hf + 1) * BLK] = pv(p_own[hf], starts[hf], BLK)

    span = group * BLK
    n_pairs = (2 * i2 + 2 * group) // (2 * group)
    last = nb // group - 1
    halves = [slice(hf * BLK, (hf + 1) * BLK) for hf in range(2)]

    def qk(g_idx, dst_ref, mx_dst, hs):
        start = pl.multiple_of(g_idx * span, span)
        s = jnp.dot(ka_ref[0, pl.ds(start, span), :], qa_ref[0, :, hs], preferred_element_type=F32)
        dst_ref[:, hs] = s
        mx_dst[:, hs] = jnp.max(s, axis=0, keepdims=True)

    def softmax_pv(src_ref, mx_src, g_idx, hs):
        m_old = m_ref[:, hs]
        m_new = jnp.maximum(m_old, mx_src[:, hs])
        alpha = jnp.exp2(m_old - m_new)
        m_ref[:, hs] = m_new
        p = jnp.exp2(src_ref[:, hs] - m_new).astype(BF16)
        acc_ref[:, hs] = alpha * acc_ref[:, hs] + pv(p, pl.multiple_of(g_idx * span, span), span)

    for hs in halves:
        qk(0, sa_ref, xa_ref, hs)

    def body(jj, carry):
        for hs in halves:
            qk(2 * jj + 1, sb_ref, xb_ref, hs)
            softmax_pv(sa_ref, xa_ref, 2 * jj, hs)
        for hs in halves:
            qk(jnp.minimum(2 * jj + 2, last), sa_ref, xa_ref, hs)
            softmax_pv(sb_ref, xb_ref, 2 * jj + 1, hs)
        return carry

    lax.fori_loop(0, n_pairs, body, 0)
    acc = acc_ref[...]
    o = acc[0:LANES] * (1.0 / acc[LANES:LANES + 1])
    o_ref[0] = o.T.astype(o_ref.dtype)


def _moba_bounded_kernel(qa_ref, ka_ref, vt_ref, o_ref, acc_ref, pa_ref, pb_ref, *, nb, group):
    i2 = pl.program_id(2)
    BLK = MOBA_BLOCK

    def pv(p, start, width):
        lhs = jnp.concatenate([vt_ref[0, :, pl.ds(start, width)], jnp.ones((MOBA_ROWS_L, width), BF16)], axis=0)
        return jnp.dot(lhs, p, preferred_element_type=F32)

    halves = [slice(hf * BLK, (hf + 1) * BLK) for hf in range(2)]
    key_i = lax.broadcasted_iota(jnp.int32, (BLK, BLK), 0)
    qry_i = lax.broadcasted_iota(jnp.int32, (BLK, BLK), 1)
    own = []
    for hf, hs in enumerate(halves):
        start = pl.multiple_of((2 * i2 + hf) * BLK, BLK)
        s = jnp.dot(ka_ref[0, pl.ds(start, BLK), 0:LANES], qa_ref[0, 0:LANES, hs], preferred_element_type=F32)
        s = s + qa_ref[0, 2 * LANES - 1:2 * LANES, hs].astype(F32)
        own.append((jnp.exp2(jnp.where(key_i <= qry_i, s, MASK_NEG)).astype(BF16), start))

    span = group * BLK
    n_groups = (2 * i2 + group) // group
    n_pairs = n_groups // 2
    last = nb // group - 1

    def qk_exp(g_idx, p_dst, hs):
        start = pl.multiple_of(g_idx * span, span)
        s = jnp.dot(ka_ref[0, pl.ds(start, span), :], qa_ref[0, :, hs], preferred_element_type=F32)
        p_dst[:, hs] = jnp.exp2(s).astype(BF16)

    def pv_acc(p_src, g_idx, hs):
        acc_ref[:, hs] += pv(p_src[:, hs], pl.multiple_of(g_idx * span, span), span)

    for hs in halves:
        qk_exp(0, pa_ref, hs)
    for (p, start), hs in zip(own, halves):
        acc_ref[:, hs] = pv(p, start, BLK)

    def body(jj, carry):
        for hs in halves:
            qk_exp(2 * jj + 1, pb_ref, hs)
            pv_acc(pa_ref, 2 * jj, hs)
        for hs in halves:
            qk_exp(jnp.minimum(2 * jj + 2, last), pa_ref, hs)
            pv_acc(pb_ref, 2 * jj + 1, hs)
        return carry

    lax.fori_loop(0, n_pairs, body, 0)

    @pl.when(n_groups % 2 == 1)
    def _():
        for hs in halves:
            pv_acc(pa_ref, 2 * n_pairs, hs)

    acc = acc_ref[...]
    o = acc[0:LANES] * (1.0 / acc[LANES:LANES + 1])
    o_ref[0] = o.T.astype(o_ref.dtype)


def _moba(qaug, kaug, vt, max_bound):
    b, _, t = vt.shape
    nb = t // MOBA_BLOCK
    group = 4 if nb % 8 == 0 else 2
    assert nb % (2 * group) == 0
    span = group * MOBA_BLOCK

    def call(body, scratch, name):
        return pl.pallas_call(
            functools.partial(body, nb=nb, group=group),
            grid=(b, MOBA_HEADS, t // MOBA_QW),
            in_specs=[pl.BlockSpec((1, 2 * LANES, MOBA_QW), lambda bi, hi, qi: (bi, hi, qi)),
                      pl.BlockSpec((1, t, 2 * LANES), lambda bi, hi, qi: (bi, 0, hi)),
                      pl.BlockSpec((1, LANES, t), lambda bi, hi, qi: (bi, hi, 0))],
            out_specs=pl.BlockSpec((1, MOBA_QW, LANES), lambda bi, hi, qi: (bi, qi, hi)),
            out_shape=jax.ShapeDtypeStruct((b, t, MOBA_W), BF16),
            scratch_shapes=[pltpu.VMEM((LANES + MOBA_ROWS_L, MOBA_QW), F32)] + scratch,
            compiler_params=_params(("parallel", "parallel", "arbitrary")),
            name=name,
        )

    online = call(_moba_kernel,
                  [pltpu.VMEM((1, MOBA_QW), F32),
                   pltpu.VMEM((span, MOBA_QW), F32), pltpu.VMEM((span, MOBA_QW), F32),
                   pltpu.VMEM((1, MOBA_QW), F32), pltpu.VMEM((1, MOBA_QW), F32)], "moba")
    bounded = call(_moba_bounded_kernel,
                   [pltpu.VMEM((span, MOBA_QW), BF16), pltpu.VMEM((span, MOBA_QW), BF16)], "moba_bounded")
    return lax.cond(2.0 * max_bound < MOBA_SAFE_LOG2_RANGE, bounded, online, qaug, kaug, vt)


def _oproj_kernel(og_ref, om_ref, wa_ref, wb_ref, x_ref, nw_ref, h_ref, hn_ref):
    y = jnp.dot(og_ref[...], wa_ref[...], preferred_element_type=F32)
    y = y + jnp.dot(om_ref[...], wb_ref[...], preferred_element_type=F32)
    h = x_ref[...] + y
    h_ref[...] = h
    hn = h * lax.rsqrt(jnp.mean(h * h, axis=-1, keepdims=True) + RMS_EPS)
    hn_ref[...] = (hn * nw_ref[...]).astype(hn_ref.dtype)


def _oproj(og, om, wa, wb, x, norm_w, tm):
    m, d = x.shape
    ka, kb = og.shape[1], om.shape[1]
    row = lambda w: pl.BlockSpec((tm, w), lambda i: (i, 0))
    full = lambda r, c: pl.BlockSpec((r, c), lambda i: (0, 0))
    return pl.pallas_call(
        _oproj_kernel,
        grid=(m // tm,),
        in_specs=[row(ka), row(kb), full(ka, d), full(kb, d), row(d), full(1, d)],
        out_specs=[row(d), row(d)],
        out_shape=[jax.ShapeDtypeStruct((m, d), F32), jax.ShapeDtypeStruct((m, d), BF16)],
        compiler_params=_params(("parallel",)),
        name="oproj",
    )(og, om, wa, wb, x, norm_w.reshape(1, d))


def _ffn_kernel(hn_ref, wg_ref, wu_ref, wd_ref, h_ref, nw_ref, h2_ref, hn2_ref, acc_ref):
    f = pl.program_id(1)

    @pl.when(f == 0)
    def _():
        acc_ref[...] = jnp.zeros_like(acc_ref)

    hn = hn_ref[...]
    g = jnp.dot(hn, wg_ref[...], preferred_element_type=F32)
    u = jnp.dot(hn, wu_ref[...], preferred_element_type=F32)
    a = (g * _sigmoid(g) * u).astype(BF16)
    acc_ref[...] += jnp.dot(a, wd_ref[...], preferred_element_type=F32)

    @pl.when(f == pl.num_programs(1) - 1)
    def _():
        h2 = h_ref[...] + acc_ref[...]
        h2_ref[...] = h2
        n = h2 * lax.rsqrt(jnp.mean(h2 * h2, axis=-1, keepdims=True) + RMS_EPS)
        hn2_ref[...] = (n * nw_ref[...]).astype(hn2_ref.dtype)


def _ffn(hn, wg, wu, wd, h, norm_w, tm, tf):
    m, d = h.shape
    dff = wg.shape[1]
    row = pl.BlockSpec((tm, d), lambda i, f: (i, 0))
    return pl.pallas_call(
        _ffn_kernel,
        grid=(m // tm, dff // tf),
        in_specs=[row, pl.BlockSpec((d, tf), lambda i, f: (0, f)), pl.BlockSpec((d, tf), lambda i, f: (0, f)),
                  pl.BlockSpec((tf, d), lambda i, f: (f, 0)), row, pl.BlockSpec((1, d), lambda i, f: (0, 0))],
        out_specs=[row, row],
        out_shape=[jax.ShapeDtypeStruct((m, d), F32), jax.ShapeDtypeStruct((m, d), BF16)],
        scratch_shapes=[pltpu.VMEM((tm, d), F32)],
        compiler_params=_params(("parallel", "arbitrary")),
        name="ffn",
    )(hn, wg, wu, wd, h, norm_w.reshape(1, d))


def _ple_kernel(hn_ref, wg_ref, p_ref, wp_ref, h_ref, o_ref):
    gate = _sigmoid(jnp.dot(hn_ref[...], wg_ref[...], preferred_element_type=F32))
    proj = jnp.dot(p_ref[...].astype(BF16), wp_ref[...], preferred_element_type=F32)
    o_ref[...] = h_ref[...] + gate * proj


def _ple(hn, wg, p, wp, h, tm, tn):
    m, d = h.shape
    kp = p.shape[1]
    return pl.pallas_call(
        _ple_kernel,
        grid=(m // tm, d // tn),
        in_specs=[pl.BlockSpec((tm, d), lambda i, j: (i, 0)), pl.BlockSpec((d, tn), lambda i, j: (0, j)),
                  pl.BlockSpec((tm, kp), lambda i, j: (i, 0)), pl.BlockSpec((kp, tn), lambda i, j: (0, j)),
                  pl.BlockSpec((tm, tn), lambda i, j: (i, j))],
        out_specs=pl.BlockSpec((tm, tn), lambda i, j: (i, j)),
        out_shape=jax.ShapeDtypeStruct((m, d), F32),
        compiler_params=_params(("parallel", "arbitrary")),
        name="ple",
    )(hn, wg, p, wp, h)


def _layer(h, p, attn_norm, w_in, conv_w, a_log, dt_bias, gdn_norm, q_norm, k_norm, w_o, ffn_norm,
           w_gate, w_up, w_down, ple_norm, w_ple_gate, w_ple_proj):
    b, t, d = h.shape
    m = b * t
    x2 = h.reshape(m, d)
    tm = min(1024, m)

    o_ba = 4 * GDN_W
    o_mq = o_ba + 2 * GDN_HEADS
    w_in16 = w_in.astype(BF16)
    w_gdn = jnp.pad(w_in16[:, :o_mq], ((0, 0), (0, LANES - 2 * GDN_HEADS)))
    w_mq = w_in16[:, o_mq:o_mq + MOBA_W]
    w_mk = w_in16[:, o_mq + MOBA_W:o_mq + 2 * MOBA_W]
    w_mv = w_in16[:, o_mq + 2 * MOBA_W:o_mq + 3 * MOBA_W]

    xn = _rmsnorm(x2, attn_norm, min(512, m))
    gproj = _proj(xn, w_gdn, F32, tm, w_gdn.shape[1] // 3, "proj_gdn")
    mq = _proj_moba_q(xn, w_mq, q_norm, tm, 1024)
    kaug, kmean = _proj_moba_k(xn, w_mk, k_norm, tm, 1024, t // MOBA_BLOCK)
    mv = _proj(xn, w_mv, BF16, tm, 1024, "proj_moba_v")

    o_gdn = _gdn_all(gproj.reshape(b, t, 4 * GDN_W + LANES), conv_w.T, a_log, dt_bias, gdn_norm, min(256, t))
    k_norm_bound = jnp.full((1, LANES), MOBA_DH ** 0.5, F32) * jnp.max(jnp.abs(k_norm))
    qaug, max_bound = _moba_select(mq.reshape(b, t, MOBA_W).swapaxes(1, 2),
                                   kmean.reshape(b, t // MOBA_BLOCK, MOBA_W), k_norm_bound, min(2048, t))
    o_moba = _moba(qaug, kaug.reshape(b, t, 2 * MOBA_W), mv.reshape(b, t, MOBA_W).swapaxes(1, 2), max_bound)

    w_o16 = w_o.astype(BF16)
    h1, hn = _oproj(o_gdn.reshape(m, GDN_W), o_moba.reshape(m, MOBA_W), w_o16[:GDN_W], w_o16[GDN_W:],
                    x2, ffn_norm, min(512, m))
    h2, hn2 = _ffn(hn, w_gate.astype(BF16), w_up.astype(BF16), w_down.astype(BF16), h1, ple_norm,
                   min(512, m), 512)
    h3 = _ple(hn2, w_ple_gate.astype(BF16), p.reshape(m, PLE_DIM), w_ple_proj.astype(BF16), h2, tm, 1024)
    return h3.reshape(b, t, d)


def kernel(x, p, attn_norm, w_in, conv_w, A_log, dt_bias, gdn_norm, q_norm, k_norm, w_o, ffn_norm,
           w_gate, w_up, w_down, ple_norm, w_ple_gate, w_ple_proj):
    h = x
    for i in range(p.shape[0]):
        h = _layer(h, p[i], attn_norm[i], w_in[i], conv_w[i], A_log[i], dt_bias[i], gdn_norm[i],
                   q_norm[i], k_norm[i], w_o[i], ffn_norm[i], w_gate[i], w_up[i], w_down[i],
                   ple_norm[i], w_ple_gate[i], w_ple_proj[i])
    return h
```

```python
import functools
from typing import NamedTuple

import jax
import jax.numpy as jnp
from jax import lax
from jax.experimental import pallas as pl
from jax.experimental.pallas import tpu as pltpu

D_MODEL = 2048
PLE_DIM = 256
GDN_HEADS = 8
GDN_DK = 128
GDN_DV = 128
GDN_CONV = 4
GDN_CHUNK = 64
GDN_CHUNKS_WIDE = 2
MOBA_HEADS = 8
MOBA_DH = 128
MOBA_BLOCK = 256
MOBA_TOPK = 3
RMS_EPS = 1e-6
GDN_W = GDN_HEADS * GDN_DK
MOBA_W = MOBA_HEADS * MOBA_DH
LANES = 128
SUBLANES = 8
BF16_ROWS = 16
GDN_L2_EPS = 1e-6
LOG2E = 1.4426950408889634
MASK_NEG = -1e30

F32 = jnp.float32
BF16 = jnp.bfloat16

VMEM_LIMIT = 56 * 1024 * 1024


def _params(sem):
    return pltpu.CompilerParams(dimension_semantics=sem, vmem_limit_bytes=VMEM_LIMIT)


def _sigmoid(x):
    return 1.0 / (1.0 + jnp.exp(-x))


def _dot_t(a, b):
    return lax.dot_general(a, b, (((1,), (1,)), ((), ())), preferred_element_type=F32)


def _bdot(a, b):
    return jnp.dot(a.astype(BF16), b.astype(BF16), preferred_element_type=F32)


def _rmsnorm_kernel(x_ref, w_ref, o_ref):
    x = x_ref[...]
    y = x * lax.rsqrt(jnp.mean(x * x, axis=-1, keepdims=True) + RMS_EPS)
    o_ref[...] = (y * w_ref[...]).astype(o_ref.dtype)


def _rmsnorm(x, w, tm):
    m, d = x.shape
    return pl.pallas_call(
        _rmsnorm_kernel,
        grid=(m // tm,),
        in_specs=[pl.BlockSpec((tm, d), lambda i: (i, 0)), pl.BlockSpec((1, d), lambda i: (0, 0))],
        out_specs=pl.BlockSpec((tm, d), lambda i: (i, 0)),
        out_shape=jax.ShapeDtypeStruct((m, d), BF16),
        compiler_params=_params(("parallel",)),
        name="rmsnorm_cast",
    )(x, w.reshape(1, d))


def _proj_kernel(x_ref, w_ref, o_ref):
    o_ref[...] = jnp.dot(x_ref[...], w_ref[...], preferred_element_type=F32).astype(o_ref.dtype)


def _proj(x, w, out_dtype, tm, tn, name):
    m, k = x.shape
    n = w.shape[1]
    return pl.pallas_call(
        _proj_kernel,
        grid=(m // tm, n // tn),
        in_specs=[pl.BlockSpec((tm, k), lambda i, j: (i, 0)), pl.BlockSpec((k, tn), lambda i, j: (0, j))],
        out_specs=pl.BlockSpec((tm, tn), lambda i, j: (i, j)),
        out_shape=jax.ShapeDtypeStruct((m, n), out_dtype),
        compiler_params=_params(("parallel", "arbitrary")),
        name=name,
    )(x, w)


def _head_rmsnorm(y, gain, scale):
    outs = []
    for h in range(y.shape[1] // LANES):
        yh = y[:, h * LANES:(h + 1) * LANES]
        r = lax.rsqrt(jnp.mean(yh * yh, axis=-1, keepdims=True) + RMS_EPS)
        outs.append(yh * r * gain[:, h * LANES:(h + 1) * LANES] * scale)
    return jnp.concatenate(outs, axis=1)


def _proj_qnorm_kernel(x_ref, w_ref, g_ref, o_ref, *, scale):
    y = jnp.dot(x_ref[...], w_ref[...], preferred_element_type=F32)
    o_ref[...] = _head_rmsnorm(y, g_ref[...], scale).astype(o_ref.dtype)


def _proj_knorm_kernel(x_ref, w_ref, g_ref, o_ref, km_ref, *, nb_seq):
    y = jnp.dot(x_ref[...], w_ref[...], preferred_element_type=F32)
    yn = _head_rmsnorm(y, g_ref[...], 1.0)
    tm = yn.shape[0]
    row = lax.broadcasted_iota(jnp.int32, (tm, LANES), 0) + pl.program_id(0) * tm
    lane = lax.broadcasted_iota(jnp.int32, (tm, LANES), 1)
    onehot = jnp.where((lane == lax.rem(row // MOBA_BLOCK, nb_seq)) | (lane == LANES - 1), 1.0, 0.0)
    onehot = onehot.astype(o_ref.dtype)
    yb = yn.astype(o_ref.dtype)
    parts = []
    for h in range(yn.shape[1] // LANES):
        parts += [yb[:, h * LANES:(h + 1) * LANES], onehot]
    o_ref[...] = jnp.concatenate(parts, axis=1)
    for r in range(tm // MOBA_BLOCK):
        blk = yn[r * MOBA_BLOCK:(r + 1) * MOBA_BLOCK]
        km_ref[r] = jnp.mean(blk, axis=0, keepdims=True)


def _proj_moba_q(x, w, gain, tm, tn):
    m, k = x.shape
    n = w.shape[1]
    g = jnp.tile(gain.reshape(1, MOBA_DH), (1, n // MOBA_DH))
    return pl.pallas_call(
        functools.partial(_proj_qnorm_kernel, scale=MOBA_DH ** -0.5 * LOG2E),
        grid=(m // tm, n // tn),
        in_specs=[pl.BlockSpec((tm, k), lambda i, j: (i, 0)), pl.BlockSpec((k, tn), lambda i, j: (0, j)),
                  pl.BlockSpec((1, tn), lambda i, j: (0, j))],
        out_specs=pl.BlockSpec((tm, tn), lambda i, j: (i, j)),
        out_shape=jax.ShapeDtypeStruct((m, n), BF16),
        compiler_params=_params(("parallel", "arbitrary")),
        name="proj_moba_q",
    )(x, w, g)


def _proj_moba_k(x, w, gain, tm, tn, nb_seq):
    m, k = x.shape
    n = w.shape[1]
    g = jnp.tile(gain.reshape(1, MOBA_DH), (1, n // MOBA_DH))
    nb = tm // MOBA_BLOCK
    return pl.pallas_call(
        functools.partial(_proj_knorm_kernel, nb_seq=nb_seq),
        grid=(m // tm, n // tn),
        in_specs=[pl.BlockSpec((tm, k), lambda i, j: (i, 0)), pl.BlockSpec((k, tn), lambda i, j: (0, j)),
                  pl.BlockSpec((1, tn), lambda i, j: (0, j))],
        out_specs=[pl.BlockSpec((tm, 2 * tn), lambda i, j: (i, j)),
                   pl.BlockSpec((nb, 1, tn), lambda i, j: (i, 0, j))],
        out_shape=[jax.ShapeDtypeStruct((m, 2 * n), BF16),
                   jax.ShapeDtypeStruct((m // MOBA_BLOCK, 1, n), F32)],
        compiler_params=_params(("parallel", "arbitrary")),
        name="proj_moba_k",
    )(x, w, g)


def _gdn_all_kernel(x_ref, cw_ref, alog_ref, dtb_ref, gn_ref, o_ref, s_ref, tail_ref, qkv_s, xp_s, *, tb):
    t = pl.program_id(0)
    C = GDN_CHUNK
    nc = tb // C
    nh = GDN_HEADS
    nbat = x_ref.shape[0]

    @pl.when(t == 0)
    def _():
        s_ref[...] = jnp.zeros_like(s_ref)
        tail_ref[...] = jnp.zeros_like(tail_ref)

    def conv_group(gi, l2_scale):
        off = pl.multiple_of(gi * LANES, LANES)
        w = cw_ref[:, pl.ds(off, LANES)]
        for bi in range(nbat):
            x = x_ref[bi, :, pl.ds(off, LANES)]
            tail_rows = slice(bi * SUBLANES, (bi + 1) * SUBLANES)
            xp_s[bi, 0:SUBLANES, :] = tail_ref[tail_rows, pl.ds(off, LANES)]
            xp_s[bi, SUBLANES:SUBLANES + tb, :] = x
            y = x * w[GDN_CONV - 1:GDN_CONV]
            for j in range(GDN_CONV - 1):
                lag = GDN_CONV - 1 - j
                y = y + xp_s[bi, SUBLANES - lag:SUBLANES - lag + tb, :] * w[j:j + 1]
            tail_ref[tail_rows, pl.ds(off, LANES)] = x[tb - SUBLANES:tb]
            y = y * _sigmoid(y)
            if l2_scale is not None:
                y = y * (lax.rsqrt(jnp.sum(y * y, axis=-1, keepdims=True) + GDN_L2_EPS) * l2_scale)
            qkv_s[bi * tb:(bi + 1) * tb, pl.ds(off, LANES)] = y

    for g0, l2_scale in ((0, GDN_DK ** -0.5), (nh, 1.0), (2 * nh, None)):
        lax.fori_loop(g0, g0 + nh, lambda gi, carry, sc=l2_scale: conv_group(gi, sc), None)

    row_in_chunk = lax.broadcasted_iota(jnp.int32, (tb, LANES), 0) & (C - 1)
    beta_all, gc, egc, kdsc, egl, gc_t = [], [], [], [], [], []
    for bi in range(nbat):
        ba = x_ref[bi, :, 4 * GDN_W:4 * GDN_W + LANES]
        beta_all.append(_sigmoid(ba))
        sp_in = ba + dtb_ref[...]
        softplus = jnp.maximum(sp_in, 0.0) + jnp.log1p(jnp.exp(-jnp.abs(sp_in)))
        g = -(jnp.exp(alog_ref[...]) * softplus)
        sh = 1
        while sh < C:
            g = g + jnp.where(row_in_chunk >= sh, pltpu.roll(g, sh, axis=0), 0.0)
            sh *= 2
        gl_rows = [g[(c + 1) * C - 1:(c + 1) * C] for c in range(nc)]
        gl_b = jnp.concatenate([jnp.broadcast_to(r, (C, LANES)) for r in gl_rows], axis=0)
        gc.append(g)
        egc.append(jnp.exp(g))
        kdsc.append(jnp.exp(gl_b - g))
        egl.append([jnp.exp(r) for r in gl_rows])
        gc_t.append(g.T)

    ri = lax.broadcasted_iota(jnp.int32, (C, 2 * C), 0)
    lane = lax.broadcasted_iota(jnp.int32, (C, 2 * C), 1)
    ci = lane & (C - 1)
    left = lane < C
    tril = ri >= ci
    strict = ri > ci
    eye_f = (ri == ci).astype(F32)
    lvl_masks = []
    s = 1
    while s < C:
        sh2 = s.bit_length()
        lvl_masks.append((ri >> sh2 == ci >> sh2) & ((ri & (2 * s - 1)) >= s) & ((ci & (2 * s - 1)) < s))
        s *= 2
    gn = gn_ref[...]

    def block_diag(m_l, m_r):
        z = jnp.zeros_like(m_l)
        return jnp.concatenate([jnp.concatenate([m_l, z], axis=1), jnp.concatenate([z, m_r], axis=1)], axis=0)

    def unpack_diag(m):
        return jnp.concatenate([jnp.where(left, m, 0.0), jnp.where(left, 0.0, m)], axis=0)

    seqs = [(bi, h) for bi in range(nbat) for h in range(nh)]
    ids = range(len(seqs))
    assert len(seqs) % 2 == 0
    pairs = [(2 * p, 2 * p + 1) for p in range(len(seqs) // 2)]
    pids = range(len(pairs))
    states = [s_ref[i] for i in ids]
    grp = lambda g, h: slice((g * nh + h) * LANES, (g * nh + h + 1) * LANES)
    def operands(c):
        rows = slice(c * C, (c + 1) * C)
        srow = lambda bi: slice(bi * tb + c * C, bi * tb + (c + 1) * C)
        col = lambda arr, bi, h: arr[bi][rows, nh + h:nh + h + 1]
        d = dict(rows=rows,
                 qc=[qkv_s[srow(bi), grp(0, h)] for bi, h in seqs],
                 kc=[qkv_s[srow(bi), grp(1, h)] for bi, h in seqs],
                 vc=[qkv_s[srow(bi), grp(2, h)] for bi, h in seqs],
                 bcol=[beta_all[bi][rows, h:h + 1] for bi, h in seqs],
                 ecol=[col(egc, bi, h) for bi, h in seqs],
                 gcol=[col(gc, bi, h) for bi, h in seqs],
                 kdcol=[col(kdsc, bi, h) for bi, h in seqs],
                 grow=[gc_t[bi][nh + h:nh + h + 1, rows] for bi, h in seqs])
        d["k_beta"] = [d["kc"][i] * d["bcol"][i] for i in ids]
        return d

    def stateless(chunks):
        ops = {c: operands(c) for c in chunks}
        items = [(c, p) for c in chunks for p in pids]
        nit = range(len(items))
        decay, st = [], []
        for c, p in items:
            d, (l, r) = ops[c], pairs[p]
            decay.append(jnp.exp(jnp.where(
                tril, jnp.where(left, d["gcol"][l], d["gcol"][r])
                - jnp.concatenate([d["grow"][l], d["grow"][r]], axis=1), -jnp.inf)))
        for c, p in items:
            d, (l, r) = ops[c], pairs[p]
            st.append(_dot_t(jnp.concatenate([jnp.concatenate([d["k_beta"][l], d["k_beta"][r]], axis=1),
                                              jnp.concatenate([d["qc"][l], d["qc"][r]], axis=1)],
                                             axis=0).astype(BF16),
                             block_diag(d["kc"][l], d["kc"][r]).astype(BF16)))
        a_mat = [jnp.where(strict, st[n][:C] * decay[n], 0.0) for n in nit]
        qk = [jnp.where(tril, st[n][C:] * decay[n], 0.0) for n in nit]
        t_inv = [eye_f - jnp.where(lvl_masks[0], a_mat[n], 0.0) for n in nit]
        for msk in lvl_masks[1:]:
            x_mid = [_bdot(jnp.where(msk, a_mat[n], 0.0), unpack_diag(t_inv[n])) for n in nit]
            y_mid = [_bdot(t_inv[n], unpack_diag(x_mid[n])) for n in nit]
            t_inv = [t_inv[n] - y_mid[n] for n in nit]
        uw_p = []
        for n, (c, p) in enumerate(items):
            d, (l, r) = ops[c], pairs[p]
            rhs = [jnp.concatenate([d["vc"][i] * d["bcol"][i], d["k_beta"][i] * d["ecol"][i]], axis=1)
                   for i in (l, r)]
            uw_p.append(_bdot(t_inv[n], block_diag(rhs[0], rhs[1])))
        out = {}
        for k, c in enumerate(chunks):
            base = k * len(pairs)
            uw = [uw_p[base + i // 2][:, (i % 2) * 2 * LANES:(i % 2 + 1) * 2 * LANES] for i in ids]
            out[c] = (ops[c], uw, qk[base:base + len(pairs)])
        return out

    def recurrent(c, d, uw, qk, states):
        wq = [_bdot(jnp.concatenate([uw[i][:, LANES:], d["qc"][i] * d["ecol"][i]], axis=0), states[i])
              for i in ids]
        v_new = [uw[i][:, :LANES] - wq[i][:C] for i in ids]
        o_p = [_bdot(qk[p], block_diag(v_new[l], v_new[r])) for p, (l, r) in enumerate(pairs)]
        o = [wq[i][C:] + o_p[i // 2][:, (i % 2) * LANES:(i % 2 + 1) * LANES] for i in ids]
        new_states = [states[i] * egl[bi][c][:, nh + h:nh + h + 1] + lax.dot_general(
            (d["kc"][i] * d["kdcol"][i]).astype(BF16), v_new[i].astype(BF16), (((0,), (0,)), ((), ())),
            preferred_element_type=F32) for i, (bi, h) in enumerate(seqs)]
        for i, (bi, h) in enumerate(seqs):
            on = o[i] * lax.rsqrt(jnp.mean(o[i] * o[i], axis=-1, keepdims=True) + RMS_EPS) * gn
            zc = x_ref[bi, d["rows"], grp(3, h)]
            o_ref[bi, d["rows"], grp(0, h)] = (on * (zc * _sigmoid(zc))).astype(o_ref.dtype)
        return new_states

    for c0 in range(0, nc, GDN_CHUNKS_WIDE):
        chunks = list(range(c0, min(c0 + GDN_CHUNKS_WIDE, nc)))
        ready = stateless(chunks)
        for c in chunks:
            states = recurrent(c, *ready[c], states)
    for i in ids:
        s_ref[i] = states[i]


def _gdn_all(proj, conv_wt, a_log, dt_bias, gnorm, tb):
    b, t, width = proj.shape
    nh = GDN_HEADS
    lane_vec = lambda v: jnp.pad(v.reshape(1, nh), ((0, 0), (nh, LANES - 2 * nh)))
    full = lambda shape: pl.BlockSpec(shape, lambda ti: (0,) * len(shape))
    return pl.pallas_call(
        functools.partial(_gdn_all_kernel, tb=tb),
        grid=(t // tb,),
        in_specs=[pl.BlockSpec((b, tb, width), lambda ti: (0, ti, 0)),
                  full((GDN_CONV, 3 * GDN_W)), full((1, LANES)), full((1, LANES)), full((1, GDN_DV))],
        out_specs=pl.BlockSpec((b, tb, GDN_W), lambda ti: (0, ti, 0)),
        out_shape=jax.ShapeDtypeStruct((b, t, GDN_W), BF16),
        scratch_shapes=[pltpu.VMEM((b * nh, GDN_DK, GDN_DV), F32),
                        pltpu.VMEM((b * SUBLANES, 3 * GDN_W), F32),
                        pltpu.VMEM((b * tb, 3 * GDN_W), F32),
                        pltpu.VMEM((b, tb + SUBLANES, LANES), F32)],
        compiler_params=_params(("arbitrary",)),
        name="gdn",
    )(proj, conv_wt, lane_vec(a_log), lane_vec(dt_bias), gnorm.reshape(1, GDN_DV))


MOBA_QW = 2 * MOBA_BLOCK
MOBA_ROWS_L = BF16_ROWS
MOBA_BOUND_SLACK = 1.02
MOBA_SAFE_LOG2_RANGE = 100.0


def _moba_select_kernel(qt_ref, km_ref, kn_ref, qa_ref, bm_ref, *, nb, tq):
    nbp = -(-nb // BF16_ROWS) * BF16_ROWS
    qt = qt_ref[0]
    km = km_ref[0].astype(BF16)
    if nbp > nb:
        km = jnp.concatenate([km, jnp.zeros((nbp - nb, LANES), BF16)], axis=0)
    gate = jnp.dot(km, qt, preferred_element_type=F32)
    row = lax.broadcasted_iota(jnp.int32, (nbp, tq), 0)
    qblk = (lax.broadcasted_iota(jnp.int32, (nbp, tq), 1) + pl.program_id(2) * tq) // MOBA_BLOCK
    past = row < qblk
    g = jnp.where(past, gate, -jnp.inf)
    sel = row < 0
    for _ in range(MOBA_TOPK):
        m = jnp.max(g, axis=0, keepdims=True)
        idx = jnp.min(jnp.where(g == m, row, nbp), axis=0, keepdims=True)
        hit = row == idx
        sel = sel | (hit & past)
        g = jnp.where(hit, -jnp.inf, g)
    q32 = qt.astype(F32)
    bound = jnp.sqrt(jnp.sum(q32 * q32, axis=0, keepdims=True)) * kn_ref[:, 0:1] * MOBA_BOUND_SLACK
    qa_ref[0, 0:LANES, :] = qt
    qa_ref[0, LANES:LANES + nbp, :] = jnp.where(sel, 0.0, MASK_NEG).astype(BF16)
    tail_row = lax.broadcasted_iota(jnp.int32, (LANES - nbp, tq), 0)
    qa_ref[0, LANES + nbp:, :] = jnp.where(tail_row == LANES - nbp - 1, -bound, 0.0).astype(BF16)
    stored = bound.astype(BF16).astype(F32)
    bm_ref[0, 0] = jnp.broadcast_to(jnp.max(stored, axis=1, keepdims=True), (1, LANES))


def _moba_select(qt, kmean, k_norm_bound, tq):
    b, _, t = qt.shape
    nb = t // MOBA_BLOCK
    assert -(-nb // BF16_ROWS) * BF16_ROWS < LANES
    qaug, bmax = pl.pallas_call(
        functools.partial(_moba_select_kernel, nb=nb, tq=tq),
        grid=(b, MOBA_HEADS, t // tq),
        in_specs=[pl.BlockSpec((1, LANES, tq), lambda bi, hi, qi: (bi, hi, qi)),
                  pl.BlockSpec((1, nb, LANES), lambda bi, hi, qi: (bi, 0, hi)),
                  pl.BlockSpec((1, LANES), lambda bi, hi, qi: (0, 0))],
        out_specs=[pl.BlockSpec((1, 2 * LANES, tq), lambda bi, hi, qi: (bi, hi, qi)),
                   pl.BlockSpec((1, 1, 1, LANES), lambda bi, hi, qi: (bi, hi * (t // tq) + qi, 0, 0))],
        out_shape=[jax.ShapeDtypeStruct((b, 2 * MOBA_W, t), BF16),
                   jax.ShapeDtypeStruct((b, MOBA_HEADS * (t // tq), 1, LANES), F32)],
        compiler_params=_params(("parallel", "parallel", "parallel")),
        name="moba_select",
    )(qt, kmean, k_norm_bound)
    return qaug, jnp.max(bmax)


def _moba_kernel(qa_ref, ka_ref, vt_ref, o_ref, acc_ref, m_ref, sa_ref, sb_ref, xa_ref, xb_ref, *, nb, group):
    i2 = pl.program_id(2)
    BLK = MOBA_BLOCK

    def pv(p, start, width):
        lhs = jnp.concatenate([vt_ref[0, :, pl.ds(start, width)], jnp.ones((MOBA_ROWS_L, width), BF16)], axis=0)
        return jnp.dot(lhs, p, preferred_element_type=F32)

    key_i = lax.broadcasted_iota(jnp.int32, (BLK, BLK), 0)
    qry_i = lax.broadcasted_iota(jnp.int32, (BLK, BLK), 1)
    starts = [pl.multiple_of((2 * i2 + hf) * BLK, BLK) for hf in range(2)]
    s_own = [jnp.dot(ka_ref[0, pl.ds(starts[hf], BLK), 0:LANES], qa_ref[0, 0:LANES, hf * BLK:(hf + 1) * BLK],
                     preferred_element_type=F32) for hf in range(2)]
    s_own = [s_own[hf] + qa_ref[0, 2 * LANES - 1:2 * LANES, hf * BLK:(hf + 1) * BLK].astype(F32) for hf in range(2)]
    s_own = [jnp.where(key_i <= qry_i, s, MASK_NEG) for s in s_own]
    m_own = [jnp.max(s, axis=0, keepdims=True) for s in s_own]
    p_own = [jnp.exp2(s_own[hf] - m_own[hf]).astype(BF16) for hf in range(2)]
    for hf in range(2):
        m_ref[:, hf * BLK:(hf + 1) * BLK] = m_own[hf]
        acc_ref[:, hf * BLK:(hf + 1) * BLK] = pv(p_own[hf], starts[hf], BLK)

    span = group * BLK
    n_pairs = (2 * i2 + 2 * group) // (2 * group)
    last = nb // group - 1
    halves = [slice(hf * BLK, (hf + 1) * BLK) for hf in range(2)]

    def qk(g_idx, dst_ref, mx_dst, hs):
        start = pl.multiple_of(g_idx * span, span)
        s = jnp.dot(ka_ref[0, pl.ds(start, span), :], qa_ref[0, :, hs], preferred_element_type=F32)
        dst_ref[:, hs] = s
        mx_dst[:, hs] = jnp.max(s, axis=0, keepdims=True)

    def softmax_pv(src_ref, mx_src, g_idx, hs):
        m_old = m_ref[:, hs]
        m_new = jnp.maximum(m_old, mx_src[:, hs])
        alpha = jnp.exp2(m_old - m_new)
        m_ref[:, hs] = m_new
        p = jnp.exp2(src_ref[:, hs] - m_new).astype(BF16)
        acc_ref[:, hs] = alpha * acc_ref[:, hs] + pv(p, pl.multiple_of(g_idx * span, span), span)

    for hs in halves:
        qk(0, sa_ref, xa_ref, hs)

    def body(jj, carry):
        for hs in halves:
            qk(2 * jj + 1, sb_ref, xb_ref, hs)
            softmax_pv(sa_ref, xa_ref, 2 * jj, hs)
        for hs in halves:
            qk(jnp.minimum(2 * jj + 2, last), sa_ref, xa_ref, hs)
            softmax_pv(sb_ref, xb_ref, 2 * jj + 1, hs)
        return carry

    lax.fori_loop(0, n_pairs, body, 0)
    acc = acc_ref[...]
    o = acc[0:LANES] * (1.0 / acc[LANES:LANES + 1])
    o_ref[0] = o.T.astype(o_ref.dtype)


def _moba_bounded_kernel(qa_ref, ka_ref, vt_ref, o_ref, acc_ref, pa_ref, pb_ref, *, nb, group):
    i2 = pl.program_id(2)
    BLK = MOBA_BLOCK

    def pv(p, start, width):
        lhs = jnp.concatenate([vt_ref[0, :, pl.ds(start, width)], jnp.ones((MOBA_ROWS_L, width), BF16)], axis=0)
        return jnp.dot(lhs, p, preferred_element_type=F32)

    halves = [slice(hf * BLK, (hf + 1) * BLK) for hf in range(2)]
    key_i = lax.broadcasted_iota(jnp.int32, (BLK, BLK), 0)
    qry_i = lax.broadcasted_iota(jnp.int32, (BLK, BLK), 1)
    own = []
    for hf, hs in enumerate(halves):
        start = pl.multiple_of((2 * i2 + hf) * BLK, BLK)
        s = jnp.dot(ka_ref[0, pl.ds(start, BLK), 0:LANES], qa_ref[0, 0:LANES, hs], preferred_element_type=F32)
        s = s + qa_ref[0, 2 * LANES - 1:2 * LANES, hs].astype(F32)
        own.append((jnp.exp2(jnp.where(key_i <= qry_i, s, MASK_NEG)).astype(BF16), start))

    span = group * BLK
    n_groups = (2 * i2 + group) // group
    n_pairs = n_groups // 2
    last = nb // group - 1

    def qk_exp(g_idx, p_dst, hs):
        start = pl.multiple_of(g_idx * span, span)
        s = jnp.dot(ka_ref[0, pl.ds(start, span), :], qa_ref[0, :, hs], preferred_element_type=F32)
        p_dst[:, hs] = jnp.exp2(s).astype(BF16)

    def pv_acc(p_src, g_idx, hs):
        acc_ref[:, hs] += pv(p_src[:, hs], pl.multiple_of(g_idx * span, span), span)

    for hs in halves:
        qk_exp(0, pa_ref, hs)
    for (p, start), hs in zip(own, halves):
        acc_ref[:, hs] = pv(p, start, BLK)

    def body(jj, carry):
        for hs in halves:
            qk_exp(2 * jj + 1, pb_ref, hs)
            pv_acc(pa_ref, 2 * jj, hs)
        for hs in halves:
            qk_exp(jnp.minimum(2 * jj + 2, last), pa_ref, hs)
            pv_acc(pb_ref, 2 * jj + 1, hs)
        return carry

    lax.fori_loop(0, n_pairs, body, 0)

    @pl.when(n_groups % 2 == 1)
    def _():
        for hs in halves:
            pv_acc(pa_ref, 2 * n_pairs, hs)

    acc = acc_ref[...]
    o = acc[0:LANES] * (1.0 / acc[LANES:LANES + 1])
    o_ref[0] = o.T.astype(o_ref.dtype)


def _moba(qaug, kaug, vt, max_bound):
    b, _, t = vt.shape
    nb = t // MOBA_BLOCK
    group = 4 if nb % 8 == 0 else 2
    assert nb % (2 * group) == 0
    span = group * MOBA_BLOCK

    def call(body, scratch, name):
        return pl.pallas_call(
            functools.partial(body, nb=nb, group=group),
            grid=(b, MOBA_HEADS, t // MOBA_QW),
            in_specs=[pl.BlockSpec((1, 2 * LANES, MOBA_QW), lambda bi, hi, qi: (bi, hi, qi)),
                      pl.BlockSpec((1, t, 2 * LANES), lambda bi, hi, qi: (bi, 0, hi)),
                      pl.BlockSpec((1, LANES, t), lambda bi, hi, qi: (bi, hi, 0))],
            out_specs=pl.BlockSpec((1, MOBA_QW, LANES), lambda bi, hi, qi: (bi, qi, hi)),
            out_shape=jax.ShapeDtypeStruct((b, t, MOBA_W), BF16),
            scratch_shapes=[pltpu.VMEM((LANES + MOBA_ROWS_L, MOBA_QW), F32)] + scratch,
            compiler_params=_params(("parallel", "parallel", "arbitrary")),
            name=name,
        )

    online = call(_moba_kernel,
                  [pltpu.VMEM((1, MOBA_QW), F32),
                   pltpu.VMEM((span, MOBA_QW), F32), pltpu.VMEM((span, MOBA_QW), F32),
                   pltpu.VMEM((1, MOBA_QW), F32), pltpu.VMEM((1, MOBA_QW), F32)], "moba")
    bounded = call(_moba_bounded_kernel,
                   [pltpu.VMEM((span, MOBA_QW), BF16), pltpu.VMEM((span, MOBA_QW), BF16)], "moba_bounded")
    return lax.cond(2.0 * max_bound < MOBA_SAFE_LOG2_RANGE, bounded, online, qaug, kaug, vt)


def _oproj_kernel(og_ref, om_ref, wa_ref, wb_ref, x_ref, nw_ref, h_ref, hn_ref):
    y = jnp.dot(og_ref[...], wa_ref[...], preferred_element_type=F32)
    y = y + jnp.dot(om_ref[...], wb_ref[...], preferred_element_type=F32)
    h = x_ref[...] + y
    h_ref[...] = h
    hn = h * lax.rsqrt(jnp.mean(h * h, axis=-1, keepdims=True) + RMS_EPS)
    hn_ref[...] = (hn * nw_ref[...]).astype(hn_ref.dtype)


def _oproj(og, om, wa, wb, x, norm_w, tm):
    m, d = x.shape
    ka, kb = og.shape[1], om.shape[1]
    row = lambda w: pl.BlockSpec((tm, w), lambda i: (i, 0))
    full = lambda r, c: pl.BlockSpec((r, c), lambda i: (0, 0))
    return pl.pallas_call(
        _oproj_kernel,
        grid=(m // tm,),
        in_specs=[row(ka), row(kb), full(ka, d), full(kb, d), row(d), full(1, d)],
        out_specs=[row(d), row(d)],
        out_shape=[jax.ShapeDtypeStruct((m, d), F32), jax.ShapeDtypeStruct((m, d), BF16)],
        compiler_params=_params(("parallel",)),
        name="oproj",
    )(og, om, wa, wb, x, norm_w.reshape(1, d))


def _ffn_kernel(hn_ref, wg_ref, wu_ref, wd_ref, h_ref, nw_ref, h2_ref, hn2_ref, acc_ref):
    f = pl.program_id(1)

    @pl.when(f == 0)
    def _():
        acc_ref[...] = jnp.zeros_like(acc_ref)

    hn = hn_ref[...]
    g = jnp.dot(hn, wg_ref[...], preferred_element_type=F32)
    u = jnp.dot(hn, wu_ref[...], preferred_element_type=F32)
    a = (g * _sigmoid(g) * u).astype(BF16)
    acc_ref[...] += jnp.dot(a, wd_ref[...], preferred_element_type=F32)

    @pl.when(f == pl.num_programs(1) - 1)
    def _():
        h2 = h_ref[...] + acc_ref[...]
        h2_ref[...] = h2
        n = h2 * lax.rsqrt(jnp.mean(h2 * h2, axis=-1, keepdims=True) + RMS_EPS)
        hn2_ref[...] = (n * nw_ref[...]).astype(hn2_ref.dtype)


def _ffn(hn, wg, wu, wd, h, norm_w, tm, tf):
    m, d = h.shape
    dff = wg.shape[1]
    row = pl.BlockSpec((tm, d), lambda i, f: (i, 0))
    return pl.pallas_call(
        _ffn_kernel,
        grid=(m // tm, dff // tf),
        in_specs=[row, pl.BlockSpec((d, tf), lambda i, f: (0, f)), pl.BlockSpec((d, tf), lambda i, f: (0, f)),
                  pl.BlockSpec((tf, d), lambda i, f: (f, 0)), row, pl.BlockSpec((1, d), lambda i, f: (0, 0))],
        out_specs=[row, row],
        out_shape=[jax.ShapeDtypeStruct((m, d), F32), jax.ShapeDtypeStruct((m, d), BF16)],
        scratch_shapes=[pltpu.VMEM((tm, d), F32)],
        compiler_params=_params(("parallel", "arbitrary")),
        name="ffn",
    )(hn, wg, wu, wd, h, norm_w.reshape(1, d))


def _ple_kernel(hn_ref, wg_ref, p_ref, wp_ref, h_ref, o_ref):
    gate = _sigmoid(jnp.dot(hn_ref[...], wg_ref[...], preferred_element_type=F32))
    proj = jnp.dot(p_ref[...].astype(BF16), wp_ref[...], preferred_element_type=F32)
    o_ref[...] = h_ref[...] + gate * proj


def _ple(hn, wg, p, wp, h, tm, tn):
    m, d = h.shape
    kp = p.shape[1]
    return pl.pallas_call(
        _ple_kernel,
        grid=(m // tm, d // tn),
        in_specs=[pl.BlockSpec((tm, d), lambda i, j: (i, 0)), pl.BlockSpec((d, tn), lambda i, j: (0, j)),
                  pl.BlockSpec((tm, kp), lambda i, j: (i, 0)), pl.BlockSpec((kp, tn), lambda i, j: (0, j)),
                  pl.BlockSpec((tm, tn), lambda i, j: (i, j))],
        out_specs=pl.BlockSpec((tm, tn), lambda i, j: (i, j)),
        out_shape=jax.ShapeDtypeStruct((m, d), F32),
        compiler_params=_params(("parallel", "arbitrary")),
        name="ple",
    )(hn, wg, p, wp, h)


class _Tiles(NamedTuple):
    rows: int
    cols: int
    norm_rows: int
    wide_rows: int
    ffn_cols: int
    gdn_time: int
    select_queries: int


def _tile_plan(m, t):
    return _Tiles(rows=min(1024, m), cols=1024, norm_rows=min(512, m), wide_rows=min(512, m), ffn_cols=512,
                  gdn_time=min(4 * GDN_CHUNK, t), select_queries=min(2048, t))


def _layer(h, p, attn_norm, w_in, conv_w, a_log, dt_bias, gdn_norm, q_norm, k_norm, w_o, ffn_norm,
           w_gate, w_up, w_down, ple_norm, w_ple_gate, w_ple_proj):
    b, t, d = h.shape
    m = b * t
    x2 = h.reshape(m, d)
    tiles = _tile_plan(m, t)

    o_ba = 4 * GDN_W
    o_mq = o_ba + 2 * GDN_HEADS
    w_in16 = w_in.astype(BF16)
    w_gdn = jnp.pad(w_in16[:, :o_mq], ((0, 0), (0, LANES - 2 * GDN_HEADS)))
    w_mq = w_in16[:, o_mq:o_mq + MOBA_W]
    w_mk = w_in16[:, o_mq + MOBA_W:o_mq + 2 * MOBA_W]
    w_mv = w_in16[:, o_mq + 2 * MOBA_W:o_mq + 3 * MOBA_W]

    xn = _rmsnorm(x2, attn_norm, tiles.norm_rows)
    gproj = _proj(xn, w_gdn, F32, tiles.rows, w_gdn.shape[1] // 3, "proj_gdn")
    mq = _proj_moba_q(xn, w_mq, q_norm, tiles.rows, tiles.cols)
    kaug, kmean = _proj_moba_k(xn, w_mk, k_norm, tiles.rows, tiles.cols, t // MOBA_BLOCK)
    mv = _proj(xn, w_mv, BF16, tiles.rows, tiles.cols, "proj_moba_v")

    o_gdn = _gdn_all(gproj.reshape(b, t, 4 * GDN_W + LANES), conv_w.T, a_log, dt_bias, gdn_norm, tiles.gdn_time)
    k_norm_bound = jnp.full((1, LANES), MOBA_DH ** 0.5, F32) * jnp.max(jnp.abs(k_norm))
    qaug, max_bound = _moba_select(mq.reshape(b, t, MOBA_W).swapaxes(1, 2),
                                   kmean.reshape(b, t // MOBA_BLOCK, MOBA_W), k_norm_bound, tiles.select_queries)
    o_moba = _moba(qaug, kaug.reshape(b, t, 2 * MOBA_W), mv.reshape(b, t, MOBA_W).swapaxes(1, 2), max_bound)

    w_o16 = w_o.astype(BF16)
    h1, hn = _oproj(o_gdn.reshape(m, GDN_W), o_moba.reshape(m, MOBA_W), w_o16[:GDN_W], w_o16[GDN_W:],
                    x2, ffn_norm, tiles.wide_rows)
    h2, hn2 = _ffn(hn, w_gate.astype(BF16), w_up.astype(BF16), w_down.astype(BF16), h1, ple_norm,
                   tiles.wide_rows, tiles.ffn_cols)
    h3 = _ple(hn2, w_ple_gate.astype(BF16), p.reshape(m, PLE_DIM), w_ple_proj.astype(BF16), h2,
              tiles.rows, tiles.cols)
    return h3.reshape(b, t, d)


def kernel(x, p, attn_norm, w_in, conv_w, A_log, dt_bias, gdn_norm, q_norm, k_norm, w_o, ffn_norm,
           w_gate, w_up, w_down, ple_norm, w_ple_gate, w_ple_proj):
    h = x
    for i in range(p.shape[0]):
        h = _layer(h, p[i], attn_norm[i], w_in[i], conv_w[i], A_log[i], dt_bias[i], gdn_norm[i],
                   q_norm[i], k_norm[i], w_o[i], ffn_norm[i], w_gate[i], w_up[i], w_down[i],
                   ple_norm[i], w_ple_gate[i], w_ple_proj[i])
    return h
```

```python
import functools
from typing import NamedTuple

import jax
import jax.numpy as jnp
from jax import lax
from jax.experimental import pallas as pl
from jax.experimental.pallas import tpu as pltpu

D_MODEL = 2048
PLE_DIM = 256
GDN_HEADS = 8
GDN_DK = 128
GDN_DV = 128
GDN_CONV = 4
GDN_CHUNK = 64
GDN_CHUNKS_WIDE = 2
MOBA_HEADS = 8
MOBA_DH = 128
MOBA_BLOCK = 256
MOBA_TOPK = 3
RMS_EPS = 1e-6
GDN_W = GDN_HEADS * GDN_DK
MOBA_W = MOBA_HEADS * MOBA_DH
LANES = 128
SUBLANES = 8
BF16_ROWS = 16
GDN_L2_EPS = 1e-6
LOG2E = 1.4426950408889634
MASK_NEG = -1e30

F32 = jnp.float32
BF16 = jnp.bfloat16

VMEM_LIMIT = 56 * 1024 * 1024


def _params(sem):
    return pltpu.CompilerParams(dimension_semantics=sem, vmem_limit_bytes=VMEM_LIMIT)


def _sigmoid(x):
    return 1.0 / (1.0 + jnp.exp(-x))


def _dot_t(a, b):
    return lax.dot_general(a, b, (((1,), (1,)), ((), ())), preferred_element_type=F32)


def _bdot(a, b):
    return jnp.dot(a.astype(BF16), b.astype(BF16), preferred_element_type=F32)


def _rmsnorm_kernel(x_ref, w_ref, o_ref):
    x = x_ref[...]
    y = x * lax.rsqrt(jnp.mean(x * x, axis=-1, keepdims=True) + RMS_EPS)
    o_ref[...] = (y * w_ref[...]).astype(o_ref.dtype)


def _rmsnorm(x, w, tm):
    m, d = x.shape
    return pl.pallas_call(
        _rmsnorm_kernel,
        grid=(m // tm,),
        in_specs=[pl.BlockSpec((tm, d), lambda i: (i, 0)), pl.BlockSpec((1, d), lambda i: (0, 0))],
        out_specs=pl.BlockSpec((tm, d), lambda i: (i, 0)),
        out_shape=jax.ShapeDtypeStruct((m, d), BF16),
        compiler_params=_params(("parallel",)),
        name="rmsnorm_cast",
    )(x, w.reshape(1, d))


def _proj_kernel(x_ref, w_ref, o_ref):
    o_ref[...] = jnp.dot(x_ref[...], w_ref[...], preferred_element_type=F32).astype(o_ref.dtype)


def _proj(x, w, out_dtype, tm, tn, name):
    m, k = x.shape
    n = w.shape[1]
    return pl.pallas_call(
        _proj_kernel,
        grid=(m // tm, n // tn),
        in_specs=[pl.BlockSpec((tm, k), lambda i, j: (i, 0)), pl.BlockSpec((k, tn), lambda i, j: (0, j))],
        out_specs=pl.BlockSpec((tm, tn), lambda i, j: (i, j)),
        out_shape=jax.ShapeDtypeStruct((m, n), out_dtype),
        compiler_params=_params(("parallel", "arbitrary")),
        name=name,
    )(x, w)


def _norm_proj_kernel(x_ref, nw_ref, w_ref, o_ref, xn_ref, *, rows_per_pass):
    @pl.when(pl.program_id(1) == 0)
    def _():
        def norm_rows(r, carry):
            rows = pl.ds(pl.multiple_of(r * rows_per_pass, rows_per_pass), rows_per_pass)
            x = x_ref[rows, :]
            y = x * lax.rsqrt(jnp.mean(x * x, axis=-1, keepdims=True) + RMS_EPS)
            xn_ref[rows, :] = (y * nw_ref[...]).astype(xn_ref.dtype)
            return carry
        lax.fori_loop(0, x_ref.shape[0] // rows_per_pass, norm_rows, 0)

    o_ref[...] = jnp.dot(xn_ref[...], w_ref[...], preferred_element_type=F32).astype(o_ref.dtype)


def _norm_proj(x, norm_w, w, out_dtype, tm, tn, name):
    m, k = x.shape
    n = w.shape[1]
    return pl.pallas_call(
        functools.partial(_norm_proj_kernel, rows_per_pass=min(128, tm)),
        grid=(m // tm, n // tn),
        in_specs=[pl.BlockSpec((tm, k), lambda i, j: (i, 0)), pl.BlockSpec((1, k), lambda i, j: (0, 0)),
                  pl.BlockSpec((k, tn), lambda i, j: (0, j))],
        out_specs=[pl.BlockSpec((tm, tn), lambda i, j: (i, j)), pl.BlockSpec((tm, k), lambda i, j: (i, 0))],
        out_shape=[jax.ShapeDtypeStruct((m, n), out_dtype), jax.ShapeDtypeStruct((m, k), BF16)],
        compiler_params=_params(("parallel", "arbitrary")),
        name=name,
    )(x, norm_w.reshape(1, k), w)


def _head_rmsnorm(y, gain, scale):
    outs = []
    for h in range(y.shape[1] // LANES):
        yh = y[:, h * LANES:(h + 1) * LANES]
        r = lax.rsqrt(jnp.mean(yh * yh, axis=-1, keepdims=True) + RMS_EPS)
        outs.append(yh * r * gain[:, h * LANES:(h + 1) * LANES] * scale)
    return jnp.concatenate(outs, axis=1)


def _proj_qnorm_kernel(x_ref, w_ref, g_ref, o_ref, *, scale):
    y = jnp.dot(x_ref[...], w_ref[...], preferred_element_type=F32)
    o_ref[...] = _head_rmsnorm(y, g_ref[...], scale).astype(o_ref.dtype)


def _proj_knorm_kernel(x_ref, w_ref, g_ref, o_ref, km_ref, *, nb_seq):
    y = jnp.dot(x_ref[...], w_ref[...], preferred_element_type=F32)
    yn = _head_rmsnorm(y, g_ref[...], 1.0)
    tm = yn.shape[0]
    row = lax.broadcasted_iota(jnp.int32, (tm, LANES), 0) + pl.program_id(0) * tm
    lane = lax.broadcasted_iota(jnp.int32, (tm, LANES), 1)
    onehot = jnp.where((lane == lax.rem(row // MOBA_BLOCK, nb_seq)) | (lane == LANES - 1), 1.0, 0.0)
    onehot = onehot.astype(o_ref.dtype)
    yb = yn.astype(o_ref.dtype)
    parts = []
    for h in range(yn.shape[1] // LANES):
        parts += [yb[:, h * LANES:(h + 1) * LANES], onehot]
    o_ref[...] = jnp.concatenate(parts, axis=1)
    for r in range(tm // MOBA_BLOCK):
        blk = yn[r * MOBA_BLOCK:(r + 1) * MOBA_BLOCK]
        km_ref[r] = jnp.mean(blk, axis=0, keepdims=True)


def _proj_moba_q(x, w, gain, tm, tn):
    m, k = x.shape
    n = w.shape[1]
    g = jnp.tile(gain.reshape(1, MOBA_DH), (1, n // MOBA_DH))
    return pl.pallas_call(
        functools.partial(_proj_qnorm_kernel, scale=MOBA_DH ** -0.5 * LOG2E),
        grid=(m // tm, n // tn),
        in_specs=[pl.BlockSpec((tm, k), lambda i, j: (i, 0)), pl.BlockSpec((k, tn), lambda i, j: (0, j)),
                  pl.BlockSpec((1, tn), lambda i, j: (0, j))],
        out_specs=pl.BlockSpec((tm, tn), lambda i, j: (i, j)),
        out_shape=jax.ShapeDtypeStruct((m, n), BF16),
        compiler_params=_params(("parallel", "arbitrary")),
        name="proj_moba_q",
    )(x, w, g)


def _proj_moba_k(x, w, gain, tm, tn, nb_seq):
    m, k = x.shape
    n = w.shape[1]
    g = jnp.tile(gain.reshape(1, MOBA_DH), (1, n // MOBA_DH))
    nb = tm // MOBA_BLOCK
    return pl.pallas_call(
        functools.partial(_proj_knorm_kernel, nb_seq=nb_seq),
        grid=(m // tm, n // tn),
        in_specs=[pl.BlockSpec((tm, k), lambda i, j: (i, 0)), pl.BlockSpec((k, tn), lambda i, j: (0, j)),
                  pl.BlockSpec((1, tn), lambda i, j: (0, j))],
        out_specs=[pl.BlockSpec((tm, 2 * tn), lambda i, j: (i, j)),
                   pl.BlockSpec((nb, 1, tn), lambda i, j: (i, 0, j))],
        out_shape=[jax.ShapeDtypeStruct((m, 2 * n), BF16),
                   jax.ShapeDtypeStruct((m // MOBA_BLOCK, 1, n), F32)],
        compiler_params=_params(("parallel", "arbitrary")),
        name="proj_moba_k",
    )(x, w, g)


def _gdn_all_kernel(x_ref, cw_ref, alog_ref, dtb_ref, gn_ref, o_ref, s_ref, tail_ref, qkv_s, xp_s, *, tb):
    t = pl.program_id(0)
    C = GDN_CHUNK
    nc = tb // C
    nh = GDN_HEADS
    nbat = x_ref.shape[0]

    @pl.when(t == 0)
    def _():
        s_ref[...] = jnp.zeros_like(s_ref)
        tail_ref[...] = jnp.zeros_like(tail_ref)

    def conv_group(gi, l2_scale):
        off = pl.multiple_of(gi * LANES, LANES)
        w = cw_ref[:, pl.ds(off, LANES)]
        for bi in range(nbat):
            x = x_ref[bi, :, pl.ds(off, LANES)]
            tail_rows = slice(bi * SUBLANES, (bi + 1) * SUBLANES)
            xp_s[bi, 0:SUBLANES, :] = tail_ref[tail_rows, pl.ds(off, LANES)]
            xp_s[bi, SUBLANES:SUBLANES + tb, :] = x
            y = x * w[GDN_CONV - 1:GDN_CONV]
            for j in range(GDN_CONV - 1):
                lag = GDN_CONV - 1 - j
                y = y + xp_s[bi, SUBLANES - lag:SUBLANES - lag + tb, :] * w[j:j + 1]
            tail_ref[tail_rows, pl.ds(off, LANES)] = x[tb - SUBLANES:tb]
            y = y * _sigmoid(y)
            if l2_scale is not None:
                y = y * (lax.rsqrt(jnp.sum(y * y, axis=-1, keepdims=True) + GDN_L2_EPS) * l2_scale)
            qkv_s[bi * tb:(bi + 1) * tb, pl.ds(off, LANES)] = y

    for g0, l2_scale in ((0, GDN_DK ** -0.5), (nh, 1.0), (2 * nh, None)):
        lax.fori_loop(g0, g0 + nh, lambda gi, carry, sc=l2_scale: conv_group(gi, sc), None)

    row_in_chunk = lax.broadcasted_iota(jnp.int32, (tb, LANES), 0) & (C - 1)
    beta_all, gc, egc, kdsc, egl, gc_t = [], [], [], [], [], []
    for bi in range(nbat):
        ba = x_ref[bi, :, 4 * GDN_W:4 * GDN_W + LANES]
        beta_all.append(_sigmoid(ba))
        sp_in = ba + dtb_ref[...]
        softplus = jnp.maximum(sp_in, 0.0) + jnp.log1p(jnp.exp(-jnp.abs(sp_in)))
        g = -(jnp.exp(alog_ref[...]) * softplus)
        sh = 1
        while sh < C:
            g = g + jnp.where(row_in_chunk >= sh, pltpu.roll(g, sh, axis=0), 0.0)
            sh *= 2
        gl_rows = [g[(c + 1) * C - 1:(c + 1) * C] for c in range(nc)]
        gl_b = jnp.concatenate([jnp.broadcast_to(r, (C, LANES)) for r in gl_rows], axis=0)
        gc.append(g)
        egc.append(jnp.exp(g))
        kdsc.append(jnp.exp(gl_b - g))
        egl.append([jnp.exp(r) for r in gl_rows])
        gc_t.append(g.T)

    ri = lax.broadcasted_iota(jnp.int32, (C, 2 * C), 0)
    lane = lax.broadcasted_iota(jnp.int32, (C, 2 * C), 1)
    ci = lane & (C - 1)
    left = lane < C
    tril = ri >= ci
    strict = ri > ci
    eye_f = (ri == ci).astype(F32)
    lvl_masks = []
    s = 1
    while s < C:
        sh2 = s.bit_length()
        lvl_masks.append((ri >> sh2 == ci >> sh2) & ((ri & (2 * s - 1)) >= s) & ((ci & (2 * s - 1)) < s))
        s *= 2
    gn = gn_ref[...]

    def block_diag(m_l, m_r):
        z = jnp.zeros_like(m_l)
        return jnp.concatenate([jnp.concatenate([m_l, z], axis=1), jnp.concatenate([z, m_r], axis=1)], axis=0)

    def unpack_diag(m):
        return jnp.concatenate([jnp.where(left, m, 0.0), jnp.where(left, 0.0, m)], axis=0)

    seqs = [(bi, h) for bi in range(nbat) for h in range(nh)]
    ids = range(len(seqs))
    assert len(seqs) % 2 == 0
    pairs = [(2 * p, 2 * p + 1) for p in range(len(seqs) // 2)]
    pids = range(len(pairs))
    states = [s_ref[i] for i in ids]
    grp = lambda g, h: slice((g * nh + h) * LANES, (g * nh + h + 1) * LANES)
    def operands(c):
        rows = slice(c * C, (c + 1) * C)
        srow = lambda bi: slice(bi * tb + c * C, bi * tb + (c + 1) * C)
        col = lambda arr, bi, h: arr[bi][rows, nh + h:nh + h + 1]
        d = dict(rows=rows,
                 qc=[qkv_s[srow(bi), grp(0, h)] for bi, h in seqs],
                 kc=[qkv_s[srow(bi), grp(1, h)] for bi, h in seqs],
                 vc=[qkv_s[srow(bi), grp(2, h)] for bi, h in seqs],
                 bcol=[beta_all[bi][rows, h:h + 1] for bi, h in seqs],
                 ecol=[col(egc, bi, h) for bi, h in seqs],
                 gcol=[col(gc, bi, h) for bi, h in seqs],
                 kdcol=[col(kdsc, bi, h) for bi, h in seqs],
                 grow=[gc_t[bi][nh + h:nh + h + 1, rows] for bi, h in seqs])
        d["k_beta"] = [d["kc"][i] * d["bcol"][i] for i in ids]
        return d

    def stateless(chunks):
        ops = {c: operands(c) for c in chunks}
        items = [(c, p) for c in chunks for p in pids]
        nit = range(len(items))
        decay, st = [], []
        for c, p in items:
            d, (l, r) = ops[c], pairs[p]
            decay.append(jnp.exp(jnp.where(
                tril, jnp.where(left, d["gcol"][l], d["gcol"][r])
                - jnp.concatenate([d["grow"][l], d["grow"][r]], axis=1), -jnp.inf)))
        for c, p in items:
            d, (l, r) = ops[c], pairs[p]
            st.append(_dot_t(jnp.concatenate([jnp.concatenate([d["k_beta"][l], d["k_beta"][r]], axis=1),
                                              jnp.concatenate([d["qc"][l], d["qc"][r]], axis=1)],
                                             axis=0).astype(BF16),
                             block_diag(d["kc"][l], d["kc"][r]).astype(BF16)))
        a_mat = [jnp.where(strict, st[n][:C] * decay[n], 0.0) for n in nit]
        qk = [jnp.where(tril, st[n][C:] * decay[n], 0.0) for n in nit]
        t_inv = [eye_f - jnp.where(lvl_masks[0], a_mat[n], 0.0) for n in nit]
        for msk in lvl_masks[1:]:
            x_mid = [_bdot(jnp.where(msk, a_mat[n], 0.0), unpack_diag(t_inv[n])) for n in nit]
            y_mid = [_bdot(t_inv[n], unpack_diag(x_mid[n])) for n in nit]
            t_inv = [t_inv[n] - y_mid[n] for n in nit]
        uw_p = []
        for n, (c, p) in enumerate(items):
            d, (l, r) = ops[c], pairs[p]
            rhs = [jnp.concatenate([d["vc"][i] * d["bcol"][i], d["k_beta"][i] * d["ecol"][i]], axis=1)
                   for i in (l, r)]
            uw_p.append(_bdot(t_inv[n], block_diag(rhs[0], rhs[1])))
        out = {}
        for k, c in enumerate(chunks):
            base = k * len(pairs)
            uw = [uw_p[base + i // 2][:, (i % 2) * 2 * LANES:(i % 2 + 1) * 2 * LANES] for i in ids]
            out[c] = (ops[c], uw, qk[base:base + len(pairs)])
        return out

    def recurrent(c, d, uw, qk, states):
        wq = [_bdot(jnp.concatenate([uw[i][:, LANES:], d["qc"][i] * d["ecol"][i]], axis=0), states[i])
              for i in ids]
        v_new = [uw[i][:, :LANES] - wq[i][:C] for i in ids]
        o_p = [_bdot(qk[p], block_diag(v_new[l], v_new[r])) for p, (l, r) in enumerate(pairs)]
        o = [wq[i][C:] + o_p[i // 2][:, (i % 2) * LANES:(i % 2 + 1) * LANES] for i in ids]
        new_states = [states[i] * egl[bi][c][:, nh + h:nh + h + 1] + lax.dot_general(
            (d["kc"][i] * d["kdcol"][i]).astype(BF16), v_new[i].astype(BF16), (((0,), (0,)), ((), ())),
            preferred_element_type=F32) for i, (bi, h) in enumerate(seqs)]
        for i, (bi, h) in enumerate(seqs):
            on = o[i] * lax.rsqrt(jnp.mean(o[i] * o[i], axis=-1, keepdims=True) + RMS_EPS) * gn
            zc = x_ref[bi, d["rows"], grp(3, h)]
            o_ref[bi, d["rows"], grp(0, h)] = (on * (zc * _sigmoid(zc))).astype(o_ref.dtype)
        return new_states

    for c0 in range(0, nc, GDN_CHUNKS_WIDE):
        chunks = list(range(c0, min(c0 + GDN_CHUNKS_WIDE, nc)))
        ready = stateless(chunks)
        for c in chunks:
            states = recurrent(c, *ready[c], states)
    for i in ids:
        s_ref[i] = states[i]


def _gdn_all(proj, conv_wt, a_log, dt_bias, gnorm, tb):
    b, t, width = proj.shape
    nh = GDN_HEADS
    lane_vec = lambda v: jnp.pad(v.reshape(1, nh), ((0, 0), (nh, LANES - 2 * nh)))
    full = lambda shape: pl.BlockSpec(shape, lambda ti: (0,) * len(shape))
    return pl.pallas_call(
        functools.partial(_gdn_all_kernel, tb=tb),
        grid=(t // tb,),
        in_specs=[pl.BlockSpec((b, tb, width), lambda ti: (0, ti, 0)),
                  full((GDN_CONV, 3 * GDN_W)), full((1, LANES)), full((1, LANES)), full((1, GDN_DV))],
        out_specs=pl.BlockSpec((b, tb, GDN_W), lambda ti: (0, ti, 0)),
        out_shape=jax.ShapeDtypeStruct((b, t, GDN_W), BF16),
        scratch_shapes=[pltpu.VMEM((b * nh, GDN_DK, GDN_DV), F32),
                        pltpu.VMEM((b * SUBLANES, 3 * GDN_W), F32),
                        pltpu.VMEM((b * tb, 3 * GDN_W), F32),
                        pltpu.VMEM((b, tb + SUBLANES, LANES), F32)],
        compiler_params=_params(("arbitrary",)),
        name="gdn",
    )(proj, conv_wt, lane_vec(a_log), lane_vec(dt_bias), gnorm.reshape(1, GDN_DV))


MOBA_QW = 2 * MOBA_BLOCK
MOBA_ROWS_L = BF16_ROWS
MOBA_BOUND_SLACK = 1.02
MOBA_SAFE_LOG2_RANGE = 100.0


def _moba_select_kernel(qt_ref, km_ref, kn_ref, qa_ref, bm_ref, *, nb, tq):
    nbp = -(-nb // BF16_ROWS) * BF16_ROWS
    qt = qt_ref[0]
    km = km_ref[0].astype(BF16)
    if nbp > nb:
        km = jnp.concatenate([km, jnp.zeros((nbp - nb, LANES), BF16)], axis=0)
    gate = jnp.dot(km, qt, preferred_element_type=F32)
    row = lax.broadcasted_iota(jnp.int32, (nbp, tq), 0)
    qblk = (lax.broadcasted_iota(jnp.int32, (nbp, tq), 1) + pl.program_id(2) * tq) // MOBA_BLOCK
    past = row < qblk
    g = jnp.where(past, gate, -jnp.inf)
    sel = row < 0
    for _ in range(MOBA_TOPK):
        m = jnp.max(g, axis=0, keepdims=True)
        idx = jnp.min(jnp.where(g == m, row, nbp), axis=0, keepdims=True)
        hit = row == idx
        sel = sel | (hit & past)
        g = jnp.where(hit, -jnp.inf, g)
    q32 = qt.astype(F32)
    bound = jnp.sqrt(jnp.sum(q32 * q32, axis=0, keepdims=True)) * kn_ref[:, 0:1] * MOBA_BOUND_SLACK
    qa_ref[0, 0:LANES, :] = qt
    qa_ref[0, LANES:LANES + nbp, :] = jnp.where(sel, 0.0, MASK_NEG).astype(BF16)
    tail_row = lax.broadcasted_iota(jnp.int32, (LANES - nbp, tq), 0)
    qa_ref[0, LANES + nbp:, :] = jnp.where(tail_row == LANES - nbp - 1, -bound, 0.0).astype(BF16)
    stored = bound.astype(BF16).astype(F32)
    bm_ref[0, 0] = jnp.broadcast_to(jnp.max(stored, axis=1, keepdims=True), (1, LANES))


def _moba_select(qt, kmean, k_norm_bound, tq):
    b, _, t = qt.shape
    nb = t // MOBA_BLOCK
    assert -(-nb // BF16_ROWS) * BF16_ROWS < LANES
    qaug, bmax = pl.pallas_call(
        functools.partial(_moba_select_kernel, nb=nb, tq=tq),
        grid=(b, MOBA_HEADS, t // tq),
        in_specs=[pl.BlockSpec((1, LANES, tq), lambda bi, hi, qi: (bi, hi, qi)),
                  pl.BlockSpec((1, nb, LANES), lambda bi, hi, qi: (bi, 0, hi)),
                  pl.BlockSpec((1, LANES), lambda bi, hi, qi: (0, 0))],
        out_specs=[pl.BlockSpec((1, 2 * LANES, tq), lambda bi, hi, qi: (bi, hi, qi)),
                   pl.BlockSpec((1, 1, 1, LANES), lambda bi, hi, qi: (bi, hi * (t // tq) + qi, 0, 0))],
        out_shape=[jax.ShapeDtypeStruct((b, 2 * MOBA_W, t), BF16),
                   jax.ShapeDtypeStruct((b, MOBA_HEADS * (t // tq), 1, LANES), F32)],
        compiler_params=_params(("parallel", "parallel", "parallel")),
        name="moba_select",
    )(qt, kmean, k_norm_bound)
    return qaug, jnp.max(bmax)


def _moba_kernel(qa_ref, ka_ref, vt_ref, o_ref, acc_ref, m_ref, sa_ref, sb_ref, xa_ref, xb_ref, *, nb, group):
    i2 = pl.program_id(2)
    BLK = MOBA_BLOCK

    def pv(p, start, width):
        lhs = jnp.concatenate([vt_ref[0, :, pl.ds(start, width)], jnp.ones((MOBA_ROWS_L, width), BF16)], axis=0)
        return jnp.dot(lhs, p, preferred_element_type=F32)

    key_i = lax.broadcasted_iota(jnp.int32, (BLK, BLK), 0)
    qry_i = lax.broadcasted_iota(jnp.int32, (BLK, BLK), 1)
    starts = [pl.multiple_of((2 * i2 + hf) * BLK, BLK) for hf in range(2)]
    s_own = [jnp.dot(ka_ref[0, pl.ds(starts[hf], BLK), 0:LANES], qa_ref[0, 0:LANES, hf * BLK:(hf + 1) * BLK],
                     preferred_element_type=F32) for hf in range(2)]
    s_own = [s_own[hf] + qa_ref[0, 2 * LANES - 1:2 * LANES, hf * BLK:(hf + 1) * BLK].astype(F32) for hf in range(2)]
    s_own = [jnp.where(key_i <= qry_i, s, MASK_NEG) for s in s_own]
    m_own = [jnp.max(s, axis=0, keepdims=True) for s in s_own]
    p_own = [jnp.exp2(s_own[hf] - m_own[hf]).astype(BF16) for hf in range(2)]
    for hf in range(2):
        m_ref[:, hf * BLK:(hf + 1) * BLK] = m_own[hf]
        acc_ref[:, hf * BLK:(hf + 1) * BLK] = pv(p_own[hf], starts[hf], BLK)

    span = group * BLK
    n_pairs = (2 * i2 + 2 * group) // (2 * group)
    last = nb // group - 1
    halves = [slice(hf * BLK, (hf + 1) * BLK) for hf in range(2)]

    def qk(g_idx, dst_ref, mx_dst, hs):
        start = pl.multiple_of(g_idx * span, span)
        s = jnp.dot(ka_ref[0, pl.ds(start, span), :], qa_ref[0, :, hs], preferred_element_type=F32)
        dst_ref[:, hs] = s
        mx_dst[:, hs] = jnp.max(s, axis=0, keepdims=True)

    def softmax_pv(src_ref, mx_src, g_idx, hs):
        m_old = m_ref[:, hs]
        m_new = jnp.maximum(m_old, mx_src[:, hs])
        alpha = jnp.exp2(m_old - m_new)
        m_ref[:, hs] = m_new
        p = jnp.exp2(src_ref[:, hs] - m_new).astype(BF16)
        acc_ref[:, hs] = alpha * acc_ref[:, hs] + pv(p, pl.multiple_of(g_idx * span, span), span)

    for hs in halves:
        qk(0, sa_ref, xa_ref, hs)

    def body(jj, carry):
        for hs in halves:
            qk(2 * jj + 1, sb_ref, xb_ref, hs)
            softmax_pv(sa_ref, xa_ref, 2 * jj, hs)
        for hs in halves:
            qk(jnp.minimum(2 * jj + 2, last), sa_ref, xa_ref, hs)
            softmax_pv(sb_ref, xb_ref, 2 * jj + 1, hs)
        return carry

    lax.fori_loop(0, n_pairs, body, 0)
    acc = acc_ref[...]
    o = acc[0:LANES] * (1.0 / acc[LANES:LANES + 1])
    o_ref[0] = o.T.astype(o_ref.dtype)


def _moba_bounded_kernel(qa_ref, ka_ref, vt_ref, o_ref, acc_ref, pa_ref, pb_ref, *, nb, group):
    i2 = pl.program_id(2)
    BLK = MOBA_BLOCK

    def pv(p, start, width):
        lhs = jnp.concatenate([vt_ref[0, :, pl.ds(start, width)], jnp.ones((MOBA_ROWS_L, width), BF16)], axis=0)
        return jnp.dot(lhs, p, preferred_element_type=F32)

    halves = [slice(hf * BLK, (hf + 1) * BLK) for hf in range(2)]
    key_i = lax.broadcasted_iota(jnp.int32, (BLK, BLK), 0)
    qry_i = lax.broadcasted_iota(jnp.int32, (BLK, BLK), 1)
    own = []
    for hf, hs in enumerate(halves):
        start = pl.multiple_of((2 * i2 + hf) * BLK, BLK)
        s = jnp.dot(ka_ref[0, pl.ds(start, BLK), 0:LANES], qa_ref[0, 0:LANES, hs], preferred_element_type=F32)
        s = s + qa_ref[0, 2 * LANES - 1:2 * LANES, hs].astype(F32)
        own.append((jnp.exp2(jnp.where(key_i <= qry_i, s, MASK_NEG)).astype(BF16), start))

    span = group * BLK
    n_groups = (2 * i2 + group) // group
    n_pairs = n_groups // 2
    last = nb // group - 1

    def qk_exp(g_idx, p_dst, hs):
        start = pl.multiple_of(g_idx * span, span)
        s = jnp.dot(ka_ref[0, pl.ds(start, span), :], qa_ref[0, :, hs], preferred_element_type=F32)
        p_dst[:, hs] = jnp.exp2(s).astype(BF16)

    def pv_acc(p_src, g_idx, hs):
        acc_ref[:, hs] += pv(p_src[:, hs], pl.multiple_of(g_idx * span, span), span)

    for hs in halves:
        qk_exp(0, pa_ref, hs)
    for (p, start), hs in zip(own, halves):
        acc_ref[:, hs] = pv(p, start, BLK)

    def body(jj, carry):
        for hs in halves:
            qk_exp(2 * jj + 1, pb_ref, hs)
            pv_acc(pa_ref, 2 * jj, hs)
        for hs in halves:
            qk_exp(jnp.minimum(2 * jj + 2, last), pa_ref, hs)
            pv_acc(pb_ref, 2 * jj + 1, hs)
        return carry

    lax.fori_loop(0, n_pairs, body, 0)

    @pl.when(n_groups % 2 == 1)
    def _():
        for hs in halves:
            pv_acc(pa_ref, 2 * n_pairs, hs)

    acc = acc_ref[...]
    o = acc[0:LANES] * (1.0 / acc[LANES:LANES + 1])
    o_ref[0] = o.T.astype(o_ref.dtype)


def _moba(qaug, kaug, vt, max_bound):
    b, _, t = vt.shape
    nb = t // MOBA_BLOCK
    group = 4 if nb % 8 == 0 else 2
    assert nb % (2 * group) == 0
    span = group * MOBA_BLOCK

    def call(body, scratch, name):
        return pl.pallas_call(
            functools.partial(body, nb=nb, group=group),
            grid=(b, MOBA_HEADS, t // MOBA_QW),
            in_specs=[pl.BlockSpec((1, 2 * LANES, MOBA_QW), lambda bi, hi, qi: (bi, hi, qi)),
                      pl.BlockSpec((1, t, 2 * LANES), lambda bi, hi, qi: (bi, 0, hi)),
                      pl.BlockSpec((1, LANES, t), lambda bi, hi, qi: (bi, hi, 0))],
            out_specs=pl.BlockSpec((1, MOBA_QW, LANES), lambda bi, hi, qi: (bi, qi, hi)),
            out_shape=jax.ShapeDtypeStruct((b, t, MOBA_W), BF16),
            scratch_shapes=[pltpu.VMEM((LANES + MOBA_ROWS_L, MOBA_QW), F32)] + scratch,
            compiler_params=_params(("parallel", "parallel", "arbitrary")),
            name=name,
        )

    online = call(_moba_kernel,
                  [pltpu.VMEM((1, MOBA_QW), F32),
                   pltpu.VMEM((span, MOBA_QW), F32), pltpu.VMEM((span, MOBA_QW), F32),
                   pltpu.VMEM((1, MOBA_QW), F32), pltpu.VMEM((1, MOBA_QW), F32)], "moba")
    bounded = call(_moba_bounded_kernel,
                   [pltpu.VMEM((span, MOBA_QW), BF16), pltpu.VMEM((span, MOBA_QW), BF16)], "moba_bounded")
    return lax.cond(2.0 * max_bound < MOBA_SAFE_LOG2_RANGE, bounded, online, qaug, kaug, vt)


def _oproj_kernel(og_ref, om_ref, wa_ref, wb_ref, x_ref, nw_ref, h_ref, hn_ref):
    y = jnp.dot(og_ref[...], wa_ref[...], preferred_element_type=F32)
    y = y + jnp.dot(om_ref[...], wb_ref[...], preferred_element_type=F32)
    h = x_ref[...] + y
    h_ref[...] = h
    hn = h * lax.rsqrt(jnp.mean(h * h, axis=-1, keepdims=True) + RMS_EPS)
    hn_ref[...] = (hn * nw_ref[...]).astype(hn_ref.dtype)


def _oproj(og, om, wa, wb, x, norm_w, tm):
    m, d = x.shape
    ka, kb = og.shape[1], om.shape[1]
    row = lambda w: pl.BlockSpec((tm, w), lambda i: (i, 0))
    full = lambda r, c: pl.BlockSpec((r, c), lambda i: (0, 0))
    return pl.pallas_call(
        _oproj_kernel,
        grid=(m // tm,),
        in_specs=[row(ka), row(kb), full(ka, d), full(kb, d), row(d), full(1, d)],
        out_specs=[row(d), row(d)],
        out_shape=[jax.ShapeDtypeStruct((m, d), F32), jax.ShapeDtypeStruct((m, d), BF16)],
        compiler_params=_params(("parallel",)),
        name="oproj",
    )(og, om, wa, wb, x, norm_w.reshape(1, d))


def _ffn_kernel(hn_ref, wg_ref, wu_ref, wd_ref, h_ref, nw_ref, h2_ref, hn2_ref, acc_ref):
    f = pl.program_id(1)

    @pl.when(f == 0)
    def _():
        acc_ref[...] = jnp.zeros_like(acc_ref)

    hn = hn_ref[...]
    g = jnp.dot(hn, wg_ref[...], preferred_element_type=F32)
    u = jnp.dot(hn, wu_ref[...], preferred_element_type=F32)
    a = (g * _sigmoid(g) * u).astype(BF16)
    acc_ref[...] += jnp.dot(a, wd_ref[...], preferred_element_type=F32)

    @pl.when(f == pl.num_programs(1) - 1)
    def _():
        h2 = h_ref[...] + acc_ref[...]
        h2_ref[...] = h2
        n = h2 * lax.rsqrt(jnp.mean(h2 * h2, axis=-1, keepdims=True) + RMS_EPS)
        hn2_ref[...] = (n * nw_ref[...]).astype(hn2_ref.dtype)


def _ffn(hn, wg, wu, wd, h, norm_w, tm, tf):
    m, d = h.shape
    dff = wg.shape[1]
    row = pl.BlockSpec((tm, d), lambda i, f: (i, 0))
    return pl.pallas_call(
        _ffn_kernel,
        grid=(m // tm, dff // tf),
        in_specs=[row, pl.BlockSpec((d, tf), lambda i, f: (0, f)), pl.BlockSpec((d, tf), lambda i, f: (0, f)),
                  pl.BlockSpec((tf, d), lambda i, f: (f, 0)), row, pl.BlockSpec((1, d), lambda i, f: (0, 0))],
        out_specs=[row, row],
        out_shape=[jax.ShapeDtypeStruct((m, d), F32), jax.ShapeDtypeStruct((m, d), BF16)],
        scratch_shapes=[pltpu.VMEM((tm, d), F32)],
        compiler_params=_params(("parallel", "arbitrary")),
        name="ffn",
    )(hn, wg, wu, wd, h, norm_w.reshape(1, d))


def _ple_kernel(hn_ref, wg_ref, p_ref, wp_ref, h_ref, o_ref):
    gate = _sigmoid(jnp.dot(hn_ref[...], wg_ref[...], preferred_element_type=F32))
    proj = jnp.dot(p_ref[...].astype(BF16), wp_ref[...], preferred_element_type=F32)
    o_ref[...] = h_ref[...] + gate * proj


def _ple(hn, wg, p, wp, h, tm, tn):
    m, d = h.shape
    kp = p.shape[1]
    return pl.pallas_call(
        _ple_kernel,
        grid=(m // tm, d // tn),
        in_specs=[pl.BlockSpec((tm, d), lambda i, j: (i, 0)), pl.BlockSpec((d, tn), lambda i, j: (0, j)),
                  pl.BlockSpec((tm, kp), lambda i, j: (i, 0)), pl.BlockSpec((kp, tn), lambda i, j: (0, j)),
                  pl.BlockSpec((tm, tn), lambda i, j: (i, j))],
        out_specs=pl.BlockSpec((tm, tn), lambda i, j: (i, j)),
        out_shape=jax.ShapeDtypeStruct((m, d), F32),
        compiler_params=_params(("parallel", "arbitrary")),
        name="ple",
    )(hn, wg, p, wp, h)


class _Tiles(NamedTuple):
    rows: int
    cols: int
    norm_rows: int
    wide_rows: int
    ffn_cols: int
    gdn_time: int
    select_queries: int


def _tile_plan(m, t):
    return _Tiles(rows=min(1024, m), cols=1024, norm_rows=min(512, m), wide_rows=min(512, m), ffn_cols=512,
                  gdn_time=min(4 * GDN_CHUNK, t), select_queries=min(2048, t))


def _layer(h, p, attn_norm, w_in, conv_w, a_log, dt_bias, gdn_norm, q_norm, k_norm, w_o, ffn_norm,
           w_gate, w_up, w_down, ple_norm, w_ple_gate, w_ple_proj):
    b, t, d = h.shape
    m = b * t
    x2 = h.reshape(m, d)
    tiles = _tile_plan(m, t)

    o_ba = 4 * GDN_W
    o_mq = o_ba + 2 * GDN_HEADS
    w_in16 = w_in.astype(BF16)
    w_gdn = jnp.pad(w_in16[:, :o_mq], ((0, 0), (0, LANES - 2 * GDN_HEADS)))
    w_mq = w_in16[:, o_mq:o_mq + MOBA_W]
    w_mk = w_in16[:, o_mq + MOBA_W:o_mq + 2 * MOBA_W]
    w_mv = w_in16[:, o_mq + 2 * MOBA_W:o_mq + 3 * MOBA_W]

    gproj, xn = _norm_proj(x2, attn_norm, w_gdn, F32, tiles.rows, w_gdn.shape[1] // 3, "proj_gdn")
    mq = _proj_moba_q(xn, w_mq, q_norm, tiles.rows, tiles.cols)
    kaug, kmean = _proj_moba_k(xn, w_mk, k_norm, tiles.rows, tiles.cols, t // MOBA_BLOCK)
    mv = _proj(xn, w_mv, BF16, tiles.rows, tiles.cols, "proj_moba_v")

    o_gdn = _gdn_all(gproj.reshape(b, t, 4 * GDN_W + LANES), conv_w.T, a_log, dt_bias, gdn_norm, tiles.gdn_time)
    k_norm_bound = jnp.full((1, LANES), MOBA_DH ** 0.5, F32) * jnp.max(jnp.abs(k_norm))
    qaug, max_bound = _moba_select(mq.reshape(b, t, MOBA_W).swapaxes(1, 2),
                                   kmean.reshape(b, t // MOBA_BLOCK, MOBA_W), k_norm_bound, tiles.select_queries)
    o_moba = _moba(qaug, kaug.reshape(b, t, 2 * MOBA_W), mv.reshape(b, t, MOBA_W).swapaxes(1, 2), max_bound)

    w_o16 = w_o.astype(BF16)
    h1, hn = _oproj(o_gdn.reshape(m, GDN_W), o_moba.reshape(m, MOBA_W), w_o16[:GDN_W], w_o16[GDN_W:],
                    x2, ffn_norm, tiles.wide_rows)
    h2, hn2 = _ffn(hn, w_gate.astype(BF16), w_up.astype(BF16), w_down.astype(BF16), h1, ple_norm,
                   tiles.wide_rows, tiles.ffn_cols)
    h3 = _ple(hn2, w_ple_gate.astype(BF16), p.reshape(m, PLE_DIM), w_ple_proj.astype(BF16), h2,
              tiles.rows, tiles.cols)
    return h3.reshape(b, t, d)


def kernel(x, p, attn_norm, w_in, conv_w, A_log, dt_bias, gdn_norm, q_norm, k_norm, w_o, ffn_norm,
           w_gate, w_up, w_down, ple_norm, w_ple_gate, w_ple_proj):
    h = x
    for i in range(p.shape[0]):
        h = _layer(h, p[i], attn_norm[i], w_in[i], conv_w[i], A_log[i], dt_bias[i], gdn_norm[i],
                   q_norm[i], k_norm[i], w_o[i], ffn_norm[i], w_gate[i], w_up[i], w_down[i],
                   ple_norm[i], w_ple_gate[i], w_ple_proj[i])
    return h
```

```python
import functools
from typing import NamedTuple

import jax
import jax.numpy as jnp
from jax import lax
from jax.experimental import pallas as pl
from jax.experimental.pallas import tpu as pltpu

D_MODEL = 2048
PLE_DIM = 256
GDN_HEADS = 8
GDN_DK = 128
GDN_DV = 128
GDN_CONV = 4
GDN_CHUNK = 64
GDN_CHUNKS_WIDE = 2
MOBA_HEADS = 8
MOBA_DH = 128
MOBA_BLOCK = 256
MOBA_TOPK = 3
RMS_EPS = 1e-6
GDN_W = GDN_HEADS * GDN_DK
MOBA_W = MOBA_HEADS * MOBA_DH
LANES = 128
SUBLANES = 8
BF16_ROWS = 16
GDN_L2_EPS = 1e-6
LOG2E = 1.4426950408889634
MASK_NEG = -1e30

F32 = jnp.float32
BF16 = jnp.bfloat16

VMEM_LIMIT = 56 * 1024 * 1024


def _params(sem):
    return pltpu.CompilerParams(dimension_semantics=sem, vmem_limit_bytes=VMEM_LIMIT)


def _sigmoid(x):
    return 1.0 / (1.0 + jnp.exp(-x))


def _dot_t(a, b):
    return lax.dot_general(a, b, (((1,), (1,)), ((), ())), preferred_element_type=F32)


def _bdot(a, b):
    return jnp.dot(a.astype(BF16), b.astype(BF16), preferred_element_type=F32)


def _proj_kernel(x_ref, w_ref, o_ref):
    o_ref[...] = jnp.dot(x_ref[...], w_ref[...], preferred_element_type=F32).astype(o_ref.dtype)


def _proj(x, w, out_dtype, tm, tn, name):
    m, k = x.shape
    n = w.shape[1]
    return pl.pallas_call(
        _proj_kernel,
        grid=(m // tm, n // tn),
        in_specs=[pl.BlockSpec((tm, k), lambda i, j: (i, 0)), pl.BlockSpec((k, tn), lambda i, j: (0, j))],
        out_specs=pl.BlockSpec((tm, tn), lambda i, j: (i, j)),
        out_shape=jax.ShapeDtypeStruct((m, n), out_dtype),
        compiler_params=_params(("parallel", "arbitrary")),
        name=name,
    )(x, w)


def _norm_proj_kernel(x_ref, nw_ref, w_ref, o_ref, xn_ref, *, rows_per_pass):
    @pl.when(pl.program_id(1) == 0)
    def _():
        def norm_rows(r, carry):
            rows = pl.ds(pl.multiple_of(r * rows_per_pass, rows_per_pass), rows_per_pass)
            x = x_ref[rows, :]
            y = x * lax.rsqrt(jnp.mean(x * x, axis=-1, keepdims=True) + RMS_EPS)
            xn_ref[rows, :] = (y * nw_ref[...]).astype(xn_ref.dtype)
            return carry
        lax.fori_loop(0, x_ref.shape[0] // rows_per_pass, norm_rows, 0)

    o_ref[...] = jnp.dot(xn_ref[...], w_ref[...], preferred_element_type=F32).astype(o_ref.dtype)


def _norm_proj(x, norm_w, w, out_dtype, tm, tn, name):
    m, k = x.shape
    n = w.shape[1]
    return pl.pallas_call(
        functools.partial(_norm_proj_kernel, rows_per_pass=min(128, tm)),
        grid=(m // tm, n // tn),
        in_specs=[pl.BlockSpec((tm, k), lambda i, j: (i, 0)), pl.BlockSpec((1, k), lambda i, j: (0, 0)),
                  pl.BlockSpec((k, tn), lambda i, j: (0, j))],
        out_specs=[pl.BlockSpec((tm, tn), lambda i, j: (i, j)), pl.BlockSpec((tm, k), lambda i, j: (i, 0))],
        out_shape=[jax.ShapeDtypeStruct((m, n), out_dtype), jax.ShapeDtypeStruct((m, k), BF16)],
        compiler_params=_params(("parallel", "arbitrary")),
        name=name,
    )(x, norm_w.reshape(1, k), w)


def _head_rmsnorm(y, gain, scale):
    outs = []
    for h in range(y.shape[1] // LANES):
        yh = y[:, h * LANES:(h + 1) * LANES]
        r = lax.rsqrt(jnp.mean(yh * yh, axis=-1, keepdims=True) + RMS_EPS)
        outs.append(yh * r * gain[:, h * LANES:(h + 1) * LANES] * scale)
    return jnp.concatenate(outs, axis=1)


def _proj_qnorm_kernel(x_ref, w_ref, g_ref, o_ref, *, scale):
    y = jnp.dot(x_ref[...], w_ref[...], preferred_element_type=F32)
    o_ref[...] = _head_rmsnorm(y, g_ref[...], scale).astype(o_ref.dtype)


def _proj_knorm_kernel(x_ref, w_ref, g_ref, o_ref, km_ref, *, nb_seq):
    y = jnp.dot(x_ref[...], w_ref[...], preferred_element_type=F32)
    yn = _head_rmsnorm(y, g_ref[...], 1.0)
    tm = yn.shape[0]
    row = lax.broadcasted_iota(jnp.int32, (tm, LANES), 0) + pl.program_id(0) * tm
    lane = lax.broadcasted_iota(jnp.int32, (tm, LANES), 1)
    onehot = jnp.where((lane == lax.rem(row // MOBA_BLOCK, nb_seq)) | (lane == LANES - 1), 1.0, 0.0)
    onehot = onehot.astype(o_ref.dtype)
    yb = yn.astype(o_ref.dtype)
    parts = []
    for h in range(yn.shape[1] // LANES):
        parts += [yb[:, h * LANES:(h + 1) * LANES], onehot]
    o_ref[...] = jnp.concatenate(parts, axis=1)
    for r in range(tm // MOBA_BLOCK):
        blk = yn[r * MOBA_BLOCK:(r + 1) * MOBA_BLOCK]
        km_ref[r] = jnp.mean(blk, axis=0, keepdims=True)


def _proj_moba_q(x, w, gain, tm, tn):
    m, k = x.shape
    n = w.shape[1]
    g = jnp.tile(gain.reshape(1, MOBA_DH), (1, n // MOBA_DH))
    return pl.pallas_call(
        functools.partial(_proj_qnorm_kernel, scale=MOBA_DH ** -0.5 * LOG2E),
        grid=(m // tm, n // tn),
        in_specs=[pl.BlockSpec((tm, k), lambda i, j: (i, 0)), pl.BlockSpec((k, tn), lambda i, j: (0, j)),
                  pl.BlockSpec((1, tn), lambda i, j: (0, j))],
        out_specs=pl.BlockSpec((tm, tn), lambda i, j: (i, j)),
        out_shape=jax.ShapeDtypeStruct((m, n), BF16),
        compiler_params=_params(("parallel", "arbitrary")),
        name="proj_moba_q",
    )(x, w, g)


def _proj_moba_k(x, w, gain, tm, tn, nb_seq):
    m, k = x.shape
    n = w.shape[1]
    g = jnp.tile(gain.reshape(1, MOBA_DH), (1, n // MOBA_DH))
    nb = tm // MOBA_BLOCK
    return pl.pallas_call(
        functools.partial(_proj_knorm_kernel, nb_seq=nb_seq),
        grid=(m // tm, n // tn),
        in_specs=[pl.BlockSpec((tm, k), lambda i, j: (i, 0)), pl.BlockSpec((k, tn), lambda i, j: (0, j)),
                  pl.BlockSpec((1, tn), lambda i, j: (0, j))],
        out_specs=[pl.BlockSpec((tm, 2 * tn), lambda i, j: (i, j)),
                   pl.BlockSpec((nb, 1, tn), lambda i, j: (i, 0, j))],
        out_shape=[jax.ShapeDtypeStruct((m, 2 * n), BF16),
                   jax.ShapeDtypeStruct((m // MOBA_BLOCK, 1, n), F32)],
        compiler_params=_params(("parallel", "arbitrary")),
        name="proj_moba_k",
    )(x, w, g)


def _gdn_all_kernel(x_ref, cw_ref, alog_ref, dtb_ref, gn_ref, o_ref, s_ref, tail_ref, qkv_s, xp_s, *, tb):
    t = pl.program_id(0)
    C = GDN_CHUNK
    nc = tb // C
    nh = GDN_HEADS
    nbat = x_ref.shape[0]

    @pl.when(t == 0)
    def _():
        s_ref[...] = jnp.zeros_like(s_ref)
        tail_ref[...] = jnp.zeros_like(tail_ref)

    def conv_group(gi, l2_scale):
        off = pl.multiple_of(gi * LANES, LANES)
        w = cw_ref[:, pl.ds(off, LANES)]
        for bi in range(nbat):
            x = x_ref[bi, :, pl.ds(off, LANES)]
            tail_rows = slice(bi * SUBLANES, (bi + 1) * SUBLANES)
            xp_s[bi, 0:SUBLANES, :] = tail_ref[tail_rows, pl.ds(off, LANES)]
            xp_s[bi, SUBLANES:SUBLANES + tb, :] = x
            y = x * w[GDN_CONV - 1:GDN_CONV]
            for j in range(GDN_CONV - 1):
                lag = GDN_CONV - 1 - j
                y = y + xp_s[bi, SUBLANES - lag:SUBLANES - lag + tb, :] * w[j:j + 1]
            tail_ref[tail_rows, pl.ds(off, LANES)] = x[tb - SUBLANES:tb]
            y = y * _sigmoid(y)
            if l2_scale is not None:
                y = y * (lax.rsqrt(jnp.sum(y * y, axis=-1, keepdims=True) + GDN_L2_EPS) * l2_scale)
            qkv_s[bi * tb:(bi + 1) * tb, pl.ds(off, LANES)] = y

    for g0, l2_scale in ((0, GDN_DK ** -0.5), (nh, 1.0), (2 * nh, None)):
        lax.fori_loop(g0, g0 + nh, lambda gi, carry, sc=l2_scale: conv_group(gi, sc), None)

    row_in_chunk = lax.broadcasted_iota(jnp.int32, (tb, LANES), 0) & (C - 1)
    beta_all, gc, egc, kdsc, egl, gc_t = [], [], [], [], [], []
    for bi in range(nbat):
        ba = x_ref[bi, :, 4 * GDN_W:4 * GDN_W + LANES]
        beta_all.append(_sigmoid(ba))
        sp_in = ba + dtb_ref[...]
        softplus = jnp.maximum(sp_in, 0.0) + jnp.log1p(jnp.exp(-jnp.abs(sp_in)))
        g = -(jnp.exp(alog_ref[...]) * softplus)
        sh = 1
        while sh < C:
            g = g + jnp.where(row_in_chunk >= sh, pltpu.roll(g, sh, axis=0), 0.0)
            sh *= 2
        gl_rows = [g[(c + 1) * C - 1:(c + 1) * C] for c in range(nc)]
        gl_b = jnp.concatenate([jnp.broadcast_to(r, (C, LANES)) for r in gl_rows], axis=0)
        gc.append(g)
        egc.append(jnp.exp(g))
        kdsc.append(jnp.exp(gl_b - g))
        egl.append([jnp.exp(r) for r in gl_rows])
        gc_t.append(g.T)

    ri = lax.broadcasted_iota(jnp.int32, (C, 2 * C), 0)
    lane = lax.broadcasted_iota(jnp.int32, (C, 2 * C), 1)
    ci = lane & (C - 1)
    left = lane < C
    tril = ri >= ci
    strict = ri > ci
    eye_f = (ri == ci).astype(F32)
    lvl_masks = []
    s = 1
    while s < C:
        sh2 = s.bit_length()
        lvl_masks.append((ri >> sh2 == ci >> sh2) & ((ri & (2 * s - 1)) >= s) & ((ci & (2 * s - 1)) < s))
        s *= 2
    gn = gn_ref[...]

    def block_diag(m_l, m_r):
        z = jnp.zeros_like(m_l)
        return jnp.concatenate([jnp.concatenate([m_l, z], axis=1), jnp.concatenate([z, m_r], axis=1)], axis=0)

    def unpack_diag(m):
        return jnp.concatenate([jnp.where(left, m, 0.0), jnp.where(left, 0.0, m)], axis=0)

    seqs = [(bi, h) for bi in range(nbat) for h in range(nh)]
    ids = range(len(seqs))
    assert len(seqs) % 2 == 0
    pairs = [(2 * p, 2 * p + 1) for p in range(len(seqs) // 2)]
    pids = range(len(pairs))
    states = [s_ref[i] for i in ids]
    grp = lambda g, h: slice((g * nh + h) * LANES, (g * nh + h + 1) * LANES)
    def operands(c):
        rows = slice(c * C, (c + 1) * C)
        srow = lambda bi: slice(bi * tb + c * C, bi * tb + (c + 1) * C)
        col = lambda arr, bi, h: arr[bi][rows, nh + h:nh + h + 1]
        d = dict(rows=rows,
                 qc=[qkv_s[srow(bi), grp(0, h)] for bi, h in seqs],
                 kc=[qkv_s[srow(bi), grp(1, h)] for bi, h in seqs],
                 vc=[qkv_s[srow(bi), grp(2, h)] for bi, h in seqs],
                 bcol=[beta_all[bi][rows, h:h + 1] for bi, h in seqs],
                 ecol=[col(egc, bi, h) for bi, h in seqs],
                 gcol=[col(gc, bi, h) for bi, h in seqs],
                 kdcol=[col(kdsc, bi, h) for bi, h in seqs],
                 grow=[gc_t[bi][nh + h:nh + h + 1, rows] for bi, h in seqs])
        d["k_beta"] = [d["kc"][i] * d["bcol"][i] for i in ids]
        return d

    def stateless(chunks):
        ops = {c: operands(c) for c in chunks}
        items = [(c, p) for c in chunks for p in pids]
        nit = range(len(items))
        decay, st = [], []
        for c, p in items:
            d, (l, r) = ops[c], pairs[p]
            decay.append(jnp.exp(jnp.where(
                tril, jnp.where(left, d["gcol"][l], d["gcol"][r])
                - jnp.concatenate([d["grow"][l], d["grow"][r]], axis=1), -jnp.inf)))
        for c, p in items:
            d, (l, r) = ops[c], pairs[p]
            st.append(_dot_t(jnp.concatenate([jnp.concatenate([d["k_beta"][l], d["k_beta"][r]], axis=1),
                                              jnp.concatenate([d["qc"][l], d["qc"][r]], axis=1)],
                                             axis=0).astype(BF16),
                             block_diag(d["kc"][l], d["kc"][r]).astype(BF16)))
        a_mat = [jnp.where(strict, st[n][:C] * decay[n], 0.0) for n in nit]
        qk = [jnp.where(tril, st[n][C:] * decay[n], 0.0) for n in nit]
        t_inv = [eye_f - jnp.where(lvl_masks[0], a_mat[n], 0.0) for n in nit]
        for msk in lvl_masks[1:]:
            x_mid = [_bdot(jnp.where(msk, a_mat[n], 0.0), unpack_diag(t_inv[n])) for n in nit]
            y_mid = [_bdot(t_inv[n], unpack_diag(x_mid[n])) for n in nit]
            t_inv = [t_inv[n] - y_mid[n] for n in nit]
        uw_p = []
        for n, (c, p) in enumerate(items):
            d, (l, r) = ops[c], pairs[p]
            rhs = [jnp.concatenate([d["vc"][i] * d["bcol"][i], d["k_beta"][i] * d["ecol"][i]], axis=1)
                   for i in (l, r)]
            uw_p.append(_bdot(t_inv[n], block_diag(rhs[0], rhs[1])))
        out = {}
        for k, c in enumerate(chunks):
            base = k * len(pairs)
            uw = [uw_p[base + i // 2][:, (i % 2) * 2 * LANES:(i % 2 + 1) * 2 * LANES] for i in ids]
            out[c] = (ops[c], uw, qk[base:base + len(pairs)])
        return out

    def recurrent(c, d, uw, qk, states):
        wq = [_bdot(jnp.concatenate([uw[i][:, LANES:], d["qc"][i] * d["ecol"][i]], axis=0), states[i])
              for i in ids]
        v_new = [uw[i][:, :LANES] - wq[i][:C] for i in ids]
        o_p = [_bdot(qk[p], block_diag(v_new[l], v_new[r])) for p, (l, r) in enumerate(pairs)]
        o = [wq[i][C:] + o_p[i // 2][:, (i % 2) * LANES:(i % 2 + 1) * LANES] for i in ids]
        new_states = [states[i] * egl[bi][c][:, nh + h:nh + h + 1] + lax.dot_general(
            (d["kc"][i] * d["kdcol"][i]).astype(BF16), v_new[i].astype(BF16), (((0,), (0,)), ((), ())),
            preferred_element_type=F32) for i, (bi, h) in enumerate(seqs)]
        for i, (bi, h) in enumerate(seqs):
            on = o[i] * lax.rsqrt(jnp.mean(o[i] * o[i], axis=-1, keepdims=True) + RMS_EPS) * gn
            zc = x_ref[bi, d["rows"], grp(3, h)]
            o_ref[bi, d["rows"], grp(0, h)] = (on * (zc * _sigmoid(zc))).astype(o_ref.dtype)
        return new_states

    for c0 in range(0, nc, GDN_CHUNKS_WIDE):
        chunks = list(range(c0, min(c0 + GDN_CHUNKS_WIDE, nc)))
        ready = stateless(chunks)
        for c in chunks:
            states = recurrent(c, *ready[c], states)
    for i in ids:
        s_ref[i] = states[i]


def _gdn_all(proj, conv_wt, a_log, dt_bias, gnorm, tb):
    b, t, width = proj.shape
    nh = GDN_HEADS
    lane_vec = lambda v: jnp.pad(v.reshape(1, nh), ((0, 0), (nh, LANES - 2 * nh)))
    full = lambda shape: pl.BlockSpec(shape, lambda ti: (0,) * len(shape))
    return pl.pallas_call(
        functools.partial(_gdn_all_kernel, tb=tb),
        grid=(t // tb,),
        in_specs=[pl.BlockSpec((b, tb, width), lambda ti: (0, ti, 0)),
                  full((GDN_CONV, 3 * GDN_W)), full((1, LANES)), full((1, LANES)), full((1, GDN_DV))],
        out_specs=pl.BlockSpec((b, tb, GDN_W), lambda ti: (0, ti, 0)),
        out_shape=jax.ShapeDtypeStruct((b, t, GDN_W), BF16),
        scratch_shapes=[pltpu.VMEM((b * nh, GDN_DK, GDN_DV), F32),
                        pltpu.VMEM((b * SUBLANES, 3 * GDN_W), F32),
                        pltpu.VMEM((b * tb, 3 * GDN_W), F32),
                        pltpu.VMEM((b, tb + SUBLANES, LANES), F32)],
        compiler_params=_params(("arbitrary",)),
        name="gdn",
    )(proj, conv_wt, lane_vec(a_log), lane_vec(dt_bias), gnorm.reshape(1, GDN_DV))


MOBA_QW = 2 * MOBA_BLOCK
MOBA_ROWS_L = BF16_ROWS
MOBA_BOUND_SLACK = 1.02
MOBA_SAFE_LOG2_RANGE = 100.0


def _moba_select_kernel(qt_ref, km_ref, kn_ref, qa_ref, bm_ref, *, nb, tq):
    nbp = -(-nb // BF16_ROWS) * BF16_ROWS
    qt = qt_ref[0]
    km = km_ref[0].astype(BF16)
    if nbp > nb:
        km = jnp.concatenate([km, jnp.zeros((nbp - nb, LANES), BF16)], axis=0)
    gate = jnp.dot(km, qt, preferred_element_type=F32)
    row = lax.broadcasted_iota(jnp.int32, (nbp, tq), 0)
    qblk = (lax.broadcasted_iota(jnp.int32, (nbp, tq), 1) + pl.program_id(2) * tq) // MOBA_BLOCK
    past = row < qblk
    g = jnp.where(past, gate, -jnp.inf)
    sel = row < 0
    for _ in range(MOBA_TOPK):
        m = jnp.max(g, axis=0, keepdims=True)
        idx = jnp.min(jnp.where(g == m, row, nbp), axis=0, keepdims=True)
        hit = row == idx
        sel = sel | (hit & past)
        g = jnp.where(hit, -jnp.inf, g)
    q32 = qt.astype(F32)
    bound = jnp.sqrt(jnp.sum(q32 * q32, axis=0, keepdims=True)) * kn_ref[:, 0:1] * MOBA_BOUND_SLACK
    qa_ref[0, 0:LANES, :] = qt
    qa_ref[0, LANES:LANES + nbp, :] = jnp.where(sel, 0.0, MASK_NEG).astype(BF16)
    tail_row = lax.broadcasted_iota(jnp.int32, (LANES - nbp, tq), 0)
    qa_ref[0, LANES + nbp:, :] = jnp.where(tail_row == LANES - nbp - 1, -bound, 0.0).astype(BF16)
    stored = bound.astype(BF16).astype(F32)
    bm_ref[0, 0] = jnp.broadcast_to(jnp.max(stored, axis=1, keepdims=True), (1, LANES))


def _moba_select(qt, kmean, k_norm_bound, tq):
    b, _, t = qt.shape
    nb = t // MOBA_BLOCK
    assert -(-nb // BF16_ROWS) * BF16_ROWS < LANES
    qaug, bmax = pl.pallas_call(
        functools.partial(_moba_select_kernel, nb=nb, tq=tq),
        grid=(b, MOBA_HEADS, t // tq),
        in_specs=[pl.BlockSpec((1, LANES, tq), lambda bi, hi, qi: (bi, hi, qi)),
                  pl.BlockSpec((1, nb, LANES), lambda bi, hi, qi: (bi, 0, hi)),
                  pl.BlockSpec((1, LANES), lambda bi, hi, qi: (0, 0))],
        out_specs=[pl.BlockSpec((1, 2 * LANES, tq), lambda bi, hi, qi: (bi, hi, qi)),
                   pl.BlockSpec((1, 1, 1, LANES), lambda bi, hi, qi: (bi, hi * (t // tq) + qi, 0, 0))],
        out_shape=[jax.ShapeDtypeStruct((b, 2 * MOBA_W, t), BF16),
                   jax.ShapeDtypeStruct((b, MOBA_HEADS * (t // tq), 1, LANES), F32)],
        compiler_params=_params(("parallel", "parallel", "parallel")),
        name="moba_select",
    )(qt, kmean, k_norm_bound)
    return qaug, jnp.max(bmax)


def _moba_kernel(qa_ref, ka_ref, vt_ref, o_ref, acc_ref, m_ref, sa_ref, sb_ref, xa_ref, xb_ref, *, nb, group):
    i2 = pl.program_id(2)
    BLK = MOBA_BLOCK

    def pv(p, start, width):
        lhs = jnp.concatenate([vt_ref[0, :, pl.ds(start, width)], jnp.ones((MOBA_ROWS_L, width), BF16)], axis=0)
        return jnp.dot(lhs, p, preferred_element_type=F32)

    key_i = lax.broadcasted_iota(jnp.int32, (BLK, BLK), 0)
    qry_i = lax.broadcasted_iota(jnp.int32, (BLK, BLK), 1)
    starts = [pl.multiple_of((2 * i2 + hf) * BLK, BLK) for hf in range(2)]
    s_own = [jnp.dot(ka_ref[0, pl.ds(starts[hf], BLK), 0:LANES], qa_ref[0, 0:LANES, hf * BLK:(hf + 1) * BLK],
                     preferred_element_type=F32) for hf in range(2)]
    s_own = [s_own[hf] + qa_ref[0, 2 * LANES - 1:2 * LANES, hf * BLK:(hf + 1) * BLK].astype(F32) for hf in range(2)]
    s_own = [jnp.where(key_i <= qry_i, s, MASK_NEG) for s in s_own]
    m_own = [jnp.max(s, axis=0, keepdims=True) for s in s_own]
    p_own = [jnp.exp2(s_own[hf] - m_own[hf]).astype(BF16) for hf in range(2)]
    for hf in range(2):
        m_ref[:, hf * BLK:(hf + 1) * BLK] = m_own[hf]
        acc_ref[:, hf * BLK:(hf + 1) * BLK] = pv(p_own[hf], starts[hf], BLK)

    span = group * BLK
    n_pairs = (2 * i2 + 2 * group) // (2 * group)
    last = nb // group - 1
    halves = [slice(hf * BLK, (hf + 1) * BLK) for hf in range(2)]

    def qk(g_idx, dst_ref, mx_dst, hs):
        start = pl.multiple_of(g_idx * span, span)
        s = jnp.dot(ka_ref[0, pl.ds(start, span), :], qa_ref[0, :, hs], preferred_element_type=F32)
        dst_ref[:, hs] = s
        mx_dst[:, hs] = jnp.max(s, axis=0, keepdims=True)

    def softmax_pv(src_ref, mx_src, g_idx, hs):
        m_old = m_ref[:, hs]
        m_new = jnp.maximum(m_old, mx_src[:, hs])
        alpha = jnp.exp2(m_old - m_new)
        m_ref[:, hs] = m_new
        p = jnp.exp2(src_ref[:, hs] - m_new).astype(BF16)
        acc_ref[:, hs] = alpha * acc_ref[:, hs] + pv(p, pl.multiple_of(g_idx * span, span), span)

    for hs in halves:
        qk(0, sa_ref, xa_ref, hs)

    def body(jj, carry):
        for hs in halves:
            qk(2 * jj + 1, sb_ref, xb_ref, hs)
            softmax_pv(sa_ref, xa_ref, 2 * jj, hs)
        for hs in halves:
            qk(jnp.minimum(2 * jj + 2, last), sa_ref, xa_ref, hs)
            softmax_pv(sb_ref, xb_ref, 2 * jj + 1, hs)
        return carry

    lax.fori_loop(0, n_pairs, body, 0)
    acc = acc_ref[...]
    o = acc[0:LANES] * (1.0 / acc[LANES:LANES + 1])
    o_ref[0] = o.T.astype(o_ref.dtype)


def _moba_bounded_kernel(qa_ref, ka_ref, vt_ref, o_ref, acc_ref, pa_ref, pb_ref, *, nb, group):
    i2 = pl.program_id(2)
    BLK = MOBA_BLOCK

    def pv(p, start, width):
        lhs = jnp.concatenate([vt_ref[0, :, pl.ds(start, width)], jnp.ones((MOBA_ROWS_L, width), BF16)], axis=0)
        return jnp.dot(lhs, p, preferred_element_type=F32)

    halves = [slice(hf * BLK, (hf + 1) * BLK) for hf in range(2)]
    key_i = lax.broadcasted_iota(jnp.int32, (BLK, BLK), 0)
    qry_i = lax.broadcasted_iota(jnp.int32, (BLK, BLK), 1)
    own = []
    for hf, hs in enumerate(halves):
        start = pl.multiple_of((2 * i2 + hf) * BLK, BLK)
        s = jnp.dot(ka_ref[0, pl.ds(start, BLK), 0:LANES], qa_ref[0, 0:LANES, hs], preferred_element_type=F32)
        s = s + qa_ref[0, 2 * LANES - 1:2 * LANES, hs].astype(F32)
        own.append((jnp.exp2(jnp.where(key_i <= qry_i, s, MASK_NEG)).astype(BF16), start))

    span = group * BLK
    n_groups = (2 * i2 + group) // group
    n_pairs = n_groups // 2
    last = nb // group - 1

    def qk_exp(g_idx, p_dst, hs):
        start = pl.multiple_of(g_idx * span, span)
        s = jnp.dot(ka_ref[0, pl.ds(start, span), :], qa_ref[0, :, hs], preferred_element_type=F32)
        p_dst[:, hs] = jnp.exp2(s).astype(BF16)

    def pv_acc(p_src, g_idx, hs):
        acc_ref[:, hs] += pv(p_src[:, hs], pl.multiple_of(g_idx * span, span), span)

    for hs in halves:
        qk_exp(0, pa_ref, hs)
    for (p, start), hs in zip(own, halves):
        acc_ref[:, hs] = pv(p, start, BLK)

    def body(jj, carry):
        for hs in halves:
            qk_exp(2 * jj + 1, pb_ref, hs)
            pv_acc(pa_ref, 2 * jj, hs)
        for hs in halves:
            qk_exp(jnp.minimum(2 * jj + 2, last), pa_ref, hs)
            pv_acc(pb_ref, 2 * jj + 1, hs)
        return carry

    lax.fori_loop(0, n_pairs, body, 0)

    @pl.when(n_groups % 2 == 1)
    def _():
        for hs in halves:
            pv_acc(pa_ref, 2 * n_pairs, hs)

    acc = acc_ref[...]
    o = acc[0:LANES] * (1.0 / acc[LANES:LANES + 1])
    o_ref[0] = o.T.astype(o_ref.dtype)


def _moba(qaug, kaug, vt, max_bound):
    b, _, t = vt.shape
    nb = t // MOBA_BLOCK
    group = 4 if nb % 8 == 0 else 2
    assert nb % (2 * group) == 0
    span = group * MOBA_BLOCK

    def call(body, scratch, name):
        return pl.pallas_call(
            functools.partial(body, nb=nb, group=group),
            grid=(b, MOBA_HEADS, t // MOBA_QW),
            in_specs=[pl.BlockSpec((1, 2 * LANES, MOBA_QW), lambda bi, hi, qi: (bi, hi, qi)),
                      pl.BlockSpec((1, t, 2 * LANES), lambda bi, hi, qi: (bi, 0, hi)),
                      pl.BlockSpec((1, LANES, t), lambda bi, hi, qi: (bi, hi, 0))],
            out_specs=pl.BlockSpec((1, MOBA_QW, LANES), lambda bi, hi, qi: (bi, qi, hi)),
            out_shape=jax.ShapeDtypeStruct((b, t, MOBA_W), BF16),
            scratch_shapes=[pltpu.VMEM((LANES + MOBA_ROWS_L, MOBA_QW), F32)] + scratch,
            compiler_params=_params(("parallel", "parallel", "arbitrary")),
            name=name,
        )

    online = call(_moba_kernel,
                  [pltpu.VMEM((1, MOBA_QW), F32),
                   pltpu.VMEM((span, MOBA_QW), F32), pltpu.VMEM((span, MOBA_QW), F32),
                   pltpu.VMEM((1, MOBA_QW), F32), pltpu.VMEM((1, MOBA_QW), F32)], "moba")
    bounded = call(_moba_bounded_kernel,
                   [pltpu.VMEM((span, MOBA_QW), BF16), pltpu.VMEM((span, MOBA_QW), BF16)], "moba_bounded")
    return lax.cond(2.0 * max_bound < MOBA_SAFE_LOG2_RANGE, bounded, online, qaug, kaug, vt)


def _oproj_kernel(og_ref, om_ref, w_ref, x_ref, nw_ref, h_ref, hn_ref):
    mix = jnp.concatenate([og_ref[...], om_ref[...]], axis=1)
    h = x_ref[...] + jnp.dot(mix, w_ref[...], preferred_element_type=F32)
    h_ref[...] = h
    hn = h * lax.rsqrt(jnp.mean(h * h, axis=-1, keepdims=True) + RMS_EPS)
    hn_ref[...] = (hn * nw_ref[...]).astype(hn_ref.dtype)


def _oproj(og, om, w, x, norm_w, tm):
    m, d = x.shape
    ka, kb = og.shape[1], om.shape[1]
    row = lambda width: pl.BlockSpec((tm, width), lambda i: (i, 0))
    full = lambda r, c: pl.BlockSpec((r, c), lambda i: (0, 0))
    return pl.pallas_call(
        _oproj_kernel,
        grid=(m // tm,),
        in_specs=[row(ka), row(kb), full(ka + kb, d), row(d), full(1, d)],
        out_specs=[row(d), row(d)],
        out_shape=[jax.ShapeDtypeStruct((m, d), F32), jax.ShapeDtypeStruct((m, d), BF16)],
        compiler_params=_params(("parallel",)),
        name="oproj",
    )(og, om, w, x, norm_w.reshape(1, d))


def _ffn_kernel(hn_ref, wg_ref, wu_ref, wd_ref, h_ref, nw_ref, h2_ref, hn2_ref, acc_ref):
    f = pl.program_id(1)

    @pl.when(f == 0)
    def _():
        acc_ref[...] = jnp.zeros_like(acc_ref)

    hn = hn_ref[...]
    g = jnp.dot(hn, wg_ref[...], preferred_element_type=F32)
    u = jnp.dot(hn, wu_ref[...], preferred_element_type=F32)
    a = (g * _sigmoid(g) * u).astype(BF16)
    acc_ref[...] += jnp.dot(a, wd_ref[...], preferred_element_type=F32)

    @pl.when(f == pl.num_programs(1) - 1)
    def _():
        h2 = h_ref[...] + acc_ref[...]
        h2_ref[...] = h2
        n = h2 * lax.rsqrt(jnp.mean(h2 * h2, axis=-1, keepdims=True) + RMS_EPS)
        hn2_ref[...] = (n * nw_ref[...]).astype(hn2_ref.dtype)


def _ffn(hn, wg, wu, wd, h, norm_w, tm, tf):
    m, d = h.shape
    dff = wg.shape[1]
    row = pl.BlockSpec((tm, d), lambda i, f: (i, 0))
    return pl.pallas_call(
        _ffn_kernel,
        grid=(m // tm, dff // tf),
        in_specs=[row, pl.BlockSpec((d, tf), lambda i, f: (0, f)), pl.BlockSpec((d, tf), lambda i, f: (0, f)),
                  pl.BlockSpec((tf, d), lambda i, f: (f, 0)), row, pl.BlockSpec((1, d), lambda i, f: (0, 0))],
        out_specs=[row, row],
        out_shape=[jax.ShapeDtypeStruct((m, d), F32), jax.ShapeDtypeStruct((m, d), BF16)],
        scratch_shapes=[pltpu.VMEM((tm, d), F32)],
        compiler_params=_params(("parallel", "arbitrary")),
        name="ffn",
    )(hn, wg, wu, wd, h, norm_w.reshape(1, d))


def _ple_kernel(hn_ref, wg_ref, p_ref, wp_ref, h_ref, o_ref):
    gate = _sigmoid(jnp.dot(hn_ref[...], wg_ref[...], preferred_element_type=F32))
    proj = jnp.dot(p_ref[...].astype(BF16), wp_ref[...], preferred_element_type=F32)
    o_ref[...] = h_ref[...] + gate * proj


def _ple(hn, wg, p, wp, h, tm, tn):
    m, d = h.shape
    kp = p.shape[1]
    return pl.pallas_call(
        _ple_kernel,
        grid=(m // tm, d // tn),
        in_specs=[pl.BlockSpec((tm, d), lambda i, j: (i, 0)), pl.BlockSpec((d, tn), lambda i, j: (0, j)),
                  pl.BlockSpec((tm, kp), lambda i, j: (i, 0)), pl.BlockSpec((kp, tn), lambda i, j: (0, j)),
                  pl.BlockSpec((tm, tn), lambda i, j: (i, j))],
        out_specs=pl.BlockSpec((tm, tn), lambda i, j: (i, j)),
        out_shape=jax.ShapeDtypeStruct((m, d), F32),
        compiler_params=_params(("parallel", "arbitrary")),
        name="ple",
    )(hn, wg, p, wp, h)


class _Tiles(NamedTuple):
    rows: int
    cols: int
    wide_rows: int
    ffn_cols: int
    gdn_time: int
    select_queries: int


def _tile_plan(m, t):
    return _Tiles(rows=min(1024, m), cols=1024, wide_rows=min(512, m), ffn_cols=512,
                  gdn_time=min(4 * GDN_CHUNK, t), select_queries=min(2048, t))


def _layer(h, p, attn_norm, w_in, conv_w, a_log, dt_bias, gdn_norm, q_norm, k_norm, w_o, ffn_norm,
           w_gate, w_up, w_down, ple_norm, w_ple_gate, w_ple_proj):
    b, t, d = h.shape
    m = b * t
    x2 = h.reshape(m, d)
    tiles = _tile_plan(m, t)

    o_ba = 4 * GDN_W
    o_mq = o_ba + 2 * GDN_HEADS
    w_in16 = w_in.astype(BF16)
    w_gdn = jnp.pad(w_in16[:, :o_mq], ((0, 0), (0, LANES - 2 * GDN_HEADS)))
    w_mq = w_in16[:, o_mq:o_mq + MOBA_W]
    w_mk = w_in16[:, o_mq + MOBA_W:o_mq + 2 * MOBA_W]
    w_mv = w_in16[:, o_mq + 2 * MOBA_W:o_mq + 3 * MOBA_W]

    gproj, xn = _norm_proj(x2, attn_norm, w_gdn, F32, tiles.rows, w_gdn.shape[1] // 3, "proj_gdn")
    mq = _proj_moba_q(xn, w_mq, q_norm, tiles.rows, tiles.cols)
    kaug, kmean = _proj_moba_k(xn, w_mk, k_norm, tiles.rows, tiles.cols, t // MOBA_BLOCK)
    mv = _proj(xn, w_mv, BF16, tiles.rows, tiles.cols, "proj_moba_v")

    o_gdn = _gdn_all(gproj.reshape(b, t, 4 * GDN_W + LANES), conv_w.T, a_log, dt_bias, gdn_norm, tiles.gdn_time)
    k_norm_bound = jnp.full((1, LANES), MOBA_DH ** 0.5, F32) * jnp.max(jnp.abs(k_norm))
    qaug, max_bound = _moba_select(mq.reshape(b, t, MOBA_W).swapaxes(1, 2),
                                   kmean.reshape(b, t // MOBA_BLOCK, MOBA_W), k_norm_bound, tiles.select_queries)
    o_moba = _moba(qaug, kaug.reshape(b, t, 2 * MOBA_W), mv.reshape(b, t, MOBA_W).swapaxes(1, 2), max_bound)

    h1, hn = _oproj(o_gdn.reshape(m, GDN_W), o_moba.reshape(m, MOBA_W), w_o.astype(BF16), x2, ffn_norm,
                    tiles.wide_rows)
    h2, hn2 = _ffn(hn, w_gate.astype(BF16), w_up.astype(BF16), w_down.astype(BF16), h1, ple_norm,
                   tiles.wide_rows, tiles.ffn_cols)
    h3 = _ple(hn2, w_ple_gate.astype(BF16), p.reshape(m, PLE_DIM), w_ple_proj.astype(BF16), h2,
              tiles.rows, tiles.cols)
    return h3.reshape(b, t, d)


def kernel(x, p, attn_norm, w_in, conv_w, A_log, dt_bias, gdn_norm, q_norm, k_norm, w_o, ffn_norm,
           w_gate, w_up, w_down, ple_norm, w_ple_gate, w_ple_proj):
    h = x
    for i in range(p.shape[0]):
        h = _layer(h, p[i], attn_norm[i], w_in[i], conv_w[i], A_log[i], dt_bias[i], gdn_norm[i],
                   q_norm[i], k_norm[i], w_o[i], ffn_norm[i], w_gate[i], w_up[i], w_down[i],
                   ple_norm[i], w_ple_gate[i], w_ple_proj[i])
    return h
```
